```python
import jax, jax.numpy as jnp
from jax import lax
import numpy as np

D_MODEL = 2048
BATCH = 8
SEQ = 8192
DEPTH = 4

HEAD_DIM = 128
N_HEADS_A = 12
N_HEADS_B = 12
WIDTH_A = N_HEADS_A * HEAD_DIM
WIDTH_B = N_HEADS_B * HEAD_DIM
DILATED_PATTERNS = ((128, 1), (512, 4), (2048, 16))
Q_BLOCK = 128
ROPE_THETA = 10000.0
LN_EPS = 1e-5
DEEPNORM_ALPHA = float((2 * DEPTH) ** 0.25)
DEEPNORM_BETA = float((8 * DEPTH) ** -0.25)
FORGET_BIAS_INIT = 2.0

SPLIT_SIZES = (WIDTH_A, WIDTH_A, WIDTH_A, WIDTH_A,
               WIDTH_B, WIDTH_B, WIDTH_B, WIDTH_B,
               N_HEADS_B,
               2 * D_MODEL)
SPLIT_POINTS = tuple(int(v) for v in np.cumsum(SPLIT_SIZES)[:-1])
N_IN_COLS = int(sum(SPLIT_SIZES))

kernel_name = "hybrid_dilated_forgetting_attention_deepnorm"


def rope(t, pos):
    half = HEAD_DIM // 2
    inv_freq = ROPE_THETA ** (-jnp.arange(half, dtype=jnp.float32) / half)
    ang = pos.astype(jnp.float32)[:, None] * inv_freq[None, :]
    cos = jnp.cos(ang)[None, :, None, :]
    sin = jnp.sin(ang)[None, :, None, :]
    t32 = t.astype(jnp.float32)
    t1, t2 = t32[..., :half], t32[..., half:]
    return jnp.concatenate([t1 * cos - t2 * sin, t1 * sin + t2 * cos], axis=-1).astype(t.dtype)


def dilated_pattern(q, k, v, window, dilation):
    B, S, H, hd = q.shape
    span = window // dilation
    L = S // dilation
    nblk = -(-L // Q_BLOCK)
    Lp = nblk * Q_BLOCK

    def to_residue(t):
        t = t.reshape(B, L, dilation, H, hd).transpose(0, 2, 1, 3, 4)
        return jnp.pad(t, ((0, 0), (0, 0), (0, Lp - L), (0, 0), (0, 0)))

    def band_keys(t):
        tp = jnp.pad(t, ((0, 0), (0, 0), (Q_BLOCK, 0), (0, 0), (0, 0)))
        tp = tp.reshape(B, dilation, nblk + 1, Q_BLOCK, H, hd)
        return jnp.concatenate([tp[:, :, :-1], tp[:, :, 1:]], axis=3)

    qb = to_residue(q).reshape(B, dilation, nblk, Q_BLOCK, H, hd)
    kb = band_keys(to_residue(k))
    vb = band_keys(to_residue(v))

    n_idx = jnp.arange(nblk)[:, None, None]
    i_idx = jnp.arange(Q_BLOCK)[None, :, None]
    j_idx = jnp.arange(2 * Q_BLOCK)[None, None, :]
    dist = Q_BLOCK + i_idx - j_idx
    valid = (dist >= 0) & (dist <= span) & (n_idx * Q_BLOCK + j_idx - Q_BLOCK >= 0)
    valid = valid[None, None, :, None]

    scale = HEAD_DIM ** -0.5
    s = jnp.einsum('bdnihe,bdnjhe->bdnhij', qb, kb, preferred_element_type=jnp.float32) * scale
    s = jnp.where(valid, s, -jnp.inf)
    m = jnp.max(s, axis=-1, keepdims=True)
    p = jnp.exp(s - m)
    den = jnp.sum(p, axis=-1, keepdims=True)
    out = jnp.einsum('bdnhij,bdnjhe->bdnihe', p, vb.astype(jnp.float32))
    out = out / jnp.moveaxis(den, 3, 4)
    lse = jnp.moveaxis((m + jnp.log(den))[..., 0], 3, 4)

    out = out.reshape(B, dilation, Lp, H, hd)[:, :, :L].transpose(0, 2, 1, 3, 4).reshape(B, S, H, hd)
    lse = lse.reshape(B, dilation, Lp, H)[:, :, :L].transpose(0, 2, 1, 3).reshape(B, S, H)
    return out, lse


def dilated_mixture(q, k, v):
    outs, lses = [], []
    for window, dilation in DILATED_PATTERNS:
        o, l = dilated_pattern(q, k, v, window, dilation)
        outs.append(o)
        lses.append(l)
    w = jax.nn.softmax(jnp.stack(lses, axis=0), axis=0)
    return jnp.sum(w[..., None] * jnp.stack(outs, axis=0), axis=0).astype(q.dtype)


def forgetting_attention(q, k, v, log_f):
    B, S, H, hd = q.shape
    nblk = S // Q_BLOCK
    cum = jnp.cumsum(log_f, axis=1).transpose(0, 2, 1)
    q_blocks = q.reshape(B, nblk, Q_BLOCK, H, hd).transpose(1, 0, 2, 3, 4)
    c_blocks = cum.reshape(B, H, nblk, Q_BLOCK).transpose(2, 0, 1, 3)
    kpos = jnp.arange(S)
    scale = HEAD_DIM ** -0.5

    def block(args):
        n, qn, cn = args
        s = jnp.einsum('bihe,bjhe->bhij', qn, k, preferred_element_type=jnp.float32) * scale
        s = s + cn[..., :, None] - cum[..., None, :]
        qpos = n * Q_BLOCK + jnp.arange(Q_BLOCK)
        causal = kpos[None, :] <= qpos[:, None]
        s = jnp.where(causal[None, None], s, -jnp.inf)
        p = jax.nn.softmax(s, axis=-1)
        return jnp.einsum('bhij,bjhe->bihe', p, v.astype(jnp.float32)).astype(q.dtype)

    out = lax.map(block, (jnp.arange(nblk), q_blocks, c_blocks))
    return out.transpose(1, 0, 2, 3, 4).reshape(B, S, H, hd)


def layer_norm(x, g, b):
    x32 = x.astype(jnp.float32)
    mu = jnp.mean(x32, axis=-1, keepdims=True)
    var = jnp.mean(jnp.square(x32 - mu), axis=-1, keepdims=True)
    y = (x32 - mu) * lax.rsqrt(var + LN_EPS)
    return (y * g.astype(jnp.float32) + b.astype(jnp.float32)).astype(x.dtype)


def hybrid_layer(x, pos, w_in, b_forget, b_gate, w_up_a, w_up_b, w_out, ln_g, ln_b):
    B, S, _ = x.shape
    h = jnp.einsum('bsd,dc->bsc', x, w_in)
    qa, ka, va, za, qb, kb, vb, zb, f_logit, g_logit = jnp.split(h, SPLIT_POINTS, axis=-1)
    heads = lambda t, n: t.reshape(B, S, n, HEAD_DIM)

    qa = rope(heads(qa, N_HEADS_A), pos)
    ka = rope(heads(ka, N_HEADS_A), pos)
    out_a = dilated_mixture(qa, ka, heads(va, N_HEADS_A)).reshape(B, S, WIDTH_A)
    up_a = jnp.einsum('bsw,wd->bsd', out_a * jax.nn.silu(za), w_up_a)

    log_f = jax.nn.log_sigmoid((f_logit + b_forget).astype(jnp.float32))
    out_b = forgetting_attention(heads(qb, N_HEADS_B), heads(kb, N_HEADS_B),
                                 heads(vb, N_HEADS_B), log_f).reshape(B, S, WIDTH_B)
    up_b = jnp.einsum('bsw,wd->bsd', out_b * jax.nn.silu(zb), w_up_b)

    gates = jax.nn.sigmoid(g_logit + b_gate)
    g_a, g_b = gates[..., :D_MODEL], gates[..., D_MODEL:]
    y = jnp.einsum('bsd,de->bse', g_a * up_a + g_b * up_b, w_out)

    return layer_norm(DEEPNORM_ALPHA * x + y, ln_g, ln_b)


def _fwd_setup_inputs(seed: int = 0) -> dict:
    key = jax.random.key(seed)
    ks = jax.random.split(key, 10)
    x = jax.random.normal(ks[0], (BATCH, SEQ, D_MODEL), jnp.float32)

    offs = (0,) + SPLIT_POINTS
    col_scale = jnp.ones((N_IN_COLS,), jnp.float32)
    col_scale = col_scale.at[offs[2]:offs[3]].set(DEEPNORM_BETA)
    col_scale = col_scale.at[offs[6]:offs[7]].set(DEEPNORM_BETA)
    w_in = jax.random.normal(ks[1], (DEPTH, D_MODEL, N_IN_COLS), jnp.float32) * (D_MODEL ** -0.5) * col_scale
    b_forget = FORGET_BIAS_INIT + 0.1 * jax.random.normal(ks[2], (DEPTH, N_HEADS_B), jnp.float32)
    b_gate = 0.02 * jax.random.normal(ks[3], (DEPTH, 2 * D_MODEL), jnp.float32)
    w_up_a = jax.random.normal(ks[4], (DEPTH, WIDTH_A, D_MODEL), jnp.float32) * (WIDTH_A ** -0.5) * DEEPNORM_BETA
    w_up_b = jax.random.normal(ks[5], (DEPTH, WIDTH_B, D_MODEL), jnp.float32) * (WIDTH_B ** -0.5) * DEEPNORM_BETA
    w_out = jax.random.normal(ks[6], (DEPTH, D_MODEL, D_MODEL), jnp.float32) * (D_MODEL ** -0.5) * DEEPNORM_BETA
    ln_g = 1.0 + 0.02 * jax.random.normal(ks[7], (DEPTH, D_MODEL), jnp.float32)
    ln_b = 0.02 * jax.random.normal(ks[8], (DEPTH, D_MODEL), jnp.float32)
    return {"x": x, "w_in": w_in, "b_forget": b_forget, "b_gate": b_gate,
            "w_up_a": w_up_a, "w_up_b": w_up_b, "w_out": w_out,
            "ln_g": ln_g, "ln_b": ln_b}


def _fwd_reference(x, w_in, b_forget, b_gate, w_up_a, w_up_b, w_out, ln_g, ln_b):
    pos = jnp.arange(x.shape[1], dtype=jnp.int32)
    for l in range(DEPTH):
        x = hybrid_layer(x, pos, w_in[l], b_forget[l], b_gate[l],
                         w_up_a[l], w_up_b[l], w_out[l], ln_g[l], ln_b[l])
    return x


import jax as _jax
import jax.numpy as _jnp

TWIN_FORMAT = 'train_step'
FWD_PARAMS = ['x', 'w_in', 'b_forget', 'b_gate', 'w_up_a', 'w_up_b', 'w_out', 'ln_g', 'ln_b']
TWIN_WEIGHTS = ['w_in', 'b_forget', 'b_gate', 'w_up_a', 'w_up_b', 'w_out', 'ln_g', 'ln_b']
TWIN_DIFF_INPUT = 'x'
TWIN_INPUTS = ['x', 'w_in', 'b_forget', 'b_gate', 'w_up_a', 'w_up_b', 'w_out', 'ln_g', 'ln_b', 'loss_target', 'm_w_in', 'm_b_forget', 'm_b_gate', 'm_w_up_a', 'm_w_up_b', 'm_w_out', 'm_ln_g', 'm_ln_b', 'v_w_in', 'v_b_forget', 'v_b_gate', 'v_w_up_a', 'v_w_up_b', 'v_w_out', 'v_ln_g', 'v_ln_b']
TWIN_OUTPUTS = ['loss', 'grad_x', 'grad_w_in', 'grad_b_forget', 'grad_b_gate', 'grad_w_up_a', 'grad_w_up_b', 'grad_w_out', 'grad_ln_g', 'grad_ln_b', 'delta_w_in', 'delta_b_forget', 'delta_b_gate', 'delta_w_up_a', 'delta_w_up_b', 'delta_w_out', 'delta_ln_g', 'delta_ln_b', 'new_m_w_in', 'new_m_b_forget', 'new_m_b_gate', 'new_m_w_up_a', 'new_m_w_up_b', 'new_m_w_out', 'new_m_ln_g', 'new_m_ln_b', 'new_v_w_in', 'new_v_b_forget', 'new_v_b_gate', 'new_v_w_up_a', 'new_v_w_up_b', 'new_v_w_out', 'new_v_ln_g', 'new_v_ln_b']
TWIN_LEAF_KINDS = {'loss': 'loss', 'grad_x': 'grad_x', 'grad_w_in': 'grad_w', 'grad_b_forget': 'grad_w', 'grad_b_gate': 'grad_w', 'grad_w_up_a': 'grad_w', 'grad_w_up_b': 'grad_w', 'grad_w_out': 'grad_w', 'grad_ln_g': 'grad_w', 'grad_ln_b': 'grad_w', 'delta_w_in': 'delta_w', 'delta_b_forget': 'delta_w', 'delta_b_gate': 'delta_w', 'delta_w_up_a': 'delta_w', 'delta_w_up_b': 'delta_w', 'delta_w_out': 'delta_w', 'delta_ln_g': 'delta_w', 'delta_ln_b': 'delta_w', 'new_m_w_in': 'new_m', 'new_m_b_forget': 'new_m', 'new_m_b_gate': 'new_m', 'new_m_w_up_a': 'new_m', 'new_m_w_up_b': 'new_m', 'new_m_w_out': 'new_m', 'new_m_ln_g': 'new_m', 'new_m_ln_b': 'new_m', 'new_v_w_in': 'new_v', 'new_v_b_forget': 'new_v', 'new_v_b_gate': 'new_v', 'new_v_w_up_a': 'new_v', 'new_v_w_up_b': 'new_v', 'new_v_w_out': 'new_v', 'new_v_ln_g': 'new_v', 'new_v_ln_b': 'new_v'}


def _forward(args):
    return _fwd_reference(*[args[k] for k in FWD_PARAMS])


def _output_shape():
    def fwd():
        inp = _fwd_setup_inputs(0)
        return _fwd_reference(*[inp[k] for k in FWD_PARAMS])
    out = _jax.eval_shape(fwd)
    return out.shape, out.dtype

N_MICROBATCH = 1
ADAM_LR = 0.001
ADAM_B1 = 0.9
ADAM_B2 = 0.999
ADAM_EPS = 1e-08
ADAM_WD = 0.01
ADAM_STEP = 10
PER_EXAMPLE_BATCH_AXIS = {'x': 0, 'loss_target': 0}
SHARED_INPUTS = []
_WEIGHT_DTYPES = {'w_in': _jnp.float32, 'b_forget': _jnp.float32, 'b_gate': _jnp.float32, 'w_up_a': _jnp.float32, 'w_up_b': _jnp.float32, 'w_out': _jnp.float32, 'ln_g': _jnp.float32, 'ln_b': _jnp.float32}
MOMENT_SCALE = {'w_in': 8.618653e-04, 'b_forget': 5.166372e-03, 'b_gate': 2.360576e-04, 'w_up_a': 9.964723e-04, 'w_up_b': 1.778055e-03, 'w_out': 2.036235e-03, 'ln_g': 1.604849e+01, 'ln_b': 6.782336e-01}


def _to_microbatches(a, axis):
    t = _jnp.moveaxis(a, axis, 0)
    t = t.reshape((N_MICROBATCH, t.shape[0] // N_MICROBATCH) + t.shape[1:])
    return _jnp.moveaxis(t, 1, axis + 1)


def setup_inputs(seed: int = 0) -> dict:
    inp = _fwd_setup_inputs(seed)
    key = _jax.random.fold_in(_jax.random.key(seed), 7919)
    shape, _ = _output_shape()
    out = dict(inp)
    out["loss_target"] = _jax.random.normal(_jax.random.fold_in(key, 0), shape, _jnp.float32)
    for i, name in enumerate(TWIN_WEIGHTS):
        w = inp[name].astype(_jnp.float32)
        if MOMENT_SCALE is None:
            s = _jnp.sqrt(_jnp.mean(_jnp.square(w)) + 1e-30)
        else:
            s = MOMENT_SCALE[name]
        km, kv = _jax.random.split(_jax.random.fold_in(key, i + 1))
        out[name] = w
        out["m_" + name] = s * _jax.random.normal(km, w.shape, _jnp.float32)
        out["v_" + name] = (s * s) * _jax.random.uniform(kv, w.shape, _jnp.float32, 0.5, 1.5)
    if N_MICROBATCH > 1:
        for name, axis in PER_EXAMPLE_BATCH_AXIS.items():
            out[name] = _to_microbatches(out[name], axis)
    return {'x': out['x'], 'w_in': out['w_in'], 'b_forget': out['b_forget'], 'b_gate': out['b_gate'], 'w_up_a': out['w_up_a'], 'w_up_b': out['w_up_b'], 'w_out': out['w_out'], 'ln_g': out['ln_g'], 'ln_b': out['ln_b'], 'loss_target': out['loss_target'], 'm_w_in': out['m_w_in'], 'm_b_forget': out['m_b_forget'], 'm_b_gate': out['m_b_gate'], 'm_w_up_a': out['m_w_up_a'], 'm_w_up_b': out['m_w_up_b'], 'm_w_out': out['m_w_out'], 'm_ln_g': out['m_ln_g'], 'm_ln_b': out['m_ln_b'], 'v_w_in': out['v_w_in'], 'v_b_forget': out['v_b_forget'], 'v_b_gate': out['v_b_gate'], 'v_w_up_a': out['v_w_up_a'], 'v_w_up_b': out['v_w_up_b'], 'v_w_out': out['v_w_out'], 'v_ln_g': out['v_ln_g'], 'v_ln_b': out['v_ln_b']}


def _loss(weights, diff, rest, loss_target):
    with _jax.named_scope("forward"):
        args = {**rest, TWIN_DIFF_INPUT: diff, **{k: w.astype(_WEIGHT_DTYPES[k]) for k, w in weights.items()}}
        y = _forward(args)
    with _jax.named_scope("loss_head"):
        err = _jnp.square(y.astype(_jnp.float32) - loss_target)
        return 0.5 * _jnp.sum(_jnp.mean(err, axis=-1)) if err.ndim else 0.5 * err


def _adamw(w, g, m, v):
    m = ADAM_B1 * m + (1.0 - ADAM_B1) * g
    v = ADAM_B2 * v + (1.0 - ADAM_B2) * _jnp.square(g)
    m_hat = m / (1.0 - ADAM_B1 ** ADAM_STEP)
    v_hat = v / (1.0 - ADAM_B2 ** ADAM_STEP)
    delta = -ADAM_LR * (m_hat / (_jnp.sqrt(v_hat) + ADAM_EPS) + ADAM_WD * w)
    return delta, m, v


def reference(x, w_in, b_forget, b_gate, w_up_a, w_up_b, w_out, ln_g, ln_b, loss_target, m_w_in, m_b_forget, m_b_gate, m_w_up_a, m_w_up_b, m_w_out, m_ln_g, m_ln_b, v_w_in, v_b_forget, v_b_gate, v_w_up_a, v_w_up_b, v_w_out, v_ln_g, v_ln_b):
    given = dict(x=x, w_in=w_in, b_forget=b_forget, b_gate=b_gate, w_up_a=w_up_a, w_up_b=w_up_b, w_out=w_out, ln_g=ln_g, ln_b=ln_b, loss_target=loss_target, m_w_in=m_w_in, m_b_forget=m_b_forget, m_b_gate=m_b_gate, m_w_up_a=m_w_up_a, m_w_up_b=m_w_up_b, m_w_out=m_w_out, m_ln_g=m_ln_g, m_ln_b=m_ln_b, v_w_in=v_w_in, v_b_forget=v_b_forget, v_b_gate=v_b_gate, v_w_up_a=v_w_up_a, v_w_up_b=v_w_up_b, v_w_out=v_w_out, v_ln_g=v_ln_g, v_ln_b=v_ln_b)
    weights = {n: given[n] for n in TWIN_WEIGHTS}
    shared = {n: given[n] for n in SHARED_INPUTS}
    per_example = {n: given[n] for n in ['x']}
    grad_fn = _jax.value_and_grad(_loss, argnums=(0, 1))

    def one_microbatch(ex, loss_target):
        ex = dict(ex)
        diff = ex.pop(TWIN_DIFF_INPUT)
        return grad_fn(weights, diff, {**shared, **ex}, loss_target)

    if N_MICROBATCH == 1:
        loss, (grad_w, grad_x) = one_microbatch(per_example, given["loss_target"])
    else:
        def body(carry, xs):
            loss_sum, grad_sum = carry
            l_k, (gw_k, gx_k) = one_microbatch(xs[0], xs[1])
            with _jax.named_scope("update"):
                return (loss_sum + l_k, _jax.tree.map(_jnp.add, grad_sum, gw_k)), gx_k

        init = (_jnp.zeros((), _jnp.float32), _jax.tree.map(_jnp.zeros_like, weights))
        (loss, grad_w), grad_x = _jax.lax.scan(body, init, (per_example, given["loss_target"]))
    with _jax.named_scope("update"):
        delta_w, new_m, new_v = {}, {}, {}
        for n in TWIN_WEIGHTS:
            delta_w[n], new_m[n], new_v[n] = _adamw(weights[n], grad_w[n], given["m_" + n], given["v_" + n])
    return (loss, grad_x, *[grad_w[n] for n in TWIN_WEIGHTS], *[delta_w[n] for n in TWIN_WEIGHTS],
            *[new_m[n] for n in TWIN_WEIGHTS], *[new_v[n] for n in TWIN_WEIGHTS])
```

```python
import functools

import jax
import jax.numpy as jnp
from jax import lax
from jax.experimental import pallas as pl
from jax.experimental.pallas import tpu as pltpu

F32 = jnp.float32
BF16 = jnp.bfloat16
MESH = pl.DeviceIdType.MESH
ANY = pl.BlockSpec(memory_space=pl.ANY)

HEAD_DIM = 128
LANES = 128
Q_BLOCK = 128
DILATED_PATTERNS = ((128, 1), (512, 4), (2048, 16))
ROPE_THETA = 10000.0
LN_EPS = 1e-5
ADAM_LR, ADAM_B1, ADAM_B2, ADAM_EPS, ADAM_WD, ADAM_STEP = 0.001, 0.9, 0.999, 1e-08, 0.01, 10
NEG = -1e30
VMEM_LIMIT = 56 * 2**20
ELEMENTWISE_BUDGET = 20 * 2**20
FOX_TILE = 512
MM_TILES = (1024, 1024, 512)


def _params():
    return pltpu.CompilerParams(vmem_limit_bytes=VMEM_LIMIT)


def _tile(dim, target, align):
    if dim <= target:
        return dim
    t = (target // align) * align
    while t >= align:
        if dim % t == 0:
            return t
        t -= align
    return dim


def _rows(n_rows, bytes_per_row, align=16):
    return _tile(n_rows, max(align, ELEMENTWISE_BUDGET // (2 * bytes_per_row)), align)


def _sigmoid(v):
    return 1.0 / (1.0 + jnp.exp(-v))


def _matmul(a, b, *, mode, out_dtype, name, acc_in=None, acc_scale=1.0, rope=None, rope_cols=0):
    if mode == "nn":
        (M, K), (K2, N) = a.shape, b.shape
    elif mode == "nt":
        (M, K), (N, K2) = a.shape, b.shape
    else:
        (K, M), (K2, N) = a.shape, b.shape
    assert K == K2, (a.shape, b.shape, mode)
    tm, tn, tk = _tile(M, MM_TILES[0], 128), _tile(N, MM_TILES[1], 128), _tile(K, MM_TILES[2], 128)
    if rope is not None:
        assert rope_cols % tn == 0
    nk = K // tk
    n_rope_tiles = rope_cols // tn if rope is not None else 0
    dims = {"nn": (((1,), (0,)), ((), ())), "nt": (((1,), (1,)), ((), ())), "tn": (((0,), (0,)), ((), ()))}[mode]

    def body(*refs):
        a_ref, b_ref = refs[0], refs[1]
        pos = 2
        if rope is not None:
            cos_ref, sin_ref = refs[pos], refs[pos + 1]
            pos += 2
        if acc_in is not None:
            acc_in_ref = refs[pos]
            pos += 1
        o_ref, acc_ref = refs[pos], refs[pos + 1]
        j, k = pl.program_id(1), pl.program_id(2)

        @pl.when(k == 0)
        def _():
            acc_ref[...] = jnp.zeros_like(acc_ref)

        acc_ref[...] += lax.dot_general(a_ref[...], b_ref[...], dims, preferred_element_type=F32)

        def finish(rotate):
            r = acc_ref[...]
            if acc_in is not None:
                r = r + acc_scale * acc_in_ref[...]
            if rotate:
                cos, sin = cos_ref[...], sin_ref[...]
                for g in range(tn // HEAD_DIM):
                    sl = slice(g * HEAD_DIM, (g + 1) * HEAD_DIM)
                    t = r[:, sl]
                    o_ref[:, sl] = (t * cos + pltpu.roll(t, HEAD_DIM // 2, 1) * sin).astype(o_ref.dtype)
            else:
                o_ref[...] = r.astype(o_ref.dtype)

        if n_rope_tiles:
            @pl.when((k == nk - 1) & (j < n_rope_tiles))
            def _():
                finish(True)

            @pl.when((k == nk - 1) & (j >= n_rope_tiles))
            def _():
                finish(False)
        else:
            @pl.when(k == nk - 1)
            def _():
                finish(False)

    if mode == "nn":
        in_specs = [pl.BlockSpec((tm, tk), lambda i, j, k: (i, k)), pl.BlockSpec((tk, tn), lambda i, j, k: (k, j))]
    elif mode == "nt":
        in_specs = [pl.BlockSpec((tm, tk), lambda i, j, k: (i, k)), pl.BlockSpec((tn, tk), lambda i, j, k: (j, k))]
    else:
        in_specs = [pl.BlockSpec((tk, tm), lambda i, j, k: (k, i)), pl.BlockSpec((tk, tn), lambda i, j, k: (k, j))]
    args = [a, b]
    if rope is not None:
        in_specs += [pl.BlockSpec((tm, HEAD_DIM), lambda i, j, k: (i, 0))] * 2
        args += list(rope)
    if acc_in is not None:
        in_specs.append(pl.BlockSpec((tm, tn), lambda i, j, k: (i, j)))
        args.append(acc_in)
    return pl.pallas_call(
        body, name=name, grid=(M // tm, N // tn, nk), in_specs=in_specs,
        out_specs=pl.BlockSpec((tm, tn), lambda i, j, k: (i, j)),
        out_shape=jax.ShapeDtypeStruct((M, N), out_dtype),
        scratch_shapes=[pltpu.VMEM((tm, tn), F32)], compiler_params=_params(),
    )(*args)


def _ln_fwd(r, g, b):
    S, D = r.shape
    tm = _rows(S, D * (4 + 4 + 2))

    def body(r_ref, g_ref, b_ref, x_ref, xb_ref):
        v = r_ref[...]
        mu = jnp.mean(v, axis=1, keepdims=True)
        cen = v - mu
        var = jnp.mean(cen * cen, axis=1, keepdims=True)
        out = cen * lax.rsqrt(var + LN_EPS) * g_ref[...] + b_ref[...]
        x_ref[...] = out
        xb_ref[...] = out.astype(BF16)

    row = pl.BlockSpec((tm, D), lambda i: (i, 0))
    vec = pl.BlockSpec((1, D), lambda i: (0, 0))
    return pl.pallas_call(
        body, name="ln_fwd", grid=(S // tm,), in_specs=[row, vec, vec], out_specs=[row, row],
        out_shape=[jax.ShapeDtypeStruct((S, D), F32), jax.ShapeDtypeStruct((S, D), BF16)],
        compiler_params=_params(),
    )(r, g, b)


def _ln_bwd(dx, r, g, alpha):
    S, D = r.shape
    tm = _rows(S, D * (4 + 4 + 2 + 4))

    def body(dx_ref, r_ref, g_ref, drb_ref, adr_ref, dg_ref, db_ref):
        @pl.when(pl.program_id(0) == 0)
        def _():
            dg_ref[...] = jnp.zeros_like(dg_ref)
            db_ref[...] = jnp.zeros_like(db_ref)

        v, d = r_ref[...], dx_ref[...]
        mu = jnp.mean(v, axis=1, keepdims=True)
        cen = v - mu
        var = jnp.mean(cen * cen, axis=1, keepdims=True)
        rstd = lax.rsqrt(var + LN_EPS)
        xhat = cen * rstd
        dxhat = d * g_ref[...]
        dr = rstd * (dxhat - jnp.mean(dxhat, axis=1, keepdims=True)
                     - xhat * jnp.mean(dxhat * xhat, axis=1, keepdims=True))
        drb_ref[...] = dr.astype(BF16)
        adr_ref[...] = alpha * dr
        dg_ref[...] += jnp.sum(d * xhat, axis=0, keepdims=True)
        db_ref[...] += jnp.sum(d, axis=0, keepdims=True)

    row = pl.BlockSpec((tm, D), lambda i: (i, 0))
    vec = pl.BlockSpec((1, D), lambda i: (0, 0))
    return pl.pallas_call(
        body, name="ln_bwd", grid=(S // tm,), in_specs=[row, row, vec], out_specs=[row, row, vec, vec],
        out_shape=[jax.ShapeDtypeStruct((S, D), BF16), jax.ShapeDtypeStruct((S, D), F32),
                   jax.ShapeDtypeStruct((1, D), F32), jax.ShapeDtypeStruct((1, D), F32)],
        compiler_params=_params(),
    )(dx, r, g)


def _loss(y, target):
    S, D = y.shape
    tm = _rows(S, D * 12)

    def body(y_ref, t_ref, dy_ref, sq_ref):
        @pl.when(pl.program_id(0) == 0)
        def _():
            sq_ref[...] = jnp.zeros_like(sq_ref)

        err = y_ref[...] - t_ref[...]
        dy_ref[...] = err * (1.0 / D)
        sq_ref[...] += jnp.sum(err * err, axis=0, keepdims=True)

    row = pl.BlockSpec((tm, D), lambda i: (i, 0))
    vec = pl.BlockSpec((1, D), lambda i: (0, 0))
    return pl.pallas_call(
        body, name="loss", grid=(S // tm,), in_specs=[row, row], out_specs=[row, vec],
        out_shape=[jax.ShapeDtypeStruct((S, D), F32), jax.ShapeDtypeStruct((1, D), F32)],
        compiler_params=_params(),
    )(y, target)


def _merge_fwd(up_a, up_b, hfg, b_gate):
    S, D = up_a.shape
    tm = _rows(S, D * (2 + 2 + 4 + 4 + 2))

    def body(ua_ref, ub_ref, gla_ref, glb_ref, bga_ref, bgb_ref, u_ref):
        ga = _sigmoid(gla_ref[...] + bga_ref[...])
        gb = _sigmoid(glb_ref[...] + bgb_ref[...])
        u_ref[...] = (ga * ua_ref[...].astype(F32) + gb * ub_ref[...].astype(F32)).astype(BF16)

    row = pl.BlockSpec((tm, D), lambda i: (i, 0))
    row1 = pl.BlockSpec((tm, D), lambda i: (i, 1))
    v0 = pl.BlockSpec((1, D), lambda i: (0, 0))
    v1 = pl.BlockSpec((1, D), lambda i: (0, 1))
    return pl.pallas_call(
        body, name="merge_fwd", grid=(S // tm,), in_specs=[row, row, row, row1, v0, v1], out_specs=row,
        out_shape=jax.ShapeDtypeStruct((S, D), BF16), compiler_params=_params(),
    )(up_a, up_b, hfg, hfg, b_gate, b_gate)


def _merge_bwd(du, up_a, up_b, hfg, b_gate):
    S, D = up_a.shape
    tm = _rows(S, D * (4 + 2 + 2 + 4 + 4 + 2 + 2 + 4))

    def body(du_ref, ua_ref, ub_ref, gla_ref, glb_ref, bga_ref, bgb_ref, dua_ref, dub_ref, dgl_ref, dbg_ref):
        @pl.when(pl.program_id(0) == 0)
        def _():
            dbg_ref[...] = jnp.zeros_like(dbg_ref)

        du = du_ref[...]
        ga = _sigmoid(gla_ref[...] + bga_ref[...])
        gb = _sigmoid(glb_ref[...] + bgb_ref[...])
        dua_ref[...] = (du * ga).astype(BF16)
        dub_ref[...] = (du * gb).astype(BF16)
        dgla = du * ua_ref[...].astype(F32) * ga * (1.0 - ga)
        dglb = du * ub_ref[...].astype(F32) * gb * (1.0 - gb)
        dgl_ref[:, :D] = dgla.astype(BF16)
        dgl_ref[:, D:] = dglb.astype(BF16)
        dbg_ref[:, :D] += jnp.sum(dgla, axis=0, keepdims=True)
        dbg_ref[:, D:] += jnp.sum(dglb, axis=0, keepdims=True)

    row = pl.BlockSpec((tm, D), lambda i: (i, 0))
    row1 = pl.BlockSpec((tm, D), lambda i: (i, 1))
    v0 = pl.BlockSpec((1, D), lambda i: (0, 0))
    v1 = pl.BlockSpec((1, D), lambda i: (0, 1))
    return pl.pallas_call(
        body, name="merge_bwd", grid=(S // tm,), in_specs=[row, row, row, row, row1, v0, v1],
        out_specs=[row, row, pl.BlockSpec((tm, 2 * D), lambda i: (i, 0)), pl.BlockSpec((1, 2 * D), lambda i: (0, 0))],
        out_shape=[jax.ShapeDtypeStruct((S, D), BF16), jax.ShapeDtypeStruct((S, D), BF16),
                   jax.ShapeDtypeStruct((S, 2 * D), BF16), jax.ShapeDtypeStruct((1, 2 * D), F32)],
        compiler_params=_params(),
    )(du, up_a, up_b, hfg, hfg, b_gate, b_gate)


def _gate_bwd(dg, out, h, z_block, name):
    S, W = out.shape
    tm = _rows(S, W * (4 + 2 + 2 + 2 + 2 + 4))

    def body(dg_ref, o_ref, z_ref, do_ref, dz_ref, dl_ref):
        z = z_ref[...].astype(F32)
        o = o_ref[...].astype(F32)
        d = dg_ref[...]
        sg = _sigmoid(z)
        dout = d * z * sg
        do_ref[...] = dout.astype(BF16)
        dz_ref[...] = (d * o * sg * (1.0 + z * (1.0 - sg))).astype(BF16)
        prod = dout * o
        for hh in range(W // HEAD_DIM):
            sl = slice(hh * HEAD_DIM, (hh + 1) * HEAD_DIM)
            dl_ref[:, sl] = jnp.broadcast_to(jnp.sum(prod[:, sl], axis=1, keepdims=True), (tm, HEAD_DIM))

    row = pl.BlockSpec((tm, W), lambda i: (i, 0))
    return pl.pallas_call(
        body, name=name, grid=(S // tm,),
        in_specs=[row, row, pl.BlockSpec((tm, W), lambda i: (i, z_block))], out_specs=[row, row, row],
        out_shape=[jax.ShapeDtypeStruct((S, W), BF16), jax.ShapeDtypeStruct((S, W), BF16),
                   jax.ShapeDtypeStruct((S, W), F32)],
        compiler_params=_params(),
    )(dg, out, h)


def _dil_masks(n):
    i = lax.broadcasted_iota(jnp.int32, (Q_BLOCK, Q_BLOCK), 0)
    j = lax.broadcasted_iota(jnp.int32, (Q_BLOCK, Q_BLOCK), 1)
    return j <= i, (j >= i) & (n > 0)


def _dil_fwd(hv, d, W, name):
    L = hv.shape[0]
    nblk = L // Q_BLOCK
    scale = HEAD_DIM ** -0.5
    nt = (((1,), (1,)), ((), ()))

    def body(q_ref, kp_ref, kc_ref, vp_ref, vc_ref, o_ref, lse_ref):
        mc, mp = _dil_masks(pl.program_id(1))
        for hh in range(W // HEAD_DIM):
            sl = slice(hh * HEAD_DIM, (hh + 1) * HEAD_DIM)
            q = q_ref[:, sl]
            sc = jnp.where(mc, lax.dot_general(q, kc_ref[:, sl], nt, preferred_element_type=F32) * scale, NEG)
            sp = jnp.where(mp, lax.dot_general(q, kp_ref[:, sl], nt, preferred_element_type=F32) * scale, NEG)
            m = jnp.maximum(jnp.max(sc, axis=1, keepdims=True), jnp.max(sp, axis=1, keepdims=True))
            pc, pp = jnp.exp(sc - m), jnp.exp(sp - m)
            den = jnp.sum(pc, axis=1, keepdims=True) + jnp.sum(pp, axis=1, keepdims=True)
            acc = (jnp.dot(pc.astype(BF16), vc_ref[:, sl], preferred_element_type=F32)
                   + jnp.dot(pp.astype(BF16), vp_ref[:, sl], preferred_element_type=F32))
            o_ref[:, sl] = acc / den
            lse_ref[:, sl] = jnp.broadcast_to(m + jnp.log(den), (Q_BLOCK, HEAD_DIM))

    def spec(col, prev):
        if prev:
            return pl.BlockSpec((Q_BLOCK, W), lambda r, n: (jnp.maximum(n - 1, 0), r * 8 + col))
        return pl.BlockSpec((Q_BLOCK, W), lambda r, n: (n, r * 8 + col))

    out = pl.BlockSpec((Q_BLOCK, W), lambda r, n: (n, r))
    return pl.pallas_call(
        body, name=name, grid=(d, nblk),
        in_specs=[spec(0, False), spec(1, True), spec(1, False), spec(2, True), spec(2, False)],
        out_specs=[out, out],
        out_shape=[jax.ShapeDtypeStruct((L, d * W), F32), jax.ShapeDtypeStruct((L, d * W), F32)],
        compiler_params=_params(),
    )(hv, hv, hv, hv, hv)


def _dil_combine_fwd(os, lses, h, W):
    S = h.shape[0]
    tm = _rows(S, W * (6 * 4 + 2 + 2 + 4 + 2))

    def body(o1, o2, o3, l1, l2, l3, z_ref, out_ref, lse_ref, g_ref):
        a, b, c = l1[...], l2[...], l3[...]
        m = jnp.maximum(jnp.maximum(a, b), c)
        ea, eb, ec = jnp.exp(a - m), jnp.exp(b - m), jnp.exp(c - m)
        den = ea + eb + ec
        out = (ea * o1[...] + eb * o2[...] + ec * o3[...]) / den
        z = z_ref[...].astype(F32)
        out_ref[...] = out.astype(BF16)
        lse_ref[...] = m + jnp.log(den)
        g_ref[...] = (out * z * _sigmoid(z)).astype(BF16)

    row = pl.BlockSpec((tm, W), lambda i: (i, 0))
    return pl.pallas_call(
        body, name="dil_combine_fwd", grid=(S // tm,),
        in_specs=[row] * 6 + [pl.BlockSpec((tm, W), lambda i: (i, 3))], out_specs=[row, row, row],
        out_shape=[jax.ShapeDtypeStruct((S, W), BF16), jax.ShapeDtypeStruct((S, W), F32),
                   jax.ShapeDtypeStruct((S, W), BF16)],
        compiler_params=_params(),
    )(*os, *lses, h)


def _dil_bwd(hv, dov, lsev, dlv, d, W, name):
    L = hv.shape[0]
    nblk = L // Q_BLOCK
    scale = HEAD_DIM ** -0.5
    nt = (((1,), (1,)), ((), ()))
    tn = (((0,), (0,)), ((), ()))

    def body(q_ref, kp_ref, kc_ref, vp_ref, vc_ref, do_ref, lse_ref, dl_ref, dq_ref, dk_ref, dv_ref, ck_ref, cv_ref):
        n = pl.program_id(1)

        @pl.when(n == 0)
        def _():
            ck_ref[...] = jnp.zeros_like(ck_ref)
            cv_ref[...] = jnp.zeros_like(cv_ref)

        @pl.when(n < nblk)
        def _():
            mc, mp = _dil_masks(n)
            for hh in range(W // HEAD_DIM):
                sl = slice(hh * HEAD_DIM, (hh + 1) * HEAD_DIM)
                q, do = q_ref[:, sl], do_ref[:, sl]
                kc, kp, vc, vp = kc_ref[:, sl], kp_ref[:, sl], vc_ref[:, sl], vp_ref[:, sl]
                lse, dl = lse_ref[:, sl][:, :1], dl_ref[:, sl][:, :1]
                sc = jnp.where(mc, lax.dot_general(q, kc, nt, preferred_element_type=F32) * scale, NEG)
                sp = jnp.where(mp, lax.dot_general(q, kp, nt, preferred_element_type=F32) * scale, NEG)
                pc, pp = jnp.exp(sc - lse), jnp.exp(sp - lse)
                dsc = pc * (lax.dot_general(do, vc, nt, preferred_element_type=F32) - dl) * scale
                dsp = pp * (lax.dot_general(do, vp, nt, preferred_element_type=F32) - dl) * scale
                dsc_b, dsp_b = dsc.astype(BF16), dsp.astype(BF16)
                dq_ref[:, sl] = (jnp.dot(dsc_b, kc, preferred_element_type=F32)
                                 + jnp.dot(dsp_b, kp, preferred_element_type=F32))
                dk_ref[:, sl] = ck_ref[:, sl] + lax.dot_general(dsp_b, q, tn, preferred_element_type=F32)
                dv_ref[:, sl] = cv_ref[:, sl] + lax.dot_general(pp.astype(BF16), do, tn, preferred_element_type=F32)
                ck_ref[:, sl] = lax.dot_general(dsc_b, q, tn, preferred_element_type=F32)
                cv_ref[:, sl] = lax.dot_general(pc.astype(BF16), do, tn, preferred_element_type=F32)

        @pl.when(n == nblk)
        def _():
            dk_ref[...] = ck_ref[...]
            dv_ref[...] = cv_ref[...]

    last = nblk - 1

    def hspec(col, prev):
        if prev:
            return pl.BlockSpec((Q_BLOCK, W), lambda r, n: (jnp.clip(n - 1, 0, last), r * 8 + col))
        return pl.BlockSpec((Q_BLOCK, W), lambda r, n: (jnp.minimum(n, last), r * 8 + col))

    cur = pl.BlockSpec((Q_BLOCK, W), lambda r, n: (jnp.minimum(n, last), r))
    lag = pl.BlockSpec((Q_BLOCK, W), lambda r, n: (jnp.maximum(n - 1, 0), r))
    shape = jax.ShapeDtypeStruct((L, d * W), F32)
    return pl.pallas_call(
        body, name=name, grid=(d, nblk + 1),
        in_specs=[hspec(0, False), hspec(1, True), hspec(1, False), hspec(2, True), hspec(2, False), cur, cur, cur],
        out_specs=[cur, lag, lag], out_shape=[shape, shape, shape],
        scratch_shapes=[pltpu.VMEM((Q_BLOCK, W), F32), pltpu.VMEM((Q_BLOCK, W), F32)],
        compiler_params=_params(),
    )(hv, hv, hv, hv, hv, dov, lsev, dlv)


def _dil_combine_bwd(dqs, dks, dvs, cos, sin):
    S, W = dqs[0].shape
    tm = _rows(S, W * (9 * 4 + 3 * 2))

    def body(q1, q2, q3, k1, k2, k3, v1, v2, v3, cos_ref, sin_ref, o_ref):
        cos_t, sin_t = cos_ref[...], -sin_ref[...]
        dq = q1[...] + q2[...] + q3[...]
        dk = k1[...] + k2[...] + k3[...]
        for hh in range(W // HEAD_DIM):
            sl = slice(hh * HEAD_DIM, (hh + 1) * HEAD_DIM)
            tq, tk = dq[:, sl], dk[:, sl]
            o_ref[:, hh * HEAD_DIM:(hh + 1) * HEAD_DIM] = (
                tq * cos_t + pltpu.roll(tq, HEAD_DIM // 2, 1) * sin_t).astype(BF16)
            o_ref[:, W + hh * HEAD_DIM:W + (hh + 1) * HEAD_DIM] = (
                tk * cos_t + pltpu.roll(tk, HEAD_DIM // 2, 1) * sin_t).astype(BF16)
        o_ref[:, 2 * W:] = (v1[...] + v2[...] + v3[...]).astype(BF16)

    row = pl.BlockSpec((tm, W), lambda i: (i, 0))
    tab = pl.BlockSpec((tm, HEAD_DIM), lambda i: (i, 0))
    return pl.pallas_call(
        body, name="dil_combine_bwd", grid=(S // tm,), in_specs=[row] * 9 + [tab, tab],
        out_specs=pl.BlockSpec((tm, 3 * W), lambda i: (i, 0)),
        out_shape=jax.ShapeDtypeStruct((S, 3 * W), BF16), compiler_params=_params(),
    )(*dqs, *dks, *dvs, cos, sin)


def _scan_tile(S):
    return _tile(S, 256, 8)


def _scan_fwd(hfg, bf_pad, f_block):
    S = hfg.shape[0]
    tm = _scan_tile(S)

    def body(f_ref, b_ref, c_ref, carry_ref):
        @pl.when(pl.program_id(0) == 0)
        def _():
            carry_ref[...] = jnp.zeros_like(carry_ref)

        v = f_ref[...] + b_ref[...]
        logf = jnp.minimum(v, 0.0) - jnp.log(1.0 + jnp.exp(-jnp.abs(v)))
        tri = (lax.broadcasted_iota(jnp.int32, (tm, tm), 1) <= lax.broadcasted_iota(jnp.int32, (tm, tm), 0)).astype(F32)
        c = jnp.dot(tri, logf, preferred_element_type=F32, precision=lax.Precision.HIGHEST) + carry_ref[...]
        c_ref[...] = c
        carry_ref[...] = c[tm - 1:tm, :]

    return pl.pallas_call(
        body, name="scan_fwd", grid=(S // tm,),
        in_specs=[pl.BlockSpec((tm, LANES), lambda i: (i, f_block)), pl.BlockSpec((1, LANES), lambda i: (0, 0))],
        out_specs=pl.BlockSpec((tm, LANES), lambda i: (i, 0)),
        out_shape=jax.ShapeDtypeStruct((S, LANES), F32),
        scratch_shapes=[pltpu.VMEM((1, LANES), F32)], compiler_params=_params(),
    )(hfg, bf_pad)


def _scan_bwd(dc, hfg, bf_pad, f_block, n_heads):
    S = hfg.shape[0]
    tm = _scan_tile(S)
    nt = S // tm

    def body(dc_ref, f_ref, b_ref, df_ref, db_ref, carry_ref):
        @pl.when(pl.program_id(0) == 0)
        def _():
            carry_ref[...] = jnp.zeros_like(carry_ref)
            db_ref[...] = jnp.zeros_like(db_ref)

        tri = (lax.broadcasted_iota(jnp.int32, (tm, tm), 1) >= lax.broadcasted_iota(jnp.int32, (tm, tm), 0)).astype(F32)
        dlogf = jnp.dot(tri, dc_ref[...], preferred_element_type=F32, precision=lax.Precision.HIGHEST) + carry_ref[...]
        carry_ref[...] = dlogf[0:1, :]
        v = f_ref[...] + b_ref[...]
        lane = lax.broadcasted_iota(jnp.int32, (tm, LANES), 1)
        df = jnp.where(lane < n_heads, dlogf * _sigmoid(-v), 0.0)
        df_ref[...] = df.astype(BF16)
        db_ref[...] += jnp.sum(df, axis=0, keepdims=True)

    return pl.pallas_call(
        body, name="scan_bwd", grid=(nt,),
        in_specs=[pl.BlockSpec((tm, LANES), lambda i: (nt - 1 - i, 0)),
                  pl.BlockSpec((tm, LANES), lambda i: (nt - 1 - i, f_block)),
                  pl.BlockSpec((1, LANES), lambda i: (0, 0))],
        out_specs=[pl.BlockSpec((tm, LANES), lambda i: (nt - 1 - i, 0)), pl.BlockSpec((1, LANES), lambda i: (0, 0))],
        out_shape=[jax.ShapeDtypeStruct((S, LANES), BF16), jax.ShapeDtypeStruct((1, LANES), F32)],
        scratch_shapes=[pltpu.VMEM((1, LANES), F32)], compiler_params=_params(),
    )(dc, hfg, bf_pad)


def _head_column(c_tile, h):
    lane = lax.broadcasted_iota(jnp.int32, c_tile.shape, 1)
    return jnp.sum(jnp.where(lane == h, c_tile, 0.0), axis=1, keepdims=True)


def _fox_fwd(h, c, ct, H):
    S = h.shape[0]
    W = H * HEAD_DIM
    T = _tile(S, FOX_TILE, 128)
    nq = S // T
    scale = HEAD_DIM ** -0.5
    nt = (((1,), (1,)), ((), ()))

    def body(q_ref, k_ref, v_ref, z_ref, c_ref, ct_ref, o_ref, g_ref, lse_ref, m_ref, l_ref, acc_ref, cc_ref):
        hh, i, j = pl.program_id(0), pl.program_id(1), pl.program_id(2)

        @pl.when(j == 0)
        def _():
            m_ref[...] = jnp.full_like(m_ref, NEG)
            l_ref[...] = jnp.zeros_like(l_ref)
            acc_ref[...] = jnp.zeros_like(acc_ref)
            cc_ref[...] = _head_column(c_ref[...], hh)

        def step(diag):
            s = lax.dot_general(q_ref[...], k_ref[...], nt, preferred_element_type=F32) * scale
            s = s + (cc_ref[...] - ct_ref[...])
            if diag:
                s = jnp.where(lax.broadcasted_iota(jnp.int32, (T, T), 1) <= lax.broadcasted_iota(jnp.int32, (T, T), 0), s, NEG)
            m_new = jnp.maximum(m_ref[...], jnp.max(s, axis=1, keepdims=True))
            a = jnp.exp(m_ref[...] - m_new)
            p = jnp.exp(s - m_new)
            l_ref[...] = a * l_ref[...] + jnp.sum(p, axis=1, keepdims=True)
            acc_ref[...] = a * acc_ref[...] + jnp.dot(p.astype(BF16), v_ref[...], preferred_element_type=F32)
            m_ref[...] = m_new

        @pl.when(j < i)
        def _():
            step(False)

        @pl.when(j == i)
        def _():
            step(True)

        @pl.when(j == nq - 1)
        def _():
            out = acc_ref[...] / l_ref[...]
            z = z_ref[...].astype(F32)
            o_ref[...] = out.astype(BF16)
            g_ref[...] = (out * z * _sigmoid(z)).astype(BF16)
            lse_ref[...] = jnp.broadcast_to(m_ref[...] + jnp.log(l_ref[...]), (T, HEAD_DIM))

    blk = lambda off: pl.BlockSpec((T, HEAD_DIM), lambda hh, i, j: (i, off + hh))
    kv = lambda off: pl.BlockSpec((T, HEAD_DIM), lambda hh, i, j: (jnp.minimum(j, i), off + hh))
    out = pl.BlockSpec((T, HEAD_DIM), lambda hh, i, j: (i, hh))
    return pl.pallas_call(
        body, name="fox_fwd", grid=(H, nq, nq),
        in_specs=[blk(4 * H), kv(5 * H), kv(6 * H), blk(7 * H),
                  pl.BlockSpec((T, LANES), lambda hh, i, j: (i, 0)),
                  pl.BlockSpec((None, 1, T), lambda hh, i, j: (hh, 0, jnp.minimum(j, i)))],
        out_specs=[out, out, out],
        out_shape=[jax.ShapeDtypeStruct((S, W), BF16), jax.ShapeDtypeStruct((S, W), BF16),
                   jax.ShapeDtypeStruct((S, W), F32)],
        scratch_shapes=[pltpu.VMEM((T, 1), F32), pltpu.VMEM((T, 1), F32), pltpu.VMEM((T, HEAD_DIM), F32),
                        pltpu.VMEM((T, 1), F32)],
        compiler_params=_params(),
    )(h, h, h, h, c, ct)


def _fox_bwd(h, do, lse, dl, c, ct, H):
    S = h.shape[0]
    W = H * HEAD_DIM
    T = _tile(S, FOX_TILE, 128)
    nq = S // T
    scale = HEAD_DIM ** -0.5
    nt = (((1,), (1,)), ((), ()))
    tn = (((0,), (0,)), ((), ()))

    def body(q_ref, k_ref, v_ref, do_ref, lse_ref, dl_ref, c_ref, ct_ref, dq_ref, dk_ref, dv_ref, dcq_ref, dc_ref,
             ak_ref, av_ref, ac_ref):
        hh, j, i = pl.program_id(0), pl.program_id(1), pl.program_id(2)

        @pl.when((j == 0) & (i == 0))
        def _():
            dq_ref[...] = jnp.zeros_like(dq_ref)
            dcq_ref[...] = jnp.zeros_like(dcq_ref)

        @pl.when(i == 0)
        def _():
            ak_ref[...] = jnp.zeros_like(ak_ref)
            av_ref[...] = jnp.zeros_like(av_ref)
            ac_ref[...] = jnp.zeros_like(ac_ref)

        def step(diag):
            q, k, v, d_o = q_ref[...], k_ref[...], v_ref[...], do_ref[...]
            s = lax.dot_general(q, k, nt, preferred_element_type=F32) * scale
            s = s + (_head_column(c_ref[...], hh) - ct_ref[...])
            if diag:
                s = jnp.where(lax.broadcasted_iota(jnp.int32, (T, T), 1) <= lax.broadcasted_iota(jnp.int32, (T, T), 0), s, NEG)
            p = jnp.exp(s - lse_ref[...][:, :1])
            dp = lax.dot_general(d_o, v, nt, preferred_element_type=F32)
            ds = p * (dp - dl_ref[...][:, :1])
            ds_b = (ds * scale).astype(BF16)
            av_ref[...] += lax.dot_general(p.astype(BF16), d_o, tn, preferred_element_type=F32)
            ak_ref[...] += lax.dot_general(ds_b, q, tn, preferred_element_type=F32)
            ac_ref[...] -= jnp.sum(ds, axis=0, keepdims=True)
            rows = pl.ds(pl.multiple_of(i * T, T), T)
            dq_ref[rows, :] += jnp.dot(ds_b, k, preferred_element_type=F32)
            dcq_ref[rows, :] += jnp.broadcast_to(jnp.sum(ds, axis=1, keepdims=True), (T, HEAD_DIM))

        @pl.when(i > j)
        def _():
            step(False)

        @pl.when(i == j)
        def _():
            step(True)

        @pl.when(i == nq - 1)
        def _():
            dk_ref[...] = ak_ref[...]
            dv_ref[...] = av_ref[...]
            dc_ref[...] = ac_ref[...]

    qrow = lambda off: pl.BlockSpec((T, HEAD_DIM), lambda hh, j, i: (jnp.maximum(i, j), off + hh))
    krow = lambda off: pl.BlockSpec((T, HEAD_DIM), lambda hh, j, i: (j, off + hh))
    return pl.pallas_call(
        body, name="fox_bwd", grid=(H, nq, nq),
        in_specs=[qrow(4 * H), krow(5 * H), krow(6 * H), qrow(0), qrow(0), qrow(0),
                  pl.BlockSpec((T, LANES), lambda hh, j, i: (jnp.maximum(i, j), 0)),
                  pl.BlockSpec((None, 1, T), lambda hh, j, i: (hh, 0, j))],
        out_specs=[pl.BlockSpec((S, HEAD_DIM), lambda hh, j, i: (0, hh)), krow(0), krow(0),
                   pl.BlockSpec((S, HEAD_DIM), lambda hh, j, i: (0, hh)),
                   pl.BlockSpec((None, 1, T), lambda hh, j, i: (hh, 0, j))],
        out_shape=[jax.ShapeDtypeStruct((S, W), F32), jax.ShapeDtypeStruct((S, W), F32),
                   jax.ShapeDtypeStruct((S, W), F32), jax.ShapeDtypeStruct((S, W), F32),
                   jax.ShapeDtypeStruct((H, 1, S), F32)],
        scratch_shapes=[pltpu.VMEM((T, HEAD_DIM), F32), pltpu.VMEM((T, HEAD_DIM), F32), pltpu.VMEM((1, T), F32)],
        compiler_params=_params(),
    )(h, h, h, do, lse, dl, c, ct)


def _adamw_math(w, g, m, v):
    m = ADAM_B1 * m + (1.0 - ADAM_B1) * g
    v = ADAM_B2 * v + (1.0 - ADAM_B2) * (g * g)
    m_hat = m / (1.0 - ADAM_B1 ** ADAM_STEP)
    v_hat = v / (1.0 - ADAM_B2 ** ADAM_STEP)
    delta = -ADAM_LR * (m_hat / (jnp.sqrt(v_hat) + ADAM_EPS) + ADAM_WD * w)
    return delta, m, v


def _adamw(w, g, m, v, name):
    L, R, C = w.shape
    tr = _rows(R, C * 4 * 7, 8)

    def body(w_ref, g_ref, m_ref, v_ref, d_ref, nm_ref, nv_ref):
        d_ref[...], nm_ref[...], nv_ref[...] = _adamw_math(w_ref[...], g_ref[...], m_ref[...], v_ref[...])

    blk = pl.BlockSpec((None, tr, C), lambda l, i: (l, i, 0))
    shape = jax.ShapeDtypeStruct((L, R, C), F32)
    return pl.pallas_call(
        body, name=name, grid=(L, R // tr), in_specs=[blk] * 4, out_specs=[blk] * 3,
        out_shape=[shape] * 3, compiler_params=_params(),
    )(w, g, m, v)


def _place():
    x, y, c = lax.axis_index("x"), lax.axis_index("y"), lax.axis_index("c")
    return x, y, c, [(1 - x, y), (x, 1 - y), (1 - x, 1 - y)]


def _remote(src, dst, send_sems, recv_sems, k, to):
    return pltpu.make_async_remote_copy(src_ref=src, dst_ref=dst, send_sem=send_sems.at[k], recv_sem=recv_sems.at[k],
                                        device_id=to, device_id_type=MESH)


def _gather_weights(shards):
    n = len(shards)
    half = shards[0].shape[0] // 2

    def body(*refs):
        srcs, dsts = refs[:n], refs[n:2 * n]
        send_sems, recv_sems, local_sems = refs[2 * n:]
        x, y, c, chips = _place()
        me = 2 * x + y
        mine, theirs = pl.ds(c * half, half), pl.ds((1 - c) * half, half)
        local = [pltpu.make_async_copy(srcs[a], dsts[a].at[me], local_sems.at[a]) for a in range(n)]
        for cp in local:
            cp.start()
        first = [_remote(srcs[a].at[mine], dsts[a].at[me, mine], send_sems, recv_sems, 6 * a + j, (px, py, c))
                 for a in range(n) for j, (px, py) in enumerate(chips)]
        for cp in first:
            cp.start()
        passed = []
        for a in range(n):
            for j, (px, py) in enumerate(chips):
                landed = dsts[a].at[2 * px + py, mine]
                _remote(landed, landed, send_sems, recv_sems, 6 * a + j, (px, py, c)).wait_recv()
                cp = _remote(landed, landed, send_sems, recv_sems, 6 * a + 3 + j, (x, y, 1 - c))
                cp.start()
                passed.append(cp)
        for a in range(n):
            for j, (px, py) in enumerate(chips):
                landed = dsts[a].at[2 * px + py, theirs]
                _remote(landed, landed, send_sems, recv_sems, 6 * a + 3 + j, (x, y, 1 - c)).wait_recv()
        for cp in first + passed:
            cp.wait_send()
        for cp in local:
            cp.wait()

    return pl.pallas_call(
        body, name="gather_weights", in_specs=[ANY] * n, out_specs=[ANY] * n,
        out_shape=[jax.ShapeDtypeStruct((4,) + s.shape, s.dtype) for s in shards],
        scratch_shapes=[pltpu.SemaphoreType.DMA((6 * n,)), pltpu.SemaphoreType.DMA((6 * n,)),
                        pltpu.SemaphoreType.DMA((n,))],
    )(*shards)


def _swap_other_half(parts):
    n = len(parts)
    half = parts[0].shape[1] // 2

    def body(*refs):
        srcs, dsts = refs[:n], refs[n:2 * n]
        send_sems, recv_sems = refs[2 * n:]
        x, y, c, _ = _place()
        cps = [_remote(srcs[a].at[:, pl.ds((1 - c) * half, half)], dsts[a], send_sems, recv_sems, a, (x, y, 1 - c))
               for a in range(n)]
        for cp in cps:
            cp.start()
        for cp in cps:
            cp.wait()

    return pl.pallas_call(
        body, name="grad_swap_half", in_specs=[ANY] * n, out_specs=[ANY] * n,
        out_shape=[jax.ShapeDtypeStruct((4, half) + p.shape[2:], p.dtype) for p in parts],
        scratch_shapes=[pltpu.SemaphoreType.DMA((n,)), pltpu.SemaphoreType.DMA((n,))],
    )(*parts)


def _add_half(part, got, name):
    _, half, R, C = got.shape
    tr = _rows(R, C * 2 * 3)

    def body(c_ref, p_ref, g_ref, o_ref):
        o_ref[...] = (p_ref[...].astype(F32) + g_ref[...].astype(F32)).astype(BF16)

    grid_spec = pltpu.PrefetchScalarGridSpec(
        num_scalar_prefetch=1, grid=(4, half, R // tr),
        in_specs=[pl.BlockSpec((None, None, tr, C), lambda s, l, i, c: (s, c[0] * half + l, i, 0)),
                  pl.BlockSpec((None, None, tr, C), lambda s, l, i, c: (s, l, i, 0))],
        out_specs=pl.BlockSpec((None, None, tr, C), lambda s, l, i, c: (s, l, i, 0)))
    core = lax.axis_index("c").astype(jnp.int32).reshape(1)
    return pl.pallas_call(
        body, name=name, grid_spec=grid_spec, out_shape=jax.ShapeDtypeStruct(got.shape, BF16),
        compiler_params=_params(),
    )(core, part, got)


def _scatter_to_owner(parts):
    n = len(parts)

    def body(*refs):
        srcs, dsts = refs[:n], refs[n:2 * n]
        send_sems, recv_sems, local_sems = refs[2 * n:]
        x, y, c, chips = _place()
        me = 2 * x + y
        local = [pltpu.make_async_copy(srcs[a].at[me], dsts[a].at[me], local_sems.at[a]) for a in range(n)]
        for cp in local:
            cp.start()
        sends = [_remote(srcs[a].at[2 * px + py], dsts[a].at[me], send_sems, recv_sems, 3 * a + j, (px, py, c))
                 for a in range(n) for j, (px, py) in enumerate(chips)]
        for cp in sends:
            cp.start()
        for a in range(n):
            for j, (px, py) in enumerate(chips):
                slot = dsts[a].at[2 * px + py]
                _remote(slot, slot, send_sems, recv_sems, 3 * a + j, (px, py, c)).wait_recv()
        for cp in sends:
            cp.wait_send()
        for cp in local:
            cp.wait()

    return pl.pallas_call(
        body, name="grad_scatter", in_specs=[ANY] * n, out_specs=[ANY] * n,
        out_shape=[jax.ShapeDtypeStruct(p.shape, p.dtype) for p in parts],
        scratch_shapes=[pltpu.SemaphoreType.DMA((3 * n,)), pltpu.SemaphoreType.DMA((3 * n,)),
                        pltpu.SemaphoreType.DMA((n,))],
    )(*parts)


def _sum_chips(got, name):
    _, half, R, C = got.shape
    tr = _rows(R, C * (2 * 4 + 4))

    def body(g_ref, o_ref):
        o_ref[...] = ((g_ref[0].astype(F32) + g_ref[1].astype(F32)) + g_ref[2].astype(F32)) + g_ref[3].astype(F32)

    return pl.pallas_call(
        body, name=name, grid=(half, R // tr),
        in_specs=[pl.BlockSpec((4, None, tr, C), lambda l, i: (0, l, i, 0))],
        out_specs=pl.BlockSpec((None, tr, C), lambda l, i: (l, i, 0)),
        out_shape=jax.ShapeDtypeStruct((half, R, C), F32), compiler_params=_params(),
    )(got)


def _share_halves(halves):
    n = len(halves)
    half = halves[0].shape[0]

    def body(*refs):
        srcs, dsts = refs[:n], refs[n:2 * n]
        send_sems, recv_sems, local_sems = refs[2 * n:]
        x, y, c, _ = _place()
        mine, theirs = pl.ds(c * half, half), pl.ds((1 - c) * half, half)
        local = [pltpu.make_async_copy(srcs[a], dsts[a].at[mine], local_sems.at[a]) for a in range(n)]
        sends = [_remote(srcs[a], dsts[a].at[mine], send_sems, recv_sems, a, (x, y, 1 - c)) for a in range(n)]
        for cp in local + sends:
            cp.start()
        for a in range(n):
            landed = dsts[a].at[theirs]
            _remote(landed, landed, send_sems, recv_sems, a, (x, y, 1 - c)).wait_recv()
        for cp in sends:
            cp.wait_send()
        for cp in local:
            cp.wait()

    return pl.pallas_call(
        body, name="grad_share_halves", in_specs=[ANY] * n, out_specs=[ANY] * n,
        out_shape=[jax.ShapeDtypeStruct((2 * half,) + h.shape[1:], h.dtype) for h in halves],
        scratch_shapes=[pltpu.SemaphoreType.DMA((n,)), pltpu.SemaphoreType.DMA((n,)), pltpu.SemaphoreType.DMA((n,))],
    )(*halves)


def _small_allreduce_adamw(part, w, m, v):
    R = part.shape[0]
    deltas = [(dx, dy, dc) for dx in (0, 1) for dy in (0, 1) for dc in (0, 1)][1:]

    def body(p_ref, w_ref, m_ref, v_ref, g_ref, d_ref, nm_ref, nv_ref, all_ref, send_sems, recv_sems):
        x, y, c, _ = _place()
        me = 4 * x + 2 * y + c
        all_ref[me] = p_ref[...]
        cps = [_remote(p_ref, all_ref.at[me], send_sems, recv_sems, k, (x ^ dx, y ^ dy, c ^ dc))
               for k, (dx, dy, dc) in enumerate(deltas)]
        for cp in cps:
            cp.start()
        for k, (dx, dy, dc) in enumerate(deltas):
            slot = all_ref.at[4 * (x ^ dx) + 2 * (y ^ dy) + (c ^ dc)]
            _remote(slot, slot, send_sems, recv_sems, k, (x ^ dx, y ^ dy, c ^ dc)).wait_recv()
        for cp in cps:
            cp.wait_send()
        g = all_ref[0]
        for k in range(1, 8):
            g = g + all_ref[k]
        g_ref[...] = g
        d_ref[...], nm_ref[...], nv_ref[...] = _adamw_math(w_ref[...], g, m_ref[...], v_ref[...])

    vm = pl.BlockSpec(memory_space=pltpu.VMEM)
    shape = jax.ShapeDtypeStruct((R, LANES), F32)
    return pl.pallas_call(
        body, name="small_allreduce_adamw", in_specs=[vm] * 4, out_specs=[vm] * 4, out_shape=[shape] * 4,
        scratch_shapes=[pltpu.VMEM((8, R, LANES), F32), pltpu.SemaphoreType.DMA((7,)), pltpu.SemaphoreType.DMA((7,))],
    )(part, w, m, v)


def _pack_small(bf, bg, lg, lb, extra=None):
    L, H = bf.shape
    per = jnp.concatenate([jnp.pad(bf, ((0, 0), (0, LANES - H))), bg, lg, lb], axis=1)
    flat = per.reshape(-1, LANES)
    last = jnp.zeros((8 + (-flat.shape[0]) % 8, LANES), F32)
    if extra is not None:
        last = last.at[-8, 0].set(extra)
    return jnp.concatenate([flat, last], axis=0)


def _unpack_small(p, L, H, D):
    per = p[:L * (1 + 4 * D // LANES)].reshape(L, -1)
    return per[:, :H], per[:, LANES:LANES + 2 * D], per[:, LANES + 2 * D:LANES + 3 * D], per[:, LANES + 3 * D:]


def kernel(x, w_in, b_forget, b_gate, w_up_a, w_up_b, w_out, ln_g, ln_b, loss_target, m_w_in, m_b_forget, m_b_gate, m_w_up_a, m_w_up_b, m_w_out, m_ln_g, m_ln_b, v_w_in, v_b_forget, v_b_gate, v_w_up_a, v_w_up_b, v_w_out, v_ln_g, v_ln_b):
    _, S, D = x.shape
    L, _, C4 = w_in.shape
    H = b_forget.shape[1]
    W = w_up_a.shape[1]
    D4 = D // 4
    NC = 4 * C4
    assert W == H * HEAD_DIM and NC == 8 * W + H + 2 * D and L % 2 == 0 and D % LANES == 0
    alpha = float((2 * L) ** 0.25)
    f_block = 2 * D // LANES

    gi, gu, go = _gather_weights([w_in.astype(BF16),
                                  jnp.concatenate([w_up_a, w_up_b], axis=2).astype(BF16),
                                  w_out.astype(BF16)])
    w_main, w_fg, w_ua, w_ub, w_o = [], [], [], [], []
    for l in range(L):
        full = gi[:, l].transpose(1, 0, 2).reshape(D, NC)
        w_main.append(full[:, :8 * W])
        w_fg.append(jnp.concatenate([full[:, 8 * W + H:], full[:, 8 * W:8 * W + H],
                                     jnp.zeros((D, LANES - H), BF16)], axis=1))
        up = gu[:, l]
        w_ua.append(up[:, :, :D4].transpose(1, 0, 2).reshape(W, D))
        w_ub.append(up[:, :, D4:].transpose(1, 0, 2).reshape(W, D))
        w_o.append(go[:, l].reshape(D, D))

    pos = jnp.arange(S, dtype=F32)
    inv_freq = ROPE_THETA ** (-jnp.arange(HEAD_DIM // 2, dtype=F32) / (HEAD_DIM // 2))
    ang = pos[:, None] * inv_freq[None, :]
    cos = jnp.concatenate([jnp.cos(ang), jnp.cos(ang)], axis=1)
    sin = jnp.concatenate([-jnp.sin(ang), jnp.sin(ang)], axis=1)
    bf_pad = jnp.pad(b_forget, ((0, 0), (0, LANES - H)))

    xs = x[0]
    xb = xs.astype(BF16)
    saved = []
    for l in range(L):
        h = _matmul(xb, w_main[l], mode="nn", out_dtype=BF16, name="in_proj", rope=(cos, sin), rope_cols=2 * W)
        hfg = _matmul(xb, w_fg[l], mode="nn", out_dtype=F32, name="in_proj_gates")
        os, lses = [], []
        for window, d in DILATED_PATTERNS:
            o, lse = _dil_fwd(h.reshape(S // d, d * 8 * W), d, W, f"dil_fwd_d{d}")
            os.append(o.reshape(S, W))
            lses.append(lse.reshape(S, W))
        out_a, lse_a, ga = _dil_combine_fwd(os, lses, h, W)
        c = _scan_fwd(hfg, bf_pad[l:l + 1], f_block)
        ct = c.T.reshape(LANES, 1, S)
        out_b, gb, lse_b = _fox_fwd(h, c, ct, H)
        up_a = _matmul(ga, w_ua[l], mode="nn", out_dtype=BF16, name="up_proj")
        up_b = _matmul(gb, w_ub[l], mode="nn", out_dtype=BF16, name="up_proj")
        u = _merge_fwd(up_a, up_b, hfg, b_gate[l:l + 1])
        r = _matmul(u, w_o[l], mode="nn", out_dtype=F32, name="out_proj", acc_in=xs, acc_scale=alpha)
        saved.append((xb, h, hfg, out_a, lse_a, ga, c, ct, out_b, gb, lse_b, up_a, up_b, u, r))
        xs, xb = _ln_fwd(r, ln_g[l:l + 1], ln_b[l:l + 1])

    dx, sq = _loss(xs, loss_target[0])
    loss_part = 0.5 * jnp.sum(sq) / D

    g_in, g_up, g_out, g_bf, g_bg, g_lg, g_lb = [], [], [], [], [], [], []
    for l in reversed(range(L)):
        xb, h, hfg, out_a, lse_a, ga, c, ct, out_b, gb, lse_b, up_a, up_b, u, r = saved[l]
        dr, adr, dlg, dlb = _ln_bwd(dx, r, ln_g[l:l + 1], alpha)
        du = _matmul(dr, w_o[l], mode="nt", out_dtype=F32, name="out_proj_dx")
        dwo = _matmul(u, dr, mode="tn", out_dtype=F32, name="out_proj_dw")
        dua, dub, dgl, dbg = _merge_bwd(du, up_a, up_b, hfg, b_gate[l:l + 1])
        dga = _matmul(dua, w_ua[l], mode="nt", out_dtype=F32, name="up_proj_dx")
        dgb = _matmul(dub, w_ub[l], mode="nt", out_dtype=F32, name="up_proj_dx")
        dwua = _matmul(ga, dua, mode="tn", out_dtype=F32, name="up_proj_dw")
        dwub = _matmul(gb, dub, mode="tn", out_dtype=F32, name="up_proj_dw")
        do_a, dz_a, dl_a = _gate_bwd(dga, out_a, h, 3, "gate_bwd_a")
        dqs, dks, dvs = [], [], []
        for window, d in DILATED_PATTERNS:
            view = lambda t: t.reshape(S // d, d * t.shape[1])
            dq, dk, dv = _dil_bwd(view(h), view(do_a), view(lse_a), view(dl_a), d, W, f"dil_bwd_d{d}")
            dqs.append(dq.reshape(S, W))
            dks.append(dk.reshape(S, W))
            dvs.append(dv.reshape(S, W))
        dqkv_a = _dil_combine_bwd(dqs, dks, dvs, cos, sin)
        do_b, dz_b, dl_b = _gate_bwd(dgb, out_b, h, 7, "gate_bwd_b")
        dq_b, dk_b, dv_b, dcq, dct = _fox_bwd(h, do_b, lse_b, dl_b, c, ct, H)
        dc = jnp.pad(dcq.reshape(S, H, HEAD_DIM)[:, :, 0] + dct.reshape(H, S).T, ((0, 0), (0, LANES - H)))
        df, dbf = _scan_bwd(dc, hfg, bf_pad[l:l + 1], f_block, H)
        dh = jnp.concatenate([dqkv_a, dz_a, dq_b.astype(BF16), dk_b.astype(BF16), dv_b.astype(BF16), dz_b], axis=1)
        dhfg = jnp.concatenate([dgl, df], axis=1)
        dx1 = _matmul(dh, w_main[l], mode="nt", out_dtype=F32, name="in_proj_dx", acc_in=adr)
        dx = _matmul(dhfg, w_fg[l], mode="nt", out_dtype=F32, name="in_proj_gates_dx", acc_in=dx1)
        dwm = _matmul(xb, dh, mode="tn", out_dtype=F32, name="in_proj_dw")
        dwfg = _matmul(xb, dhfg, mode="tn", out_dtype=F32, name="in_proj_gates_dw")
        full = jnp.concatenate([dwm, dwfg[:, 2 * D:2 * D + H], dwfg[:, :2 * D]], axis=1)
        g_in.append(full.reshape(D, 4, C4).transpose(1, 0, 2).astype(BF16))
        g_up.append(jnp.concatenate([dwua.reshape(W, 4, D4), dwub.reshape(W, 4, D4)], axis=2).transpose(1, 0, 2).astype(BF16))
        g_out.append(dwo.reshape(4, D4, D).astype(BF16))
        g_bf.append(dbf[0, :H])
        g_bg.append(dbg[0])
        g_lg.append(dlg[0])
        g_lb.append(dlb[0])
    grad_x = dx[None]
    for lst in (g_in, g_up, g_out, g_bf, g_bg, g_lg, g_lb):
        lst.reverse()

    parts = [jnp.stack(g_in, axis=1), jnp.stack(g_up, axis=1), jnp.stack(g_out, axis=1)]
    names = ["w_in", "w_up", "w_out"]
    got = _swap_other_half(parts)
    chip = [_add_half(p, g, f"grad_add_half_{n}") for p, g, n in zip(parts, got, names)]
    landed = _scatter_to_owner(chip)
    halves = [_sum_chips(g, f"grad_sum_chips_{n}") for g, n in zip(landed, names)]
    grad_w_in, grad_up, grad_w_out = _share_halves(halves)
    grad_w_up_a, grad_w_up_b = grad_up[:, :, :D4], grad_up[:, :, D4:]

    d_in, nm_in, nv_in = _adamw(w_in, grad_w_in, m_w_in, v_w_in, "adamw_w_in")
    d_ua, nm_ua, nv_ua = _adamw(w_up_a, grad_w_up_a, m_w_up_a, v_w_up_a, "adamw_w_up")
    d_ub, nm_ub, nv_ub = _adamw(w_up_b, grad_w_up_b, m_w_up_b, v_w_up_b, "adamw_w_up")
    d_o, nm_o, nv_o = _adamw(w_out, grad_w_out, m_w_out, v_w_out, "adamw_w_out")

    small_g = _pack_small(jnp.stack(g_bf), jnp.stack(g_bg), jnp.stack(g_lg), jnp.stack(g_lb), loss_part)
    small = _small_allreduce_adamw(small_g, _pack_small(b_forget, b_gate, ln_g, ln_b),
                                   _pack_small(m_b_forget, m_b_gate, m_ln_g, m_ln_b),
                                   _pack_small(v_b_forget, v_b_gate, v_ln_g, v_ln_b))
    loss = small[0][-8, 0]
    (g_bf, g_bg, g_lg, g_lb), (d_bf, d_bg, d_lg, d_lb), (nm_bf, nm_bg, nm_lg, nm_lb), (nv_bf, nv_bg, nv_lg, nv_lb) = [
        _unpack_small(p, L, H, D) for p in small]

    return (loss, grad_x,
            grad_w_in, g_bf, g_bg, grad_w_up_a, grad_w_up_b, grad_w_out, g_lg, g_lb,
            d_in, d_bf, d_bg, d_ua, d_ub, d_o, d_lg, d_lb,
            nm_in, nm_bf, nm_bg, nm_ua, nm_ub, nm_o, nm_lg, nm_lb,
            nv_in, nv_bf, nv_bg, nv_ua, nv_ub, nv_o, nv_lg, nv_lb)
```

```python
import functools

import jax
import jax.numpy as jnp
from jax import lax
from jax.experimental import pallas as pl
from jax.experimental.pallas import tpu as pltpu

F32 = jnp.float32
BF16 = jnp.bfloat16
MESH = pl.DeviceIdType.MESH
ANY = pl.BlockSpec(memory_space=pl.ANY)

HEAD_DIM = 128
LANES = 128
Q_BLOCK = 128
DILATED_PATTERNS = ((128, 1), (512, 4), (2048, 16))
ROPE_THETA = 10000.0
LN_EPS = 1e-5
ADAM_LR, ADAM_B1, ADAM_B2, ADAM_EPS, ADAM_WD, ADAM_STEP = 0.001, 0.9, 0.999, 1e-08, 0.01, 10
NEG = -1e30
VMEM_LIMIT = 56 * 2**20
ELEMENTWISE_BUDGET = 20 * 2**20
FOX_TILE = 512
MM_TILES = (1024, 1024, 512)


def _params():
    return pltpu.CompilerParams(vmem_limit_bytes=VMEM_LIMIT)


def _tile(dim, target, align):
    if dim <= target:
        return dim
    t = (target // align) * align
    while t >= align:
        if dim % t == 0:
            return t
        t -= align
    return dim


def _rows(n_rows, bytes_per_row, align=16):
    return _tile(n_rows, max(align, ELEMENTWISE_BUDGET // (2 * bytes_per_row)), align)


def _sigmoid(v):
    return 1.0 / (1.0 + jnp.exp(-v))


def _matmul(a, b, *, mode, out_dtype, name, acc_in=None, acc_scale=1.0, rope=None, rope_cols=0):
    if mode == "nn":
        (M, K), (K2, N) = a.shape, b.shape
    elif mode == "nt":
        (M, K), (N, K2) = a.shape, b.shape
    else:
        (K, M), (K2, N) = a.shape, b.shape
    assert K == K2, (a.shape, b.shape, mode)
    tm, tn, tk = _tile(M, MM_TILES[0], 128), _tile(N, MM_TILES[1], 128), _tile(K, MM_TILES[2], 128)
    if rope is not None:
        assert rope_cols % tn == 0
    nk = K // tk
    n_rope_tiles = rope_cols // tn if rope is not None else 0
    dims = {"nn": (((1,), (0,)), ((), ())), "nt": (((1,), (1,)), ((), ())), "tn": (((0,), (0,)), ((), ()))}[mode]

    def body(*refs):
        a_ref, b_ref = refs[0], refs[1]
        pos = 2
        if rope is not None:
            cos_ref, sin_ref = refs[pos], refs[pos + 1]
            pos += 2
        if acc_in is not None:
            acc_in_ref = refs[pos]
            pos += 1
        o_ref, acc_ref = refs[pos], refs[pos + 1]
        j, k = pl.program_id(1), pl.program_id(2)

        @pl.when(k == 0)
        def _():
            acc_ref[...] = jnp.zeros_like(acc_ref)

        acc_ref[...] += lax.dot_general(a_ref[...], b_ref[...], dims, preferred_element_type=F32)

        def finish(rotate):
            r = acc_ref[...]
            if acc_in is not None:
                r = r + acc_scale * acc_in_ref[...]
            if rotate:
                cos, sin = cos_ref[...], sin_ref[...]
                for g in range(tn // HEAD_DIM):
                    sl = slice(g * HEAD_DIM, (g + 1) * HEAD_DIM)
                    t = r[:, sl]
                    o_ref[:, sl] = (t * cos + pltpu.roll(t, HEAD_DIM // 2, 1) * sin).astype(o_ref.dtype)
            else:
                o_ref[...] = r.astype(o_ref.dtype)

        if n_rope_tiles:
            @pl.when((k == nk - 1) & (j < n_rope_tiles))
            def _():
                finish(True)

            @pl.when((k == nk - 1) & (j >= n_rope_tiles))
            def _():
                finish(False)
        else:
            @pl.when(k == nk - 1)
            def _():
                finish(False)

    if mode == "nn":
        in_specs = [pl.BlockSpec((tm, tk), lambda i, j, k: (i, k)), pl.BlockSpec((tk, tn), lambda i, j, k: (k, j))]
    elif mode == "nt":
        in_specs = [pl.BlockSpec((tm, tk), lambda i, j, k: (i, k)), pl.BlockSpec((tn, tk), lambda i, j, k: (j, k))]
    else:
        in_specs = [pl.BlockSpec((tk, tm), lambda i, j, k: (k, i)), pl.BlockSpec((tk, tn), lambda i, j, k: (k, j))]
    args = [a, b]
    if rope is not None:
        in_specs += [pl.BlockSpec((tm, HEAD_DIM), lambda i, j, k: (i, 0))] * 2
        args += list(rope)
    if acc_in is not None:
        in_specs.append(pl.BlockSpec((tm, tn), lambda i, j, k: (i, j)))
        args.append(acc_in)
    return pl.pallas_call(
        body, name=name, grid=(M // tm, N // tn, nk), in_specs=in_specs,
        out_specs=pl.BlockSpec((tm, tn), lambda i, j, k: (i, j)),
        out_shape=jax.ShapeDtypeStruct((M, N), out_dtype),
        scratch_shapes=[pltpu.VMEM((tm, tn), F32)], compiler_params=_params(),
    )(*args)


def _ln_fwd(r, g, b):
    S, D = r.shape
    tm = _rows(S, D * (4 + 4 + 2))

    def body(r_ref, g_ref, b_ref, x_ref, xb_ref):
        v = r_ref[...]
        mu = jnp.mean(v, axis=1, keepdims=True)
        cen = v - mu
        var = jnp.mean(cen * cen, axis=1, keepdims=True)
        out = cen * lax.rsqrt(var + LN_EPS) * g_ref[...] + b_ref[...]
        x_ref[...] = out
        xb_ref[...] = out.astype(BF16)

    row = pl.BlockSpec((tm, D), lambda i: (i, 0))
    vec = pl.BlockSpec((1, D), lambda i: (0, 0))
    return pl.pallas_call(
        body, name="ln_fwd", grid=(S // tm,), in_specs=[row, vec, vec], out_specs=[row, row],
        out_shape=[jax.ShapeDtypeStruct((S, D), F32), jax.ShapeDtypeStruct((S, D), BF16)],
        compiler_params=_params(),
    )(r, g, b)


def _ln_bwd(dx, r, g, alpha):
    S, D = r.shape
    tm = _rows(S, D * (4 + 4 + 2 + 4))

    def body(dx_ref, r_ref, g_ref, drb_ref, adr_ref, dg_ref, db_ref):
        @pl.when(pl.program_id(0) == 0)
        def _():
            dg_ref[...] = jnp.zeros_like(dg_ref)
            db_ref[...] = jnp.zeros_like(db_ref)

        v, d = r_ref[...], dx_ref[...]
        mu = jnp.mean(v, axis=1, keepdims=True)
        cen = v - mu
        var = jnp.mean(cen * cen, axis=1, keepdims=True)
        rstd = lax.rsqrt(var + LN_EPS)
        xhat = cen * rstd
        dxhat = d * g_ref[...]
        dr = rstd * (dxhat - jnp.mean(dxhat, axis=1, keepdims=True)
                     - xhat * jnp.mean(dxhat * xhat, axis=1, keepdims=True))
        drb_ref[...] = dr.astype(BF16)
        adr_ref[...] = alpha * dr
        dg_ref[...] += jnp.sum(d * xhat, axis=0, keepdims=True)
        db_ref[...] += jnp.sum(d, axis=0, keepdims=True)

    row = pl.BlockSpec((tm, D), lambda i: (i, 0))
    vec = pl.BlockSpec((1, D), lambda i: (0, 0))
    return pl.pallas_call(
        body, name="ln_bwd", grid=(S // tm,), in_specs=[row, row, vec], out_specs=[row, row, vec, vec],
        out_shape=[jax.ShapeDtypeStruct((S, D), BF16), jax.ShapeDtypeStruct((S, D), F32),
                   jax.ShapeDtypeStruct((1, D), F32), jax.ShapeDtypeStruct((1, D), F32)],
        compiler_params=_params(),
    )(dx, r, g)


def _loss(y, target):
    S, D = y.shape
    tm = _rows(S, D * 12)

    def body(y_ref, t_ref, dy_ref, sq_ref):
        @pl.when(pl.program_id(0) == 0)
        def _():
            sq_ref[...] = jnp.zeros_like(sq_ref)

        err = y_ref[...] - t_ref[...]
        dy_ref[...] = err * (1.0 / D)
        sq_ref[...] += jnp.sum(err * err, axis=0, keepdims=True)

    row = pl.BlockSpec((tm, D), lambda i: (i, 0))
    vec = pl.BlockSpec((1, D), lambda i: (0, 0))
    return pl.pallas_call(
        body, name="loss", grid=(S // tm,), in_specs=[row, row], out_specs=[row, vec],
        out_shape=[jax.ShapeDtypeStruct((S, D), F32), jax.ShapeDtypeStruct((1, D), F32)],
        compiler_params=_params(),
    )(y, target)


def _merge_fwd(up_a, up_b, hfg, b_gate):
    S, D = up_a.shape
    tm = _rows(S, D * (2 + 2 + 4 + 4 + 2))

    def body(ua_ref, ub_ref, gla_ref, glb_ref, bga_ref, bgb_ref, u_ref):
        ga = _sigmoid(gla_ref[...] + bga_ref[...])
        gb = _sigmoid(glb_ref[...] + bgb_ref[...])
        u_ref[...] = (ga * ua_ref[...].astype(F32) + gb * ub_ref[...].astype(F32)).astype(BF16)

    row = pl.BlockSpec((tm, D), lambda i: (i, 0))
    row1 = pl.BlockSpec((tm, D), lambda i: (i, 1))
    v0 = pl.BlockSpec((1, D), lambda i: (0, 0))
    v1 = pl.BlockSpec((1, D), lambda i: (0, 1))
    return pl.pallas_call(
        body, name="merge_fwd", grid=(S // tm,), in_specs=[row, row, row, row1, v0, v1], out_specs=row,
        out_shape=jax.ShapeDtypeStruct((S, D), BF16), compiler_params=_params(),
    )(up_a, up_b, hfg, hfg, b_gate, b_gate)


def _merge_bwd(du, up_a, up_b, hfg, b_gate):
    S, D = up_a.shape
    tm = _rows(S, D * (4 + 2 + 2 + 4 + 4 + 2 + 2 + 4))

    def body(du_ref, ua_ref, ub_ref, gla_ref, glb_ref, bga_ref, bgb_ref, dua_ref, dub_ref, dgl_ref, dbg_ref):
        @pl.when(pl.program_id(0) == 0)
        def _():
            dbg_ref[...] = jnp.zeros_like(dbg_ref)

        du = du_ref[...]
        ga = _sigmoid(gla_ref[...] + bga_ref[...])
        gb = _sigmoid(glb_ref[...] + bgb_ref[...])
        dua_ref[...] = (du * ga).astype(BF16)
        dub_ref[...] = (du * gb).astype(BF16)
        dgla = du * ua_ref[...].astype(F32) * ga * (1.0 - ga)
        dglb = du * ub_ref[...].astype(F32) * gb * (1.0 - gb)
        dgl_ref[:, :D] = dgla.astype(BF16)
        dgl_ref[:, D:] = dglb.astype(BF16)
        dbg_ref[:, :D] += jnp.sum(dgla, axis=0, keepdims=True)
        dbg_ref[:, D:] += jnp.sum(dglb, axis=0, keepdims=True)

    row = pl.BlockSpec((tm, D), lambda i: (i, 0))
    row1 = pl.BlockSpec((tm, D), lambda i: (i, 1))
    v0 = pl.BlockSpec((1, D), lambda i: (0, 0))
    v1 = pl.BlockSpec((1, D), lambda i: (0, 1))
    return pl.pallas_call(
        body, name="merge_bwd", grid=(S // tm,), in_specs=[row, row, row, row, row1, v0, v1],
        out_specs=[row, row, pl.BlockSpec((tm, 2 * D), lambda i: (i, 0)), pl.BlockSpec((1, 2 * D), lambda i: (0, 0))],
        out_shape=[jax.ShapeDtypeStruct((S, D), BF16), jax.ShapeDtypeStruct((S, D), BF16),
                   jax.ShapeDtypeStruct((S, 2 * D), BF16), jax.ShapeDtypeStruct((1, 2 * D), F32)],
        compiler_params=_params(),
    )(du, up_a, up_b, hfg, hfg, b_gate, b_gate)


def _gate_bwd(dg, out, h, z_block, compact, name):
    S, W = out.shape
    H = W // HEAD_DIM
    tm = _rows(S, W * (4 + 2 + 2 + 2 + 2 + 4))

    def body(dg_ref, o_ref, z_ref, do_ref, dz_ref, dl_ref):
        z = z_ref[...].astype(F32)
        o = o_ref[...].astype(F32)
        d = dg_ref[...]
        sg = _sigmoid(z)
        dout = d * z * sg
        do_ref[...] = dout.astype(BF16)
        dz_ref[...] = (d * o * sg * (1.0 + z * (1.0 - sg))).astype(BF16)
        prod = dout * o
        for hh in range(H):
            sl = slice(hh * HEAD_DIM, (hh + 1) * HEAD_DIM)
            tot = jnp.sum(prod[:, sl], axis=1, keepdims=True)
            if compact:
                dl_ref[:, hh:hh + 1] = tot
            else:
                dl_ref[:, sl] = jnp.broadcast_to(tot, (tm, HEAD_DIM))

    row = pl.BlockSpec((tm, W), lambda i: (i, 0))
    dl_cols = H if compact else W
    return pl.pallas_call(
        body, name=name, grid=(S // tm,),
        in_specs=[row, row, pl.BlockSpec((tm, W), lambda i: (i, z_block))],
        out_specs=[row, row, pl.BlockSpec((tm, dl_cols), lambda i: (i, 0))],
        out_shape=[jax.ShapeDtypeStruct((S, W), BF16), jax.ShapeDtypeStruct((S, W), BF16),
                   jax.ShapeDtypeStruct((S, dl_cols), F32)],
        compiler_params=_params(),
    )(dg, out, h)


def _dil_masks(n):
    i = lax.broadcasted_iota(jnp.int32, (Q_BLOCK, Q_BLOCK), 0)
    j = lax.broadcasted_iota(jnp.int32, (Q_BLOCK, Q_BLOCK), 1)
    return j <= i, (j >= i) & (n > 0)


def _dil_fwd(hv, d, W, name):
    L = hv.shape[0]
    H = W // HEAD_DIM
    nblk = L // Q_BLOCK
    scale = HEAD_DIM ** -0.5
    nt = (((1,), (1,)), ((), ()))

    def body(q_ref, kp_ref, kc_ref, vp_ref, vc_ref, o_ref, lse_ref):
        mc, mp = _dil_masks(pl.program_id(1))
        for hh in range(H):
            sl = slice(hh * HEAD_DIM, (hh + 1) * HEAD_DIM)
            q = q_ref[:, sl]
            sc = jnp.where(mc, lax.dot_general(q, kc_ref[:, sl], nt, preferred_element_type=F32) * scale, NEG)
            sp = jnp.where(mp, lax.dot_general(q, kp_ref[:, sl], nt, preferred_element_type=F32) * scale, NEG)
            m = jnp.maximum(jnp.max(sc, axis=1, keepdims=True), jnp.max(sp, axis=1, keepdims=True))
            pc, pp = jnp.exp(sc - m), jnp.exp(sp - m)
            den = jnp.sum(pc, axis=1, keepdims=True) + jnp.sum(pp, axis=1, keepdims=True)
            acc = (jnp.dot(pc.astype(BF16), vc_ref[:, sl], preferred_element_type=F32)
                   + jnp.dot(pp.astype(BF16), vp_ref[:, sl], preferred_element_type=F32))
            o_ref[:, sl] = (acc / den).astype(BF16)
            lse_ref[:, hh:hh + 1] = m + jnp.log(den)

    def spec(col, prev):
        if prev:
            return pl.BlockSpec((Q_BLOCK, W), lambda r, n: (jnp.maximum(n - 1, 0), r * 3 + col))
        return pl.BlockSpec((Q_BLOCK, W), lambda r, n: (n, r * 3 + col))

    return pl.pallas_call(
        body, name=name, grid=(d, nblk),
        in_specs=[spec(0, False), spec(1, True), spec(1, False), spec(2, True), spec(2, False)],
        out_specs=[pl.BlockSpec((Q_BLOCK, W), lambda r, n: (n, r)),
                   pl.BlockSpec((None, Q_BLOCK, H), lambda r, n: (r, n, 0))],
        out_shape=[jax.ShapeDtypeStruct((L, d * W), BF16), jax.ShapeDtypeStruct((d, L, H), F32)],
        compiler_params=_params(),
    )(hv, hv, hv, hv, hv)


def _dil_combine_fwd(os, lses, h, W):
    S = h.shape[0]
    H = W // HEAD_DIM
    tm = _rows(S, W * (3 * 2 + 2 + 2 + 2) + 4 * H * 4)

    def body(o1, o2, o3, l1, l2, l3, z_ref, out_ref, lse_ref, g_ref):
        a, b, c = l1[...], l2[...], l3[...]
        m = jnp.maximum(jnp.maximum(a, b), c)
        ea, eb, ec = jnp.exp(a - m), jnp.exp(b - m), jnp.exp(c - m)
        den = ea + eb + ec
        wa, wb, wc = ea / den, eb / den, ec / den
        lse_ref[...] = m + jnp.log(den)
        for hh in range(H):
            sl = slice(hh * HEAD_DIM, (hh + 1) * HEAD_DIM)
            out = (wa[:, hh:hh + 1] * o1[:, sl].astype(F32) + wb[:, hh:hh + 1] * o2[:, sl].astype(F32)
                   + wc[:, hh:hh + 1] * o3[:, sl].astype(F32))
            z = z_ref[:, sl].astype(F32)
            out_ref[:, sl] = out.astype(BF16)
            g_ref[:, sl] = (out * z * _sigmoid(z)).astype(BF16)

    row = pl.BlockSpec((tm, W), lambda i: (i, 0))
    stat = pl.BlockSpec((tm, H), lambda i: (i, 0))
    return pl.pallas_call(
        body, name="dil_combine_fwd", grid=(S // tm,),
        in_specs=[row] * 3 + [stat] * 3 + [pl.BlockSpec((tm, W), lambda i: (i, 3))], out_specs=[row, stat, row],
        out_shape=[jax.ShapeDtypeStruct((S, W), BF16), jax.ShapeDtypeStruct((S, H), F32),
                   jax.ShapeDtypeStruct((S, W), BF16)],
        compiler_params=_params(),
    )(*os, *lses, h)


def _dil_bwd(hv, dov, lsev, dlv, d, W, name):
    L = hv.shape[0]
    H = W // HEAD_DIM
    nblk = L // Q_BLOCK
    scale = HEAD_DIM ** -0.5
    nt = (((1,), (1,)), ((), ()))
    tn = (((0,), (0,)), ((), ()))

    def body(q_ref, kp_ref, kc_ref, vp_ref, vc_ref, do_ref, lse_ref, dl_ref, dq_ref, dk_ref, dv_ref, ck_ref, cv_ref):
        n = pl.program_id(1)

        @pl.when(n == 0)
        def _():
            ck_ref[...] = jnp.zeros_like(ck_ref)
            cv_ref[...] = jnp.zeros_like(cv_ref)

        @pl.when(n < nblk)
        def _():
            mc, mp = _dil_masks(n)
            for hh in range(H):
                sl = slice(hh * HEAD_DIM, (hh + 1) * HEAD_DIM)
                q, do = q_ref[:, sl], do_ref[:, sl]
                kc, kp, vc, vp = kc_ref[:, sl], kp_ref[:, sl], vc_ref[:, sl], vp_ref[:, sl]
                lse, dl = lse_ref[:, hh:hh + 1], dl_ref[:, hh:hh + 1]
                sc = jnp.where(mc, lax.dot_general(q, kc, nt, preferred_element_type=F32) * scale, NEG)
                sp = jnp.where(mp, lax.dot_general(q, kp, nt, preferred_element_type=F32) * scale, NEG)
                pc, pp = jnp.exp(sc - lse), jnp.exp(sp - lse)
                dsc = pc * (lax.dot_general(do, vc, nt, preferred_element_type=F32) - dl) * scale
                dsp = pp * (lax.dot_general(do, vp, nt, preferred_element_type=F32) - dl) * scale
                dsc_b, dsp_b = dsc.astype(BF16), dsp.astype(BF16)
                dq_ref[:, sl] = (jnp.dot(dsc_b, kc, preferred_element_type=F32)
                                 + jnp.dot(dsp_b, kp, preferred_element_type=F32)).astype(BF16)
                dk_ref[:, sl] = (ck_ref[:, sl] + lax.dot_general(dsp_b, q, tn, preferred_element_type=F32)).astype(BF16)
                dv_ref[:, sl] = (cv_ref[:, sl]
                                 + lax.dot_general(pp.astype(BF16), do, tn, preferred_element_type=F32)).astype(BF16)
                ck_ref[:, sl] = lax.dot_general(dsc_b, q, tn, preferred_element_type=F32)
                cv_ref[:, sl] = lax.dot_general(pc.astype(BF16), do, tn, preferred_element_type=F32)

        @pl.when(n == nblk)
        def _():
            dk_ref[...] = ck_ref[...].astype(BF16)
            dv_ref[...] = cv_ref[...].astype(BF16)

    last = nblk - 1

    def hspec(col, prev):
        if prev:
            return pl.BlockSpec((Q_BLOCK, W), lambda r, n: (jnp.clip(n - 1, 0, last), r * 3 + col))
        return pl.BlockSpec((Q_BLOCK, W), lambda r, n: (jnp.minimum(n, last), r * 3 + col))

    cur = pl.BlockSpec((Q_BLOCK, W), lambda r, n: (jnp.minimum(n, last), r))
    lag = pl.BlockSpec((Q_BLOCK, W), lambda r, n: (jnp.maximum(n - 1, 0), r))
    stat = pl.BlockSpec((None, Q_BLOCK, H), lambda r, n: (r, jnp.minimum(n, last), 0))
    shape = jax.ShapeDtypeStruct((L, d * W), BF16)
    return pl.pallas_call(
        body, name=name, grid=(d, nblk + 1),
        in_specs=[hspec(0, False), hspec(1, True), hspec(1, False), hspec(2, True), hspec(2, False), cur, stat, stat],
        out_specs=[cur, lag, lag], out_shape=[shape, shape, shape],
        scratch_shapes=[pltpu.VMEM((Q_BLOCK, W), F32), pltpu.VMEM((Q_BLOCK, W), F32)],
        compiler_params=_params(),
    )(hv, hv, hv, hv, hv, dov, lsev, dlv)


def _dil_combine_bwd(dqs, dks, dvs, cos, sin):
    S, W = dqs[0].shape
    tm = _rows(S, W * (9 * 2 + 3 * 2 + 3 * 4))

    def body(q1, q2, q3, k1, k2, k3, v1, v2, v3, cos_ref, sin_ref, o_ref):
        cos_t, sin_t = cos_ref[...], -sin_ref[...]
        add3 = lambda a, b, c: a[...].astype(F32) + b[...].astype(F32) + c[...].astype(F32)
        dq = add3(q1, q2, q3)
        dk = add3(k1, k2, k3)
        for hh in range(W // HEAD_DIM):
            sl = slice(hh * HEAD_DIM, (hh + 1) * HEAD_DIM)
            tq, tk = dq[:, sl], dk[:, sl]
            o_ref[:, hh * HEAD_DIM:(hh + 1) * HEAD_DIM] = (
                tq * cos_t + pltpu.roll(tq, HEAD_DIM // 2, 1) * sin_t).astype(BF16)
            o_ref[:, W + hh * HEAD_DIM:W + (hh + 1) * HEAD_DIM] = (
                tk * cos_t + pltpu.roll(tk, HEAD_DIM // 2, 1) * sin_t).astype(BF16)
        o_ref[:, 2 * W:] = add3(v1, v2, v3).astype(BF16)

    row = pl.BlockSpec((tm, W), lambda i: (i, 0))
    tab = pl.BlockSpec((tm, HEAD_DIM), lambda i: (i, 0))
    return pl.pallas_call(
        body, name="dil_combine_bwd", grid=(S // tm,), in_specs=[row] * 9 + [tab, tab],
        out_specs=pl.BlockSpec((tm, 3 * W), lambda i: (i, 0)),
        out_shape=jax.ShapeDtypeStruct((S, 3 * W), BF16), compiler_params=_params(),
    )(*dqs, *dks, *dvs, cos, sin)


def _scan_tile(S):
    return _tile(S, 256, 8)


def _scan_fwd(hfg, bf_pad, f_block):
    S = hfg.shape[0]
    tm = _scan_tile(S)

    def body(f_ref, b_ref, c_ref, carry_ref):
        @pl.when(pl.program_id(0) == 0)
        def _():
            carry_ref[...] = jnp.zeros_like(carry_ref)

        v = f_ref[...] + b_ref[...]
        logf = jnp.minimum(v, 0.0) - jnp.log(1.0 + jnp.exp(-jnp.abs(v)))
        tri = (lax.broadcasted_iota(jnp.int32, (tm, tm), 1) <= lax.broadcasted_iota(jnp.int32, (tm, tm), 0)).astype(F32)
        c = jnp.dot(tri, logf, preferred_element_type=F32, precision=lax.Precision.HIGHEST) + carry_ref[...]
        c_ref[...] = c
        carry_ref[...] = c[tm - 1:tm, :]

    return pl.pallas_call(
        body, name="scan_fwd", grid=(S // tm,),
        in_specs=[pl.BlockSpec((tm, LANES), lambda i: (i, f_block)), pl.BlockSpec((1, LANES), lambda i: (0, 0))],
        out_specs=pl.BlockSpec((tm, LANES), lambda i: (i, 0)),
        out_shape=jax.ShapeDtypeStruct((S, LANES), F32),
        scratch_shapes=[pltpu.VMEM((1, LANES), F32)], compiler_params=_params(),
    )(hfg, bf_pad)


def _scan_bwd(dc, hfg, bf_pad, f_block, n_heads):
    S = hfg.shape[0]
    tm = _scan_tile(S)
    nt = S // tm

    def body(dc_ref, f_ref, b_ref, df_ref, db_ref, carry_ref):
        @pl.when(pl.program_id(0) == 0)
        def _():
            carry_ref[...] = jnp.zeros_like(carry_ref)
            db_ref[...] = jnp.zeros_like(db_ref)

        tri = (lax.broadcasted_iota(jnp.int32, (tm, tm), 1) >= lax.broadcasted_iota(jnp.int32, (tm, tm), 0)).astype(F32)
        dlogf = jnp.dot(tri, dc_ref[...], preferred_element_type=F32, precision=lax.Precision.HIGHEST) + carry_ref[...]
        carry_ref[...] = dlogf[0:1, :]
        v = f_ref[...] + b_ref[...]
        lane = lax.broadcasted_iota(jnp.int32, (tm, LANES), 1)
        df = jnp.where(lane < n_heads, dlogf * _sigmoid(-v), 0.0)
        df_ref[...] = df.astype(BF16)
        db_ref[...] += jnp.sum(df, axis=0, keepdims=True)

    return pl.pallas_call(
        body, name="scan_bwd", grid=(nt,),
        in_specs=[pl.BlockSpec((tm, LANES), lambda i: (nt - 1 - i, 0)),
                  pl.BlockSpec((tm, LANES), lambda i: (nt - 1 - i, f_block)),
                  pl.BlockSpec((1, LANES), lambda i: (0, 0))],
        out_specs=[pl.BlockSpec((tm, LANES), lambda i: (nt - 1 - i, 0)), pl.BlockSpec((1, LANES), lambda i: (0, 0))],
        out_shape=[jax.ShapeDtypeStruct((S, LANES), BF16), jax.ShapeDtypeStruct((1, LANES), F32)],
        scratch_shapes=[pltpu.VMEM((1, LANES), F32)], compiler_params=_params(),
    )(dc, hfg, bf_pad)


FOX_AUG = 2 * HEAD_DIM
LOG2E = 1.4426950408889634


def _fox_prep(h, c, H):
    S = h.shape[0]
    W = H * HEAD_DIM
    tm = _rows(S, 2 * W * 2 + LANES * 4 + 2 * H * FOX_AUG * 2)
    inv_scale = HEAD_DIM ** 0.5

    def body(q_ref, k_ref, c_ref, qa_ref, ka_ref):
        lane = lax.broadcasted_iota(jnp.int32, (tm, HEAD_DIM), 1)
        a = c_ref[...] * inv_scale
        for hh in range(H):
            col = a[:, hh:hh + 1]
            hi = col.astype(BF16).astype(F32)
            mid = (col - hi).astype(BF16).astype(F32)
            lo = col - hi - mid
            piece = jnp.where(lane % 3 == 0, hi, jnp.where(lane % 3 == 1, mid, lo))
            extra_q = jnp.where(lane < 3, 1.0, jnp.where(lane < 6, piece, 0.0))
            extra_k = jnp.where(lane < 3, -piece, jnp.where(lane < 6, 1.0, 0.0))
            qa_ref[:, hh * FOX_AUG:hh * FOX_AUG + HEAD_DIM] = q_ref[:, hh * HEAD_DIM:(hh + 1) * HEAD_DIM]
            qa_ref[:, hh * FOX_AUG + HEAD_DIM:(hh + 1) * FOX_AUG] = extra_q.astype(BF16)
            ka_ref[:, hh * FOX_AUG:hh * FOX_AUG + HEAD_DIM] = k_ref[:, hh * HEAD_DIM:(hh + 1) * HEAD_DIM]
            ka_ref[:, hh * FOX_AUG + HEAD_DIM:(hh + 1) * FOX_AUG] = extra_k.astype(BF16)

    aug = pl.BlockSpec((tm, H * FOX_AUG), lambda i: (i, 0))
    return pl.pallas_call(
        body, name="fox_prep", grid=(S // tm,),
        in_specs=[pl.BlockSpec((tm, W), lambda i: (i, 4)), pl.BlockSpec((tm, W), lambda i: (i, 5)),
                  pl.BlockSpec((tm, LANES), lambda i: (i, 0))],
        out_specs=[aug, aug], out_shape=[jax.ShapeDtypeStruct((S, H * FOX_AUG), BF16)] * 2,
        compiler_params=_params(),
    )(h, h, c)


def _causal(T):
    return lax.broadcasted_iota(jnp.int32, (T, T), 1) <= lax.broadcasted_iota(jnp.int32, (T, T), 0)


def _fox_fwd(qa, ka, h, H):
    S = h.shape[0]
    W = H * HEAD_DIM
    T = _tile(S, FOX_TILE, 128)
    nq = S // T
    k1 = HEAD_DIM ** -0.5 * LOG2E
    nt = (((1,), (1,)), ((), ()))

    def body(q_ref, k_ref, v_ref, z_ref, o_ref, g_ref, lse_ref, m_ref, l_ref, acc_ref):
        i, j = pl.program_id(1), pl.program_id(2)

        @pl.when(j == 0)
        def _():
            m_ref[...] = jnp.full_like(m_ref, NEG)
            l_ref[...] = jnp.zeros_like(l_ref)
            acc_ref[...] = jnp.zeros_like(acc_ref)

        def step(diag):
            raw = lax.dot_general(q_ref[...], k_ref[...], nt, preferred_element_type=F32)
            if diag:
                raw = jnp.where(_causal(T), raw, NEG)
            m_new = jnp.maximum(m_ref[...], jnp.max(raw, axis=1, keepdims=True))
            a = jnp.exp2((m_ref[...] - m_new) * k1)
            p = jnp.exp2((raw - m_new) * k1)
            l_ref[...] = a * l_ref[...] + jnp.sum(p, axis=1, keepdims=True)
            acc_ref[...] = a * acc_ref[...] + jnp.dot(p.astype(BF16), v_ref[...], preferred_element_type=F32)
            m_ref[...] = m_new

        @pl.when(j < i)
        def _():
            step(False)

        @pl.when(j == i)
        def _():
            step(True)

        @pl.when(j == nq - 1)
        def _():
            out = acc_ref[...] / l_ref[...]
            z = z_ref[...].astype(F32)
            o_ref[...] = out.astype(BF16)
            g_ref[...] = (out * z * _sigmoid(z)).astype(BF16)
            lse_ref[...] = jnp.broadcast_to(m_ref[...] * k1 + jnp.log(l_ref[...]) * LOG2E, (T, HEAD_DIM))

    out = pl.BlockSpec((T, HEAD_DIM), lambda hh, i, j: (i, hh))
    return pl.pallas_call(
        body, name="fox_fwd", grid=(H, nq, nq),
        in_specs=[pl.BlockSpec((T, FOX_AUG), lambda hh, i, j: (i, hh)),
                  pl.BlockSpec((T, FOX_AUG), lambda hh, i, j: (jnp.minimum(j, i), hh)),
                  pl.BlockSpec((T, HEAD_DIM), lambda hh, i, j: (jnp.minimum(j, i), 6 * H + hh)),
                  pl.BlockSpec((T, HEAD_DIM), lambda hh, i, j: (i, 7 * H + hh))],
        out_specs=[out, out, out],
        out_shape=[jax.ShapeDtypeStruct((S, W), BF16), jax.ShapeDtypeStruct((S, W), BF16),
                   jax.ShapeDtypeStruct((S, W), F32)],
        scratch_shapes=[pltpu.VMEM((T, 1), F32), pltpu.VMEM((T, 1), F32), pltpu.VMEM((T, HEAD_DIM), F32)],
        compiler_params=_params(),
    )(qa, ka, h, h)


def _fox_bwd(qa, ka, h, do, lse, dl, H):
    S = h.shape[0]
    W = H * HEAD_DIM
    T = _tile(S, FOX_TILE, 128)
    nq = S // T
    scale = HEAD_DIM ** -0.5
    k1 = scale * LOG2E
    nt = (((1,), (1,)), ((), ()))
    tn = (((0,), (0,)), ((), ()))

    def body(q_ref, k_ref, v_ref, do_ref, lse_ref, dl_ref, dq_ref, dk_ref, dv_ref, dcq_ref, dc_ref,
             ak_ref, av_ref, ac_ref):
        j, i = pl.program_id(1), pl.program_id(2)

        @pl.when((j == 0) & (i == 0))
        def _():
            dq_ref[...] = jnp.zeros_like(dq_ref)
            dcq_ref[...] = jnp.zeros_like(dcq_ref)

        @pl.when(i == 0)
        def _():
            ak_ref[...] = jnp.zeros_like(ak_ref)
            av_ref[...] = jnp.zeros_like(av_ref)
            ac_ref[...] = jnp.zeros_like(ac_ref)

        def step(diag):
            q, k, v, d_o = q_ref[...], k_ref[...], v_ref[...], do_ref[...]
            raw = lax.dot_general(q, k, nt, preferred_element_type=F32)
            if diag:
                raw = jnp.where(_causal(T), raw, NEG)
            p = jnp.exp2(raw * k1 - lse_ref[...][:, :1])
            dp = lax.dot_general(d_o, v, nt, preferred_element_type=F32)
            ds = p * (dp - dl_ref[...][:, :1])
            ds_b = ds.astype(BF16)
            av_ref[...] += lax.dot_general(p.astype(BF16), d_o, tn, preferred_element_type=F32)
            ak_ref[...] += lax.dot_general(ds_b, q[:, :HEAD_DIM], tn, preferred_element_type=F32)
            ac_ref[...] -= jnp.sum(ds, axis=0, keepdims=True)
            rows = pl.ds(pl.multiple_of(i * T, T), T)
            dq_ref[rows, :] += jnp.dot(ds_b, k[:, :HEAD_DIM], preferred_element_type=F32) * scale
            dcq_ref[rows, :] += jnp.broadcast_to(jnp.sum(ds, axis=1, keepdims=True), (T, HEAD_DIM))

        @pl.when(i > j)
        def _():
            step(False)

        @pl.when(i == j)
        def _():
            step(True)

        @pl.when(i == nq - 1)
        def _():
            dk_ref[...] = ak_ref[...] * scale
            dv_ref[...] = av_ref[...]
            dc_ref[...] = ac_ref[...]

    qrow = lambda hh, j, i: (jnp.maximum(i, j), hh)
    krow = lambda hh, j, i: (j, hh)
    head = pl.BlockSpec((S, HEAD_DIM), lambda hh, j, i: (0, hh))
    return pl.pallas_call(
        body, name="fox_bwd", grid=(H, nq, nq),
        in_specs=[pl.BlockSpec((T, FOX_AUG), qrow), pl.BlockSpec((T, FOX_AUG), krow),
                  pl.BlockSpec((T, HEAD_DIM), lambda hh, j, i: (j, 6 * H + hh)),
                  pl.BlockSpec((T, HEAD_DIM), qrow), pl.BlockSpec((T, HEAD_DIM), qrow), pl.BlockSpec((T, HEAD_DIM), qrow)],
        out_specs=[head, pl.BlockSpec((T, HEAD_DIM), krow), pl.BlockSpec((T, HEAD_DIM), krow), head,
                   pl.BlockSpec((None, 1, T), lambda hh, j, i: (hh, 0, j))],
        out_shape=[jax.ShapeDtypeStruct((S, W), F32), jax.ShapeDtypeStruct((S, W), F32),
                   jax.ShapeDtypeStruct((S, W), F32), jax.ShapeDtypeStruct((S, W), F32),
                   jax.ShapeDtypeStruct((H, 1, S), F32)],
        scratch_shapes=[pltpu.VMEM((T, HEAD_DIM), F32), pltpu.VMEM((T, HEAD_DIM), F32), pltpu.VMEM((1, T), F32)],
        compiler_params=_params(),
    )(qa, ka, h, do, lse, dl)


def _adamw_math(w, g, m, v):
    m = ADAM_B1 * m + (1.0 - ADAM_B1) * g
    v = ADAM_B2 * v + (1.0 - ADAM_B2) * (g * g)
    m_hat = m / (1.0 - ADAM_B1 ** ADAM_STEP)
    v_hat = v / (1.0 - ADAM_B2 ** ADAM_STEP)
    delta = -ADAM_LR * (m_hat / (jnp.sqrt(v_hat) + ADAM_EPS) + ADAM_WD * w)
    return delta, m, v


def _adamw(w, g_mine, g_theirs, m, v, name):
    L, R, C = w.shape
    half = L // 2
    tr = _rows(R, C * 4 * 9, 8)

    def body(c_ref, w_ref, gm_ref, gt_ref, m_ref, v_ref, g_ref, d_ref, nm_ref, nv_ref):
        g = jnp.where(pl.program_id(0) // half == c_ref[0], gm_ref[...], gt_ref[...])
        g_ref[...] = g
        d_ref[...], nm_ref[...], nv_ref[...] = _adamw_math(w_ref[...], g, m_ref[...], v_ref[...])

    blk = pl.BlockSpec((None, tr, C), lambda l, i, c: (l, i, 0))
    mine = pl.BlockSpec((None, tr, C), lambda l, i, c: (jnp.clip(l - c[0] * half, 0, half - 1), i, 0))
    theirs = pl.BlockSpec((None, tr, C), lambda l, i, c: (jnp.clip(l - (1 - c[0]) * half, 0, half - 1), i, 0))
    grid_spec = pltpu.PrefetchScalarGridSpec(
        num_scalar_prefetch=1, grid=(L, R // tr), in_specs=[blk, mine, theirs, blk, blk], out_specs=[blk] * 4)
    core = lax.axis_index("c").astype(jnp.int32).reshape(1)
    return pl.pallas_call(
        body, name=name, grid_spec=grid_spec, out_shape=[jax.ShapeDtypeStruct((L, R, C), F32)] * 4,
        compiler_params=_params(),
    )(core, w, g_mine, g_theirs, m, v)


def _place():
    x, y, c = lax.axis_index("x"), lax.axis_index("y"), lax.axis_index("c")
    return x, y, c, [(1 - x, y), (x, 1 - y), (1 - x, 1 - y)]


def _remote(src, dst, send_sems, recv_sems, k, to):
    return pltpu.make_async_remote_copy(src_ref=src, dst_ref=dst, send_sem=send_sems.at[k], recv_sem=recv_sems.at[k],
                                        device_id=to, device_id_type=MESH)


def _gather_weights(shards):
    n = len(shards)
    half = shards[0].shape[0] // 2

    def body(*refs):
        srcs, dsts = refs[:n], refs[n:2 * n]
        send_sems, recv_sems = refs[2 * n:]
        x, y, c, chips = _place()
        me = 2 * x + y
        mine, theirs = pl.ds(c * half, half), pl.ds((1 - c) * half, half)
        first = [_remote(srcs[a].at[mine], dsts[a].at[me, mine], send_sems, recv_sems, 6 * a + j, (px, py, c))
                 for a in range(n) for j, (px, py) in enumerate(chips)]
        for cp in first:
            cp.start()
        passed = []
        for a in range(n):
            for j, (px, py) in enumerate(chips):
                landed = dsts[a].at[2 * px + py, mine]
                _remote(landed, landed, send_sems, recv_sems, 6 * a + j, (px, py, c)).wait_recv()
                cp = _remote(landed, landed, send_sems, recv_sems, 6 * a + 3 + j, (x, y, 1 - c))
                cp.start()
                passed.append(cp)
        for a in range(n):
            for j, (px, py) in enumerate(chips):
                landed = dsts[a].at[2 * px + py, theirs]
                _remote(landed, landed, send_sems, recv_sems, 6 * a + 3 + j, (x, y, 1 - c)).wait_recv()
        for cp in first + passed:
            cp.wait_send()

    return pl.pallas_call(
        body, name="gather_weights", in_specs=[ANY] * n, out_specs=[ANY] * n,
        out_shape=[jax.ShapeDtypeStruct((4,) + s.shape, s.dtype) for s in shards],
        scratch_shapes=[pltpu.SemaphoreType.DMA((6 * n,)), pltpu.SemaphoreType.DMA((6 * n,))],
    )(*shards)


def _swap_other_half(parts):
    n = len(parts)
    half = parts[0].shape[1] // 2

    def body(*refs):
        srcs, dsts = refs[:n], refs[n:2 * n]
        send_sems, recv_sems = refs[2 * n:]
        x, y, c, _ = _place()
        cps = [_remote(srcs[a].at[:, pl.ds((1 - c) * half, half)], dsts[a], send_sems, recv_sems, a, (x, y, 1 - c))
               for a in range(n)]
        for cp in cps:
            cp.start()
        for cp in cps:
            cp.wait()

    return pl.pallas_call(
        body, name="grad_swap_half", in_specs=[ANY] * n, out_specs=[ANY] * n,
        out_shape=[jax.ShapeDtypeStruct((4, half) + p.shape[2:], p.dtype) for p in parts],
        scratch_shapes=[pltpu.SemaphoreType.DMA((n,)), pltpu.SemaphoreType.DMA((n,))],
    )(*parts)


def _add_half(part, got, name):
    _, half, R, C = got.shape
    tr = _rows(R, C * 2 * 3)

    def body(c_ref, p_ref, g_ref, o_ref):
        o_ref[...] = (p_ref[...].astype(F32) + g_ref[...].astype(F32)).astype(BF16)

    grid_spec = pltpu.PrefetchScalarGridSpec(
        num_scalar_prefetch=1, grid=(4, half, R // tr),
        in_specs=[pl.BlockSpec((None, None, tr, C), lambda s, l, i, c: (s, c[0] * half + l, i, 0)),
                  pl.BlockSpec((None, None, tr, C), lambda s, l, i, c: (s, l, i, 0))],
        out_specs=pl.BlockSpec((None, None, tr, C), lambda s, l, i, c: (s, l, i, 0)))
    core = lax.axis_index("c").astype(jnp.int32).reshape(1)
    return pl.pallas_call(
        body, name=name, grid_spec=grid_spec, out_shape=jax.ShapeDtypeStruct(got.shape, BF16),
        compiler_params=_params(),
    )(core, part, got)


def _scatter_to_owner(parts):
    n = len(parts)

    def body(*refs):
        srcs, dsts = refs[:n], refs[n:2 * n]
        send_sems, recv_sems, local_sems = refs[2 * n:]
        x, y, c, chips = _place()
        me = 2 * x + y
        local = [pltpu.make_async_copy(srcs[a].at[me], dsts[a].at[me], local_sems.at[a]) for a in range(n)]
        for cp in local:
            cp.start()
        sends = [_remote(srcs[a].at[2 * px + py], dsts[a].at[me], send_sems, recv_sems, 3 * a + j, (px, py, c))
                 for a in range(n) for j, (px, py) in enumerate(chips)]
        for cp in sends:
            cp.start()
        for a in range(n):
            for j, (px, py) in enumerate(chips):
                slot = dsts[a].at[2 * px + py]
                _remote(slot, slot, send_sems, recv_sems, 3 * a + j, (px, py, c)).wait_recv()
        for cp in sends:
            cp.wait_send()
        for cp in local:
            cp.wait()

    return pl.pallas_call(
        body, name="grad_scatter", in_specs=[ANY] * n, out_specs=[ANY] * n,
        out_shape=[jax.ShapeDtypeStruct(p.shape, p.dtype) for p in parts],
        scratch_shapes=[pltpu.SemaphoreType.DMA((3 * n,)), pltpu.SemaphoreType.DMA((3 * n,)),
                        pltpu.SemaphoreType.DMA((n,))],
    )(*parts)


def _sum_chips(got, name):
    _, half, R, C = got.shape
    tr = _rows(R, C * (2 * 4 + 4))

    def body(g_ref, o_ref):
        o_ref[...] = ((g_ref[0].astype(F32) + g_ref[1].astype(F32)) + g_ref[2].astype(F32)) + g_ref[3].astype(F32)

    return pl.pallas_call(
        body, name=name, grid=(half, R // tr),
        in_specs=[pl.BlockSpec((4, None, tr, C), lambda l, i: (0, l, i, 0))],
        out_specs=pl.BlockSpec((None, tr, C), lambda l, i: (l, i, 0)),
        out_shape=jax.ShapeDtypeStruct((half, R, C), F32), compiler_params=_params(),
    )(got)


def _share_halves(halves):
    n = len(halves)

    def body(*refs):
        srcs, dsts = refs[:n], refs[n:2 * n]
        send_sems, recv_sems = refs[2 * n:]
        x, y, c, _ = _place()
        cps = [_remote(srcs[a], dsts[a], send_sems, recv_sems, a, (x, y, 1 - c)) for a in range(n)]
        for cp in cps:
            cp.start()
        for cp in cps:
            cp.wait()

    return pl.pallas_call(
        body, name="grad_share_halves", in_specs=[ANY] * n, out_specs=[ANY] * n,
        out_shape=[jax.ShapeDtypeStruct(h.shape, h.dtype) for h in halves],
        scratch_shapes=[pltpu.SemaphoreType.DMA((n,)), pltpu.SemaphoreType.DMA((n,))],
    )(*halves)


def _small_allreduce_adamw(part, w, m, v):
    R = part.shape[0]
    deltas = [(dx, dy, dc) for dx in (0, 1) for dy in (0, 1) for dc in (0, 1)][1:]

    def body(p_ref, w_ref, m_ref, v_ref, g_ref, d_ref, nm_ref, nv_ref, all_ref, send_sems, recv_sems):
        x, y, c, _ = _place()
        me = 4 * x + 2 * y + c
        all_ref[me] = p_ref[...]
        cps = [_remote(p_ref, all_ref.at[me], send_sems, recv_sems, k, (x ^ dx, y ^ dy, c ^ dc))
               for k, (dx, dy, dc) in enumerate(deltas)]
        for cp in cps:
            cp.start()
        for k, (dx, dy, dc) in enumerate(deltas):
            slot = all_ref.at[4 * (x ^ dx) + 2 * (y ^ dy) + (c ^ dc)]
            _remote(slot, slot, send_sems, recv_sems, k, (x ^ dx, y ^ dy, c ^ dc)).wait_recv()
        for cp in cps:
            cp.wait_send()
        g = all_ref[0]
        for k in range(1, 8):
            g = g + all_ref[k]
        g_ref[...] = g
        d_ref[...], nm_ref[...], nv_ref[...] = _adamw_math(w_ref[...], g, m_ref[...], v_ref[...])

    vm = pl.BlockSpec(memory_space=pltpu.VMEM)
    shape = jax.ShapeDtypeStruct((R, LANES), F32)
    return pl.pallas_call(
        body, name="small_allreduce_adamw", in_specs=[vm] * 4, out_specs=[vm] * 4, out_shape=[shape] * 4,
        scratch_shapes=[pltpu.VMEM((8, R, LANES), F32), pltpu.SemaphoreType.DMA((7,)), pltpu.SemaphoreType.DMA((7,))],
    )(part, w, m, v)


def _pack_small(bf, bg, lg, lb, extra=None):
    L, H = bf.shape
    per = jnp.concatenate([jnp.pad(bf, ((0, 0), (0, LANES - H))), bg, lg, lb], axis=1)
    flat = per.reshape(-1, LANES)
    last = jnp.zeros((8 + (-flat.shape[0]) % 8, LANES), F32)
    if extra is not None:
        last = last.at[-8, 0].set(extra)
    return jnp.concatenate([flat, last], axis=0)


def _unpack_small(p, L, H, D):
    per = p[:L * (1 + 4 * D // LANES)].reshape(L, -1)
    return per[:, :H], per[:, LANES:LANES + 2 * D], per[:, LANES + 2 * D:LANES + 3 * D], per[:, LANES + 3 * D:]


def kernel(x, w_in, b_forget, b_gate, w_up_a, w_up_b, w_out, ln_g, ln_b, loss_target, m_w_in, m_b_forget, m_b_gate, m_w_up_a, m_w_up_b, m_w_out, m_ln_g, m_ln_b, v_w_in, v_b_forget, v_b_gate, v_w_up_a, v_w_up_b, v_w_out, v_ln_g, v_ln_b):
    _, S, D = x.shape
    L, _, C4 = w_in.shape
    H = b_forget.shape[1]
    W = w_up_a.shape[1]
    D4 = D // 4
    NC = 4 * C4
    assert W == H * HEAD_DIM and NC == 8 * W + H + 2 * D and L % 2 == 0 and D % LANES == 0
    alpha = float((2 * L) ** 0.25)
    f_block = 2 * D // LANES

    own = [w_in.astype(BF16), jnp.concatenate([w_up_a, w_up_b], axis=2).astype(BF16), w_out.astype(BF16)]
    gathered = _gather_weights(own)
    me = 2 * lax.axis_index("x") + lax.axis_index("y")
    shard = lambda a, s, l: jnp.where(me == s, own[a][l], gathered[a][s, l])
    w_main, w_fg, w_ua, w_ub, w_o = [], [], [], [], []
    for l in range(L):
        full = jnp.concatenate([shard(0, s, l) for s in range(4)], axis=1)
        w_main.append(full[:, :8 * W])
        w_fg.append(jnp.concatenate([full[:, 8 * W + H:], full[:, 8 * W:8 * W + H],
                                     jnp.zeros((D, LANES - H), BF16)], axis=1))
        ups = [shard(1, s, l) for s in range(4)]
        w_ua.append(jnp.concatenate([u[:, :D4] for u in ups], axis=1))
        w_ub.append(jnp.concatenate([u[:, D4:] for u in ups], axis=1))
        w_o.append(jnp.concatenate([shard(2, s, l) for s in range(4)], axis=0))

    pos = jnp.arange(S, dtype=F32)
    inv_freq = ROPE_THETA ** (-jnp.arange(HEAD_DIM // 2, dtype=F32) / (HEAD_DIM // 2))
    ang = pos[:, None] * inv_freq[None, :]
    cos = jnp.concatenate([jnp.cos(ang), jnp.cos(ang)], axis=1)
    sin = jnp.concatenate([-jnp.sin(ang), jnp.sin(ang)], axis=1)
    bf_pad = jnp.pad(b_forget, ((0, 0), (0, LANES - H)))

    xs = x[0]
    xb = xs.astype(BF16)
    saved = []
    for l in range(L):
        h = _matmul(xb, w_main[l], mode="nn", out_dtype=BF16, name="in_proj", rope=(cos, sin), rope_cols=2 * W)
        hfg = _matmul(xb, w_fg[l], mode="nn", out_dtype=F32, name="in_proj_gates")
        qkv_a = h[:, :3 * W]
        views = [qkv_a.reshape(S // d, d * 3 * W) for _, d in DILATED_PATTERNS]
        os, lses = [], []
        for (_, d), hv in zip(DILATED_PATTERNS, views):
            o, lse = _dil_fwd(hv, d, W, f"dil_fwd_d{d}")
            os.append(o.reshape(S, W))
            lses.append(lse.transpose(1, 0, 2).reshape(S, H))
        out_a, lse_a, ga = _dil_combine_fwd(os, lses, h, W)
        c = _scan_fwd(hfg, bf_pad[l:l + 1], f_block)
        qa, ka = _fox_prep(h, c, H)
        out_b, gb, lse_b = _fox_fwd(qa, ka, h, H)
        up_a = _matmul(ga, w_ua[l], mode="nn", out_dtype=BF16, name="up_proj")
        up_b = _matmul(gb, w_ub[l], mode="nn", out_dtype=BF16, name="up_proj")
        u = _merge_fwd(up_a, up_b, hfg, b_gate[l:l + 1])
        r = _matmul(u, w_o[l], mode="nn", out_dtype=F32, name="out_proj", acc_in=xs, acc_scale=alpha)
        saved.append((xb, h, hfg, views, out_a, lse_a, ga, qa, ka, out_b, gb, lse_b, up_a, up_b, u, r))
        xs, xb = _ln_fwd(r, ln_g[l:l + 1], ln_b[l:l + 1])

    dx, sq = _loss(xs, loss_target[0])
    loss_part = 0.5 * jnp.sum(sq) / D

    g_in, g_up, g_out, g_bf, g_bg, g_lg, g_lb = [], [], [], [], [], [], []
    for l in reversed(range(L)):
        xb, h, hfg, views, out_a, lse_a, ga, qa, ka, out_b, gb, lse_b, up_a, up_b, u, r = saved[l]
        dr, adr, dlg, dlb = _ln_bwd(dx, r, ln_g[l:l + 1], alpha)
        du = _matmul(dr, w_o[l], mode="nt", out_dtype=F32, name="out_proj_dx")
        dwo = _matmul(u, dr, mode="tn", out_dtype=F32, name="out_proj_dw")
        dua, dub, dgl, dbg = _merge_bwd(du, up_a, up_b, hfg, b_gate[l:l + 1])
        dga = _matmul(dua, w_ua[l], mode="nt", out_dtype=F32, name="up_proj_dx")
        dgb = _matmul(dub, w_ub[l], mode="nt", out_dtype=F32, name="up_proj_dx")
        dwua = _matmul(ga, dua, mode="tn", out_dtype=F32, name="up_proj_dw")
        dwub = _matmul(gb, dub, mode="tn", out_dtype=F32, name="up_proj_dw")
        do_a, dz_a, dl_a = _gate_bwd(dga, out_a, h, 3, True, "gate_bwd_a")
        dqs, dks, dvs = [], [], []
        for (_, d), hv in zip(DILATED_PATTERNS, views):
            stat = lambda t: t.reshape(S // d, d, H).transpose(1, 0, 2)
            dq, dk, dv = _dil_bwd(hv, do_a.reshape(S // d, d * W), stat(lse_a), stat(dl_a), d, W, f"dil_bwd_d{d}")
            dqs.append(dq.reshape(S, W))
            dks.append(dk.reshape(S, W))
            dvs.append(dv.reshape(S, W))
        dqkv_a = _dil_combine_bwd(dqs, dks, dvs, cos, sin)
        do_b, dz_b, dl_b = _gate_bwd(dgb, out_b, h, 7, False, "gate_bwd_b")
        dq_b, dk_b, dv_b, dcq, dct = _fox_bwd(qa, ka, h, do_b, lse_b, dl_b, H)
        dc = jnp.pad(dcq.reshape(S, H, HEAD_DIM)[:, :, 0] + dct.reshape(H, S).T, ((0, 0), (0, LANES - H)))
        df, dbf = _scan_bwd(dc, hfg, bf_pad[l:l + 1], f_block, H)
        dh = jnp.concatenate([dqkv_a, dz_a, dq_b.astype(BF16), dk_b.astype(BF16), dv_b.astype(BF16), dz_b], axis=1)
        dhfg = jnp.concatenate([dgl, df], axis=1)
        dx1 = _matmul(dh, w_main[l], mode="nt", out_dtype=F32, name="in_proj_dx", acc_in=adr)
        dx = _matmul(dhfg, w_fg[l], mode="nt", out_dtype=F32, name="in_proj_gates_dx", acc_in=dx1)
        dwm = _matmul(xb, dh, mode="tn", out_dtype=F32, name="in_proj_dw")
        dwfg = _matmul(xb, dhfg, mode="tn", out_dtype=F32, name="in_proj_gates_dw")
        full = jnp.concatenate([dwm, dwfg[:, 2 * D:2 * D + H], dwfg[:, :2 * D]], axis=1)
        g_in.append(full.reshape(D, 4, C4).transpose(1, 0, 2).astype(BF16))
        g_up.append(jnp.concatenate([dwua.reshape(W, 4, D4), dwub.reshape(W, 4, D4)], axis=2).transpose(1, 0, 2).astype(BF16))
        g_out.append(dwo.reshape(4, D4, D).astype(BF16))
        g_bf.append(dbf[0, :H])
        g_bg.append(dbg[0])
        g_lg.append(dlg[0])
        g_lb.append(dlb[0])
    grad_x = dx[None]
    for lst in (g_in, g_up, g_out, g_bf, g_bg, g_lg, g_lb):
        lst.reverse()

    parts = [jnp.stack(g_in, axis=1), jnp.stack(g_up, axis=1), jnp.stack(g_out, axis=1)]
    names = ["w_in", "w_up", "w_out"]
    got = _swap_other_half(parts)
    chip = [_add_half(p, g, f"grad_add_half_{n}") for p, g, n in zip(parts, got, names)]
    landed = _scatter_to_owner(chip)
    halves = [_sum_chips(g, f"grad_sum_chips_{n}") for g, n in zip(landed, names)]
    theirs = _share_halves(halves)

    pair = lambda a, b: jnp.concatenate([a, b], axis=2)
    grad_w_in, d_in, nm_in, nv_in = _adamw(w_in, halves[0], theirs[0], m_w_in, v_w_in, "adamw_w_in")
    up = _adamw(pair(w_up_a, w_up_b), halves[1], theirs[1], pair(m_w_up_a, m_w_up_b), pair(v_w_up_a, v_w_up_b),
                "adamw_w_up")
    (grad_w_up_a, grad_w_up_b), (d_ua, d_ub), (nm_ua, nm_ub), (nv_ua, nv_ub) = [
        (t[:, :, :D4], t[:, :, D4:]) for t in up]
    grad_w_out, d_o, nm_o, nv_o = _adamw(w_out, halves[2], theirs[2], m_w_out, v_w_out, "adamw_w_out")

    small_g = _pack_small(jnp.stack(g_bf), jnp.stack(g_bg), jnp.stack(g_lg), jnp.stack(g_lb), loss_part)
    small = _small_allreduce_adamw(small_g, _pack_small(b_forget, b_gate, ln_g, ln_b),
                                   _pack_small(m_b_forget, m_b_gate, m_ln_g, m_ln_b),
                                   _pack_small(v_b_forget, v_b_gate, v_ln_g, v_ln_b))
    loss = small[0][-8, 0]
    (g_bf, g_bg, g_lg, g_lb), (d_bf, d_bg, d_lg, d_lb), (nm_bf, nm_bg, nm_lg, nm_lb), (nv_bf, nv_bg, nv_lg, nv_lb) = [
        _unpack_small(p, L, H, D) for p in small]

    return (loss, grad_x,
            grad_w_in, g_bf, g_bg, grad_w_up_a, grad_w_up_b, grad_w_out, g_lg, g_lb,
            d_in, d_bf, d_bg, d_ua, d_ub, d_o, d_lg, d_lb,
            nm_in, nm_bf, nm_bg, nm_ua, nm_ub, nm_o, nm_lg, nm_lb,
            nv_in, nv_bf, nv_bg, nv_ua, nv_ub, nv_o, nv_lg, nv_lb)
```

```python
import functools

import jax
import jax.numpy as jnp
from jax import lax
from jax.experimental import pallas as pl
from jax.experimental.pallas import tpu as pltpu

F32 = jnp.float32
BF16 = jnp.bfloat16
MESH = pl.DeviceIdType.MESH
ANY = pl.BlockSpec(memory_space=pl.ANY)

HEAD_DIM = 128
LANES = 128
Q_BLOCK = 128
DILATED_PATTERNS = ((128, 1), (512, 4), (2048, 16))
ROPE_THETA = 10000.0
LN_EPS = 1e-5
ADAM_LR, ADAM_B1, ADAM_B2, ADAM_EPS, ADAM_WD, ADAM_STEP = 0.001, 0.9, 0.999, 1e-08, 0.01, 10
NEG = -1e30
VMEM_LIMIT = 56 * 2**20
ELEMENTWISE_BUDGET = 20 * 2**20
FOX_TILE = 512
MM_TILES = (1024, 1024, 512)


def _params():
    return pltpu.CompilerParams(vmem_limit_bytes=VMEM_LIMIT)


def _tile(dim, target, align):
    if dim <= target:
        return dim
    t = (target // align) * align
    while t >= align:
        if dim % t == 0:
            return t
        t -= align
    return dim


def _rows(n_rows, bytes_per_row, align=16):
    return _tile(n_rows, max(align, ELEMENTWISE_BUDGET // (2 * bytes_per_row)), align)


def _sigmoid(v):
    return 1.0 / (1.0 + jnp.exp(-v))


def _matmul(a, b, *, mode, out_dtype, name, acc_in=None, acc_scale=1.0, rope=None, rope_cols=0):
    if mode == "nn":
        (M, K), (K2, N) = a.shape, b.shape
    elif mode == "nt":
        (M, K), (N, K2) = a.shape, b.shape
    else:
        (K, M), (K2, N) = a.shape, b.shape
    assert K == K2, (a.shape, b.shape, mode)
    tm, tn, tk = _tile(M, MM_TILES[0], 128), _tile(N, MM_TILES[1], 128), _tile(K, MM_TILES[2], 128)
    if rope is not None:
        assert rope_cols % tn == 0
    nk = K // tk
    n_rope_tiles = rope_cols // tn if rope is not None else 0
    dims = {"nn": (((1,), (0,)), ((), ())), "nt": (((1,), (1,)), ((), ())), "tn": (((0,), (0,)), ((), ()))}[mode]

    def body(*refs):
        a_ref, b_ref = refs[0], refs[1]
        pos = 2
        if rope is not None:
            cos_ref, sin_ref = refs[pos], refs[pos + 1]
            pos += 2
        if acc_in is not None:
            acc_in_ref = refs[pos]
            pos += 1
        o_ref, acc_ref = refs[pos], refs[pos + 1]
        j, k = pl.program_id(1), pl.program_id(2)

        @pl.when(k == 0)
        def _():
            acc_ref[...] = jnp.zeros_like(acc_ref)

        acc_ref[...] += lax.dot_general(a_ref[...], b_ref[...], dims, preferred_element_type=F32)

        def finish(rotate):
            r = acc_ref[...]
            if acc_in is not None:
                r = r + acc_scale * acc_in_ref[...]
            if rotate:
                cos, sin = cos_ref[...], sin_ref[...]
                for g in range(tn // HEAD_DIM):
                    sl = slice(g * HEAD_DIM, (g + 1) * HEAD_DIM)
                    t = r[:, sl]
                    o_ref[:, sl] = (t * cos + pltpu.roll(t, HEAD_DIM // 2, 1) * sin).astype(o_ref.dtype)
            else:
                o_ref[...] = r.astype(o_ref.dtype)

        if n_rope_tiles:
            @pl.when((k == nk - 1) & (j < n_rope_tiles))
            def _():
                finish(True)

            @pl.when((k == nk - 1) & (j >= n_rope_tiles))
            def _():
                finish(False)
        else:
            @pl.when(k == nk - 1)
            def _():
                finish(False)

    if mode == "nn":
        in_specs = [pl.BlockSpec((tm, tk), lambda i, j, k: (i, k)), pl.BlockSpec((tk, tn), lambda i, j, k: (k, j))]
    elif mode == "nt":
        in_specs = [pl.BlockSpec((tm, tk), lambda i, j, k: (i, k)), pl.BlockSpec((tn, tk), lambda i, j, k: (j, k))]
    else:
        in_specs = [pl.BlockSpec((tk, tm), lambda i, j, k: (k, i)), pl.BlockSpec((tk, tn), lambda i, j, k: (k, j))]
    args = [a, b]
    if rope is not None:
        in_specs += [pl.BlockSpec((tm, HEAD_DIM), lambda i, j, k: (i, 0))] * 2
        args += list(rope)
    if acc_in is not None:
        in_specs.append(pl.BlockSpec((tm, tn), lambda i, j, k: (i, j)))
        args.append(acc_in)
    return pl.pallas_call(
        body, name=name, grid=(M // tm, N // tn, nk), in_specs=in_specs,
        out_specs=pl.BlockSpec((tm, tn), lambda i, j, k: (i, j)),
        out_shape=jax.ShapeDtypeStruct((M, N), out_dtype),
        scratch_shapes=[pltpu.VMEM((tm, tn), F32)], compiler_params=_params(),
    )(*args)


def _ln_fwd(r, g, b):
    S, D = r.shape
    tm = _rows(S, D * (4 + 4 + 2))

    def body(r_ref, g_ref, b_ref, x_ref, xb_ref):
        v = r_ref[...]
        mu = jnp.mean(v, axis=1, keepdims=True)
        cen = v - mu
        var = jnp.mean(cen * cen, axis=1, keepdims=True)
        out = cen * lax.rsqrt(var + LN_EPS) * g_ref[...] + b_ref[...]
        x_ref[...] = out
        xb_ref[...] = out.astype(BF16)

    row = pl.BlockSpec((tm, D), lambda i: (i, 0))
    vec = pl.BlockSpec((1, D), lambda i: (0, 0))
    return pl.pallas_call(
        body, name="ln_fwd", grid=(S // tm,), in_specs=[row, vec, vec], out_specs=[row, row],
        out_shape=[jax.ShapeDtypeStruct((S, D), F32), jax.ShapeDtypeStruct((S, D), BF16)],
        compiler_params=_params(),
    )(r, g, b)


def _ln_bwd(dx, r, g, alpha):
    S, D = r.shape
    tm = _rows(S, D * (4 + 4 + 2 + 4))

    def body(dx_ref, r_ref, g_ref, drb_ref, adr_ref, dg_ref, db_ref):
        @pl.when(pl.program_id(0) == 0)
        def _():
            dg_ref[...] = jnp.zeros_like(dg_ref)
            db_ref[...] = jnp.zeros_like(db_ref)

        v, d = r_ref[...], dx_ref[...]
        mu = jnp.mean(v, axis=1, keepdims=True)
        cen = v - mu
        var = jnp.mean(cen * cen, axis=1, keepdims=True)
        rstd = lax.rsqrt(var + LN_EPS)
        xhat = cen * rstd
        dxhat = d * g_ref[...]
        dr = rstd * (dxhat - jnp.mean(dxhat, axis=1, keepdims=True)
                     - xhat * jnp.mean(dxhat * xhat, axis=1, keepdims=True))
        drb_ref[...] = dr.astype(BF16)
        adr_ref[...] = alpha * dr
        dg_ref[...] += jnp.sum(d * xhat, axis=0, keepdims=True)
        db_ref[...] += jnp.sum(d, axis=0, keepdims=True)

    row = pl.BlockSpec((tm, D), lambda i: (i, 0))
    vec = pl.BlockSpec((1, D), lambda i: (0, 0))
    return pl.pallas_call(
        body, name="ln_bwd", grid=(S // tm,), in_specs=[row, row, vec], out_specs=[row, row, vec, vec],
        out_shape=[jax.ShapeDtypeStruct((S, D), BF16), jax.ShapeDtypeStruct((S, D), F32),
                   jax.ShapeDtypeStruct((1, D), F32), jax.ShapeDtypeStruct((1, D), F32)],
        compiler_params=_params(),
    )(dx, r, g)


def _loss(y, target):
    S, D = y.shape
    tm = _rows(S, D * 12)

    def body(y_ref, t_ref, dy_ref, sq_ref):
        @pl.when(pl.program_id(0) == 0)
        def _():
            sq_ref[...] = jnp.zeros_like(sq_ref)

        err = y_ref[...] - t_ref[...]
        dy_ref[...] = err * (1.0 / D)
        sq_ref[...] += jnp.sum(err * err, axis=0, keepdims=True)

    row = pl.BlockSpec((tm, D), lambda i: (i, 0))
    vec = pl.BlockSpec((1, D), lambda i: (0, 0))
    return pl.pallas_call(
        body, name="loss", grid=(S // tm,), in_specs=[row, row], out_specs=[row, vec],
        out_shape=[jax.ShapeDtypeStruct((S, D), F32), jax.ShapeDtypeStruct((1, D), F32)],
        compiler_params=_params(),
    )(y, target)


def _merge_fwd(up_a, up_b, hfg, b_gate):
    S, D = up_a.shape
    tm = _rows(S, D * (2 + 2 + 4 + 4 + 2))

    def body(ua_ref, ub_ref, gla_ref, glb_ref, bga_ref, bgb_ref, u_ref):
        ga = _sigmoid(gla_ref[...] + bga_ref[...])
        gb = _sigmoid(glb_ref[...] + bgb_ref[...])
        u_ref[...] = (ga * ua_ref[...].astype(F32) + gb * ub_ref[...].astype(F32)).astype(BF16)

    row = pl.BlockSpec((tm, D), lambda i: (i, 0))
    row1 = pl.BlockSpec((tm, D), lambda i: (i, 1))
    v0 = pl.BlockSpec((1, D), lambda i: (0, 0))
    v1 = pl.BlockSpec((1, D), lambda i: (0, 1))
    return pl.pallas_call(
        body, name="merge_fwd", grid=(S // tm,), in_specs=[row, row, row, row1, v0, v1], out_specs=row,
        out_shape=jax.ShapeDtypeStruct((S, D), BF16), compiler_params=_params(),
    )(up_a, up_b, hfg, hfg, b_gate, b_gate)


def _merge_bwd(du, up_a, up_b, hfg, b_gate):
    S, D = up_a.shape
    tm = _rows(S, D * (4 + 2 + 2 + 4 + 4 + 2 + 2 + 4))

    def body(du_ref, ua_ref, ub_ref, gla_ref, glb_ref, bga_ref, bgb_ref, dua_ref, dub_ref, dgl_ref, dbg_ref):
        @pl.when(pl.program_id(0) == 0)
        def _():
            dbg_ref[...] = jnp.zeros_like(dbg_ref)

        du = du_ref[...]
        ga = _sigmoid(gla_ref[...] + bga_ref[...])
        gb = _sigmoid(glb_ref[...] + bgb_ref[...])
        dua_ref[...] = (du * ga).astype(BF16)
        dub_ref[...] = (du * gb).astype(BF16)
        dgla = du * ua_ref[...].astype(F32) * ga * (1.0 - ga)
        dglb = du * ub_ref[...].astype(F32) * gb * (1.0 - gb)
        dgl_ref[:, :D] = dgla.astype(BF16)
        dgl_ref[:, D:] = dglb.astype(BF16)
        dbg_ref[:, :D] += jnp.sum(dgla, axis=0, keepdims=True)
        dbg_ref[:, D:] += jnp.sum(dglb, axis=0, keepdims=True)

    row = pl.BlockSpec((tm, D), lambda i: (i, 0))
    row1 = pl.BlockSpec((tm, D), lambda i: (i, 1))
    v0 = pl.BlockSpec((1, D), lambda i: (0, 0))
    v1 = pl.BlockSpec((1, D), lambda i: (0, 1))
    return pl.pallas_call(
        body, name="merge_bwd", grid=(S // tm,), in_specs=[row, row, row, row, row1, v0, v1],
        out_specs=[row, row, pl.BlockSpec((tm, 2 * D), lambda i: (i, 0)), pl.BlockSpec((1, 2 * D), lambda i: (0, 0))],
        out_shape=[jax.ShapeDtypeStruct((S, D), BF16), jax.ShapeDtypeStruct((S, D), BF16),
                   jax.ShapeDtypeStruct((S, 2 * D), BF16), jax.ShapeDtypeStruct((1, 2 * D), F32)],
        compiler_params=_params(),
    )(du, up_a, up_b, hfg, hfg, b_gate, b_gate)


def _gate_bwd(dg, out, h, z_block, compact, name):
    S, W = out.shape
    H = W // HEAD_DIM
    tm = _rows(S, W * (4 + 2 + 2 + 2 + 2 + 4))

    def body(dg_ref, o_ref, z_ref, do_ref, dz_ref, dl_ref):
        z = z_ref[...].astype(F32)
        o = o_ref[...].astype(F32)
        d = dg_ref[...]
        sg = _sigmoid(z)
        dout = d * z * sg
        do_ref[...] = dout.astype(BF16)
        dz_ref[...] = (d * o * sg * (1.0 + z * (1.0 - sg))).astype(BF16)
        prod = dout * o
        for hh in range(H):
            sl = slice(hh * HEAD_DIM, (hh + 1) * HEAD_DIM)
            tot = jnp.sum(prod[:, sl], axis=1, keepdims=True)
            if compact:
                dl_ref[:, hh:hh + 1] = tot
            else:
                dl_ref[:, sl] = jnp.broadcast_to(tot, (tm, HEAD_DIM))

    row = pl.BlockSpec((tm, W), lambda i: (i, 0))
    dl_cols = H if compact else W
    return pl.pallas_call(
        body, name=name, grid=(S // tm,),
        in_specs=[row, row, pl.BlockSpec((tm, W), lambda i: (i, z_block))],
        out_specs=[row, row, pl.BlockSpec((tm, dl_cols), lambda i: (i, 0))],
        out_shape=[jax.ShapeDtypeStruct((S, W), BF16), jax.ShapeDtypeStruct((S, W), BF16),
                   jax.ShapeDtypeStruct((S, dl_cols), F32)],
        compiler_params=_params(),
    )(dg, out, h)


def _dil_masks(n):
    i = lax.broadcasted_iota(jnp.int32, (Q_BLOCK, Q_BLOCK), 0)
    j = lax.broadcasted_iota(jnp.int32, (Q_BLOCK, Q_BLOCK), 1)
    return j <= i, (j >= i) & (n > 0)


def _dil_fwd(hv, d, W, name):
    L = hv.shape[0]
    H = W // HEAD_DIM
    nblk = L // Q_BLOCK
    scale = HEAD_DIM ** -0.5
    nt = (((1,), (1,)), ((), ()))

    def body(q_ref, kp_ref, kc_ref, vp_ref, vc_ref, o_ref, lse_ref):
        mc, mp = _dil_masks(pl.program_id(1))
        for hh in range(H):
            sl = slice(hh * HEAD_DIM, (hh + 1) * HEAD_DIM)
            q = q_ref[:, sl]
            sc = jnp.where(mc, lax.dot_general(q, kc_ref[:, sl], nt, preferred_element_type=F32) * scale, NEG)
            sp = jnp.where(mp, lax.dot_general(q, kp_ref[:, sl], nt, preferred_element_type=F32) * scale, NEG)
            m = jnp.maximum(jnp.max(sc, axis=1, keepdims=True), jnp.max(sp, axis=1, keepdims=True))
            pc, pp = jnp.exp(sc - m), jnp.exp(sp - m)
            den = jnp.sum(pc, axis=1, keepdims=True) + jnp.sum(pp, axis=1, keepdims=True)
            acc = (jnp.dot(pc.astype(BF16), vc_ref[:, sl], preferred_element_type=F32)
                   + jnp.dot(pp.astype(BF16), vp_ref[:, sl], preferred_element_type=F32))
            o_ref[:, sl] = (acc / den).astype(BF16)
            lse_ref[:, hh:hh + 1] = m + jnp.log(den)

    def spec(col, prev):
        if prev:
            return pl.BlockSpec((Q_BLOCK, W), lambda r, n: (jnp.maximum(n - 1, 0), r * 3 + col))
        return pl.BlockSpec((Q_BLOCK, W), lambda r, n: (n, r * 3 + col))

    return pl.pallas_call(
        body, name=name, grid=(d, nblk),
        in_specs=[spec(0, False), spec(1, True), spec(1, False), spec(2, True), spec(2, False)],
        out_specs=[pl.BlockSpec((Q_BLOCK, W), lambda r, n: (n, r)),
                   pl.BlockSpec((None, Q_BLOCK, H), lambda r, n: (r, n, 0))],
        out_shape=[jax.ShapeDtypeStruct((L, d * W), BF16), jax.ShapeDtypeStruct((d, L, H), F32)],
        compiler_params=_params(),
    )(hv, hv, hv, hv, hv)


def _dil_combine_fwd(os, lses, h, W):
    S = h.shape[0]
    H = W // HEAD_DIM
    tm = _rows(S, W * (3 * 2 + 2 + 2 + 2) + 4 * H * 4)

    def body(o1, o2, o3, l1, l2, l3, z_ref, out_ref, lse_ref, g_ref):
        a, b, c = l1[...], l2[...], l3[...]
        m = jnp.maximum(jnp.maximum(a, b), c)
        ea, eb, ec = jnp.exp(a - m), jnp.exp(b - m), jnp.exp(c - m)
        den = ea + eb + ec
        wa, wb, wc = ea / den, eb / den, ec / den
        lse_ref[...] = m + jnp.log(den)
        for hh in range(H):
            sl = slice(hh * HEAD_DIM, (hh + 1) * HEAD_DIM)
            out = (wa[:, hh:hh + 1] * o1[:, sl].astype(F32) + wb[:, hh:hh + 1] * o2[:, sl].astype(F32)
                   + wc[:, hh:hh + 1] * o3[:, sl].astype(F32))
            z = z_ref[:, sl].astype(F32)
            out_ref[:, sl] = out.astype(BF16)
            g_ref[:, sl] = (out * z * _sigmoid(z)).astype(BF16)

    row = pl.BlockSpec((tm, W), lambda i: (i, 0))
    stat = pl.BlockSpec((tm, H), lambda i: (i, 0))
    return pl.pallas_call(
        body, name="dil_combine_fwd", grid=(S // tm,),
        in_specs=[row] * 3 + [stat] * 3 + [pl.BlockSpec((tm, W), lambda i: (i, 3))], out_specs=[row, stat, row],
        out_shape=[jax.ShapeDtypeStruct((S, W), BF16), jax.ShapeDtypeStruct((S, H), F32),
                   jax.ShapeDtypeStruct((S, W), BF16)],
        compiler_params=_params(),
    )(*os, *lses, h)


def _dil_bwd(hv, dov, lsev, dlv, d, W, name):
    L = hv.shape[0]
    H = W // HEAD_DIM
    nblk = L // Q_BLOCK
    scale = HEAD_DIM ** -0.5
    nt = (((1,), (1,)), ((), ()))
    tn = (((0,), (0,)), ((), ()))

    def body(q_ref, kp_ref, kc_ref, vp_ref, vc_ref, do_ref, lse_ref, dl_ref, dq_ref, dk_ref, dv_ref, ck_ref, cv_ref):
        n = pl.program_id(1)

        @pl.when(n == 0)
        def _():
            ck_ref[...] = jnp.zeros_like(ck_ref)
            cv_ref[...] = jnp.zeros_like(cv_ref)

        @pl.when(n < nblk)
        def _():
            mc, mp = _dil_masks(n)
            for hh in range(H):
                sl = slice(hh * HEAD_DIM, (hh + 1) * HEAD_DIM)
                q, do = q_ref[:, sl], do_ref[:, sl]
                kc, kp, vc, vp = kc_ref[:, sl], kp_ref[:, sl], vc_ref[:, sl], vp_ref[:, sl]
                lse, dl = lse_ref[:, hh:hh + 1], dl_ref[:, hh:hh + 1]
                sc = jnp.where(mc, lax.dot_general(q, kc, nt, preferred_element_type=F32) * scale, NEG)
                sp = jnp.where(mp, lax.dot_general(q, kp, nt, preferred_element_type=F32) * scale, NEG)
                pc, pp = jnp.exp(sc - lse), jnp.exp(sp - lse)
                dsc = pc * (lax.dot_general(do, vc, nt, preferred_element_type=F32) - dl) * scale
                dsp = pp * (lax.dot_general(do, vp, nt, preferred_element_type=F32) - dl) * scale
                dsc_b, dsp_b = dsc.astype(BF16), dsp.astype(BF16)
                dq_ref[:, sl] = (jnp.dot(dsc_b, kc, preferred_element_type=F32)
                                 + jnp.dot(dsp_b, kp, preferred_element_type=F32)).astype(BF16)
                dk_ref[:, sl] = (ck_ref[:, sl] + lax.dot_general(dsp_b, q, tn, preferred_element_type=F32)).astype(BF16)
                dv_ref[:, sl] = (cv_ref[:, sl]
                                 + lax.dot_general(pp.astype(BF16), do, tn, preferred_element_type=F32)).astype(BF16)
                ck_ref[:, sl] = lax.dot_general(dsc_b, q, tn, preferred_element_type=F32)
                cv_ref[:, sl] = lax.dot_general(pc.astype(BF16), do, tn, preferred_element_type=F32)

        @pl.when(n == nblk)
        def _():
            dk_ref[...] = ck_ref[...].astype(BF16)
            dv_ref[...] = cv_ref[...].astype(BF16)

    last = nblk - 1

    def hspec(col, prev):
        if prev:
            return pl.BlockSpec((Q_BLOCK, W), lambda r, n: (jnp.clip(n - 1, 0, last), r * 3 + col))
        return pl.BlockSpec((Q_BLOCK, W), lambda r, n: (jnp.minimum(n, last), r * 3 + col))

    cur = pl.BlockSpec((Q_BLOCK, W), lambda r, n: (jnp.minimum(n, last), r))
    lag = pl.BlockSpec((Q_BLOCK, W), lambda r, n: (jnp.maximum(n - 1, 0), r))
    stat = pl.BlockSpec((None, Q_BLOCK, H), lambda r, n: (r, jnp.minimum(n, last), 0))
    shape = jax.ShapeDtypeStruct((L, d * W), BF16)
    return pl.pallas_call(
        body, name=name, grid=(d, nblk + 1),
        in_specs=[hspec(0, False), hspec(1, True), hspec(1, False), hspec(2, True), hspec(2, False), cur, stat, stat],
        out_specs=[cur, lag, lag], out_shape=[shape, shape, shape],
        scratch_shapes=[pltpu.VMEM((Q_BLOCK, W), F32), pltpu.VMEM((Q_BLOCK, W), F32)],
        compiler_params=_params(),
    )(hv, hv, hv, hv, hv, dov, lsev, dlv)


def _dil_combine_bwd(dqs, dks, dvs, cos, sin):
    S, W = dqs[0].shape
    tm = _rows(S, W * (9 * 2 + 3 * 2 + 3 * 4))

    def body(q1, q2, q3, k1, k2, k3, v1, v2, v3, cos_ref, sin_ref, o_ref):
        cos_t, sin_t = cos_ref[...], -sin_ref[...]
        add3 = lambda a, b, c: a[...].astype(F32) + b[...].astype(F32) + c[...].astype(F32)
        dq = add3(q1, q2, q3)
        dk = add3(k1, k2, k3)
        for hh in range(W // HEAD_DIM):
            sl = slice(hh * HEAD_DIM, (hh + 1) * HEAD_DIM)
            tq, tk = dq[:, sl], dk[:, sl]
            o_ref[:, hh * HEAD_DIM:(hh + 1) * HEAD_DIM] = (
                tq * cos_t + pltpu.roll(tq, HEAD_DIM // 2, 1) * sin_t).astype(BF16)
            o_ref[:, W + hh * HEAD_DIM:W + (hh + 1) * HEAD_DIM] = (
                tk * cos_t + pltpu.roll(tk, HEAD_DIM // 2, 1) * sin_t).astype(BF16)
        o_ref[:, 2 * W:] = add3(v1, v2, v3).astype(BF16)

    row = pl.BlockSpec((tm, W), lambda i: (i, 0))
    tab = pl.BlockSpec((tm, HEAD_DIM), lambda i: (i, 0))
    return pl.pallas_call(
        body, name="dil_combine_bwd", grid=(S // tm,), in_specs=[row] * 9 + [tab, tab],
        out_specs=pl.BlockSpec((tm, 3 * W), lambda i: (i, 0)),
        out_shape=jax.ShapeDtypeStruct((S, 3 * W), BF16), compiler_params=_params(),
    )(*dqs, *dks, *dvs, cos, sin)


def _scan_tile(S):
    return _tile(S, 256, 8)


def _scan_fwd(hfg, bf_pad, f_block):
    S = hfg.shape[0]
    tm = _scan_tile(S)

    def body(f_ref, b_ref, c_ref, carry_ref):
        @pl.when(pl.program_id(0) == 0)
        def _():
            carry_ref[...] = jnp.zeros_like(carry_ref)

        v = f_ref[...] + b_ref[...]
        logf = jnp.minimum(v, 0.0) - jnp.log(1.0 + jnp.exp(-jnp.abs(v)))
        tri = (lax.broadcasted_iota(jnp.int32, (tm, tm), 1) <= lax.broadcasted_iota(jnp.int32, (tm, tm), 0)).astype(F32)
        c = jnp.dot(tri, logf, preferred_element_type=F32, precision=lax.Precision.HIGHEST) + carry_ref[...]
        c_ref[...] = c
        carry_ref[...] = c[tm - 1:tm, :]

    return pl.pallas_call(
        body, name="scan_fwd", grid=(S // tm,),
        in_specs=[pl.BlockSpec((tm, LANES), lambda i: (i, f_block)), pl.BlockSpec((1, LANES), lambda i: (0, 0))],
        out_specs=pl.BlockSpec((tm, LANES), lambda i: (i, 0)),
        out_shape=jax.ShapeDtypeStruct((S, LANES), F32),
        scratch_shapes=[pltpu.VMEM((1, LANES), F32)], compiler_params=_params(),
    )(hfg, bf_pad)


def _scan_bwd(dc, hfg, bf_pad, f_block, n_heads):
    S = hfg.shape[0]
    tm = _scan_tile(S)
    nt = S // tm

    def body(dc_ref, f_ref, b_ref, df_ref, db_ref, carry_ref):
        @pl.when(pl.program_id(0) == 0)
        def _():
            carry_ref[...] = jnp.zeros_like(carry_ref)
            db_ref[...] = jnp.zeros_like(db_ref)

        tri = (lax.broadcasted_iota(jnp.int32, (tm, tm), 1) >= lax.broadcasted_iota(jnp.int32, (tm, tm), 0)).astype(F32)
        dlogf = jnp.dot(tri, dc_ref[...], preferred_element_type=F32, precision=lax.Precision.HIGHEST) + carry_ref[...]
        carry_ref[...] = dlogf[0:1, :]
        v = f_ref[...] + b_ref[...]
        lane = lax.broadcasted_iota(jnp.int32, (tm, LANES), 1)
        df = jnp.where(lane < n_heads, dlogf * _sigmoid(-v), 0.0)
        df_ref[...] = df.astype(BF16)
        db_ref[...] += jnp.sum(df, axis=0, keepdims=True)

    return pl.pallas_call(
        body, name="scan_bwd", grid=(nt,),
        in_specs=[pl.BlockSpec((tm, LANES), lambda i: (nt - 1 - i, 0)),
                  pl.BlockSpec((tm, LANES), lambda i: (nt - 1 - i, f_block)),
                  pl.BlockSpec((1, LANES), lambda i: (0, 0))],
        out_specs=[pl.BlockSpec((tm, LANES), lambda i: (nt - 1 - i, 0)), pl.BlockSpec((1, LANES), lambda i: (0, 0))],
        out_shape=[jax.ShapeDtypeStruct((S, LANES), BF16), jax.ShapeDtypeStruct((1, LANES), F32)],
        scratch_shapes=[pltpu.VMEM((1, LANES), F32)], compiler_params=_params(),
    )(dc, hfg, bf_pad)


FOX_AUG = 2 * HEAD_DIM
FOX_V_ROWS = HEAD_DIM + 16
Q_ONES, K_ONES = HEAD_DIM, HEAD_DIM + 3
LOG2E = 1.4426950408889634


def _fox_prep(h, c, H):
    S = h.shape[0]
    W = H * HEAD_DIM
    tm = _rows(S, 3 * W * 2 + LANES * 4 + 2 * H * FOX_AUG * 2 + H * FOX_V_ROWS * 2, 128)
    inv_scale = HEAD_DIM ** 0.5

    def body(q_ref, k_ref, v_ref, c_ref, qa_ref, ka_ref, vt_ref):
        lane = lax.broadcasted_iota(jnp.int32, (tm, HEAD_DIM), 1)
        a = c_ref[...] * inv_scale
        for hh in range(H):
            vt_ref[hh * FOX_V_ROWS:hh * FOX_V_ROWS + HEAD_DIM, :] = (
                v_ref[:, hh * HEAD_DIM:(hh + 1) * HEAD_DIM].astype(F32).T.astype(BF16))
            vt_ref[hh * FOX_V_ROWS + HEAD_DIM:(hh + 1) * FOX_V_ROWS, :] = jnp.ones((FOX_V_ROWS - HEAD_DIM, tm), BF16)
            col = a[:, hh:hh + 1]
            hi = col.astype(BF16).astype(F32)
            mid = (col - hi).astype(BF16).astype(F32)
            lo = col - hi - mid
            piece = jnp.where(lane % 3 == 0, hi, jnp.where(lane % 3 == 1, mid, lo))
            extra_q = jnp.where(lane < 3, 1.0, jnp.where(lane < 6, piece, 0.0))
            extra_k = jnp.where(lane < 3, -piece, jnp.where(lane < 6, 1.0, 0.0))
            qa_ref[:, hh * FOX_AUG:hh * FOX_AUG + HEAD_DIM] = q_ref[:, hh * HEAD_DIM:(hh + 1) * HEAD_DIM]
            qa_ref[:, hh * FOX_AUG + HEAD_DIM:(hh + 1) * FOX_AUG] = extra_q.astype(BF16)
            ka_ref[:, hh * FOX_AUG:hh * FOX_AUG + HEAD_DIM] = k_ref[:, hh * HEAD_DIM:(hh + 1) * HEAD_DIM]
            ka_ref[:, hh * FOX_AUG + HEAD_DIM:(hh + 1) * FOX_AUG] = extra_k.astype(BF16)

    aug = pl.BlockSpec((tm, H * FOX_AUG), lambda i: (i, 0))
    return pl.pallas_call(
        body, name="fox_prep", grid=(S // tm,),
        in_specs=[pl.BlockSpec((tm, W), lambda i: (i, 4)), pl.BlockSpec((tm, W), lambda i: (i, 5)),
                  pl.BlockSpec((tm, W), lambda i: (i, 6)), pl.BlockSpec((tm, LANES), lambda i: (i, 0))],
        out_specs=[aug, aug, pl.BlockSpec((H * FOX_V_ROWS, tm), lambda i: (0, i))],
        out_shape=[jax.ShapeDtypeStruct((S, H * FOX_AUG), BF16)] * 2 + [jax.ShapeDtypeStruct((H * FOX_V_ROWS, S), BF16)],
        compiler_params=_params(),
    )(h, h, h, c)


def _causal_t(T):
    return lax.broadcasted_iota(jnp.int32, (T, T), 0) <= lax.broadcasted_iota(jnp.int32, (T, T), 1)


def _fox_fwd(qa, ka, vt, h, H):
    S = h.shape[0]
    W = H * HEAD_DIM
    T = _tile(S, FOX_TILE, 128)
    nq = S // T
    k1 = HEAD_DIM ** -0.5 * LOG2E
    nt = (((1,), (1,)), ((), ()))

    def body(q_ref, k_ref, vt_ref, z_ref, o_ref, g_ref, lse_ref, m_ref, acc_ref):
        i, j = pl.program_id(1), pl.program_id(2)

        @pl.when(j == 0)
        def _():
            m_ref[...] = jnp.full_like(m_ref, NEG)
            acc_ref[...] = jnp.zeros_like(acc_ref)

        def step(diag):
            raw = lax.dot_general(k_ref[...], q_ref[...], nt, preferred_element_type=F32)
            if diag:
                raw = jnp.where(_causal_t(T), raw, NEG)
            m_new = jnp.maximum(m_ref[...], jnp.max(raw, axis=0, keepdims=True))
            a = jnp.exp2((m_ref[...] - m_new) * k1)
            p = jnp.exp2((raw - m_new) * k1).astype(BF16)
            acc_ref[...] = a * acc_ref[...] + jnp.dot(vt_ref[...], p, preferred_element_type=F32)
            m_ref[...] = m_new

        @pl.when(j < i)
        def _():
            step(False)

        @pl.when(j == i)
        def _():
            step(True)

        @pl.when(j == nq - 1)
        def _():
            acc = acc_ref[...]
            den = acc[HEAD_DIM:HEAD_DIM + 1, :]
            out = (acc[:HEAD_DIM, :] / den).T
            z = z_ref[...].astype(F32)
            o_ref[...] = out.astype(BF16)
            g_ref[...] = (out * z * _sigmoid(z)).astype(BF16)
            lse_ref[...] = m_ref[...] * k1 + jnp.log(den) * LOG2E

    out = pl.BlockSpec((T, HEAD_DIM), lambda hh, i, j: (i, hh))
    return pl.pallas_call(
        body, name="fox_fwd", grid=(H, nq, nq),
        in_specs=[pl.BlockSpec((T, FOX_AUG), lambda hh, i, j: (i, hh)),
                  pl.BlockSpec((T, FOX_AUG), lambda hh, i, j: (jnp.minimum(j, i), hh)),
                  pl.BlockSpec((FOX_V_ROWS, T), lambda hh, i, j: (hh, jnp.minimum(j, i))),
                  pl.BlockSpec((T, HEAD_DIM), lambda hh, i, j: (i, 7 * H + hh))],
        out_specs=[out, out, pl.BlockSpec((None, 1, T), lambda hh, i, j: (hh, 0, i))],
        out_shape=[jax.ShapeDtypeStruct((S, W), BF16), jax.ShapeDtypeStruct((S, W), BF16),
                   jax.ShapeDtypeStruct((H, 1, S), F32)],
        scratch_shapes=[pltpu.VMEM((1, T), F32), pltpu.VMEM((FOX_V_ROWS, T), F32)],
        compiler_params=_params(),
    )(qa, ka, vt, h)


def _fox_bwd(qa, ka, h, do, lse, dl, H):
    S = h.shape[0]
    W = H * HEAD_DIM
    T = _tile(S, FOX_TILE, 128)
    nq = S // T
    k1 = HEAD_DIM ** -0.5 * LOG2E
    nt = (((1,), (1,)), ((), ()))
    tn = (((0,), (0,)), ((), ()))

    def body(q_ref, k_ref, v_ref, do_ref, lse_ref, dl_ref, dq_ref, dk_ref, dv_ref, ak_ref, av_ref):
        j, i = pl.program_id(1), pl.program_id(2)

        @pl.when((j == 0) & (i == 0))
        def _():
            dq_ref[...] = jnp.zeros_like(dq_ref)

        @pl.when(i == 0)
        def _():
            ak_ref[...] = jnp.zeros_like(ak_ref)
            av_ref[...] = jnp.zeros_like(av_ref)

        def step(diag):
            q, k, v, d_o = q_ref[...], k_ref[...], v_ref[...], do_ref[...]
            raw = lax.dot_general(k, q, nt, preferred_element_type=F32)
            if diag:
                raw = jnp.where(_causal_t(T), raw, NEG)
            p = jnp.exp2(raw * k1 - lse_ref[...])
            dp = lax.dot_general(v, d_o, nt, preferred_element_type=F32)
            ds = (p * (dp - dl_ref[...])).astype(BF16)
            av_ref[...] += jnp.dot(p.astype(BF16), d_o, preferred_element_type=F32)
            ak_ref[...] += jnp.dot(ds, q, preferred_element_type=F32)
            rows = pl.ds(pl.multiple_of(i * T, T), T)
            dq_ref[rows, :] += lax.dot_general(ds, k, tn, preferred_element_type=F32)

        @pl.when(i > j)
        def _():
            step(False)

        @pl.when(i == j)
        def _():
            step(True)

        @pl.when(i == nq - 1)
        def _():
            dk_ref[...] = ak_ref[...]
            dv_ref[...] = av_ref[...]

    qrow = lambda hh, j, i: (jnp.maximum(i, j), hh)
    krow = lambda hh, j, i: (j, hh)
    stat = pl.BlockSpec((None, 1, T), lambda hh, j, i: (hh, 0, jnp.maximum(i, j)))
    return pl.pallas_call(
        body, name="fox_bwd", grid=(H, nq, nq),
        in_specs=[pl.BlockSpec((T, FOX_AUG), qrow), pl.BlockSpec((T, FOX_AUG), krow),
                  pl.BlockSpec((T, HEAD_DIM), lambda hh, j, i: (j, 6 * H + hh)),
                  pl.BlockSpec((T, HEAD_DIM), qrow), stat, stat],
        out_specs=[pl.BlockSpec((S, FOX_AUG), lambda hh, j, i: (0, hh)), pl.BlockSpec((T, FOX_AUG), krow),
                   pl.BlockSpec((T, HEAD_DIM), krow)],
        out_shape=[jax.ShapeDtypeStruct((S, H * FOX_AUG), F32), jax.ShapeDtypeStruct((S, H * FOX_AUG), F32),
                   jax.ShapeDtypeStruct((S, W), F32)],
        scratch_shapes=[pltpu.VMEM((T, FOX_AUG), F32), pltpu.VMEM((T, HEAD_DIM), F32)],
        compiler_params=_params(),
    )(qa, ka, h, do, lse, dl)


def _adamw_math(w, g, m, v):
    m = ADAM_B1 * m + (1.0 - ADAM_B1) * g
    v = ADAM_B2 * v + (1.0 - ADAM_B2) * (g * g)
    m_hat = m / (1.0 - ADAM_B1 ** ADAM_STEP)
    v_hat = v / (1.0 - ADAM_B2 ** ADAM_STEP)
    delta = -ADAM_LR * (m_hat / (jnp.sqrt(v_hat) + ADAM_EPS) + ADAM_WD * w)
    return delta, m, v


def _adamw(w, g_mine, g_theirs, m, v, name):
    L, R, C = w.shape
    half = L // 2
    tr = _rows(R, C * 4 * 9, 8)

    def body(c_ref, w_ref, gm_ref, gt_ref, m_ref, v_ref, g_ref, d_ref, nm_ref, nv_ref):
        g = jnp.where(pl.program_id(0) // half == c_ref[0], gm_ref[...], gt_ref[...])
        g_ref[...] = g
        d_ref[...], nm_ref[...], nv_ref[...] = _adamw_math(w_ref[...], g, m_ref[...], v_ref[...])

    blk = pl.BlockSpec((None, tr, C), lambda l, i, c: (l, i, 0))
    mine = pl.BlockSpec((None, tr, C), lambda l, i, c: (jnp.clip(l - c[0] * half, 0, half - 1), i, 0))
    theirs = pl.BlockSpec((None, tr, C), lambda l, i, c: (jnp.clip(l - (1 - c[0]) * half, 0, half - 1), i, 0))
    grid_spec = pltpu.PrefetchScalarGridSpec(
        num_scalar_prefetch=1, grid=(L, R // tr), in_specs=[blk, mine, theirs, blk, blk], out_specs=[blk] * 4)
    core = lax.axis_index("c").astype(jnp.int32).reshape(1)
    return pl.pallas_call(
        body, name=name, grid_spec=grid_spec, out_shape=[jax.ShapeDtypeStruct((L, R, C), F32)] * 4,
        compiler_params=_params(),
    )(core, w, g_mine, g_theirs, m, v)


def _place():
    x, y, c = lax.axis_index("x"), lax.axis_index("y"), lax.axis_index("c")
    return x, y, c, [(1 - x, y), (x, 1 - y), (1 - x, 1 - y)]


def _remote(src, dst, send_sems, recv_sems, k, to):
    return pltpu.make_async_remote_copy(src_ref=src, dst_ref=dst, send_sem=send_sems.at[k], recv_sem=recv_sems.at[k],
                                        device_id=to, device_id_type=MESH)


def _gather_weights(shards):
    n = len(shards)
    half = shards[0].shape[0] // 2

    def body(*refs):
        srcs, dsts = refs[:n], refs[n:2 * n]
        send_sems, recv_sems = refs[2 * n:]
        x, y, c, chips = _place()
        me = 2 * x + y
        mine, theirs = pl.ds(c * half, half), pl.ds((1 - c) * half, half)
        first = [_remote(srcs[a].at[mine], dsts[a].at[me, mine], send_sems, recv_sems, 6 * a + j, (px, py, c))
                 for a in range(n) for j, (px, py) in enumerate(chips)]
        for cp in first:
            cp.start()
        passed = []
        for a in range(n):
            for j, (px, py) in enumerate(chips):
                landed = dsts[a].at[2 * px + py, mine]
                _remote(landed, landed, send_sems, recv_sems, 6 * a + j, (px, py, c)).wait_recv()
                cp = _remote(landed, landed, send_sems, recv_sems, 6 * a + 3 + j, (x, y, 1 - c))
                cp.start()
                passed.append(cp)
        for a in range(n):
            for j, (px, py) in enumerate(chips):
                landed = dsts[a].at[2 * px + py, theirs]
                _remote(landed, landed, send_sems, recv_sems, 6 * a + 3 + j, (x, y, 1 - c)).wait_recv()
        for cp in first + passed:
            cp.wait_send()

    return pl.pallas_call(
        body, name="gather_weights", in_specs=[ANY] * n, out_specs=[ANY] * n,
        out_shape=[jax.ShapeDtypeStruct((4,) + s.shape, s.dtype) for s in shards],
        scratch_shapes=[pltpu.SemaphoreType.DMA((6 * n,)), pltpu.SemaphoreType.DMA((6 * n,))],
    )(*shards)


def _swap_other_half(parts):
    n = len(parts)
    half = parts[0].shape[1] // 2

    def body(*refs):
        srcs, dsts = refs[:n], refs[n:2 * n]
        send_sems, recv_sems = refs[2 * n:]
        x, y, c, _ = _place()
        cps = [_remote(srcs[a].at[:, pl.ds((1 - c) * half, half)], dsts[a], send_sems, recv_sems, a, (x, y, 1 - c))
               for a in range(n)]
        for cp in cps:
            cp.start()
        for cp in cps:
            cp.wait()

    return pl.pallas_call(
        body, name="grad_swap_half", in_specs=[ANY] * n, out_specs=[ANY] * n,
        out_shape=[jax.ShapeDtypeStruct((4, half) + p.shape[2:], p.dtype) for p in parts],
        scratch_shapes=[pltpu.SemaphoreType.DMA((n,)), pltpu.SemaphoreType.DMA((n,))],
    )(*parts)


def _add_half(part, got, name):
    _, half, R, C = got.shape
    tr = _rows(R, C * 2 * 3)

    def body(c_ref, p_ref, g_ref, o_ref):
        o_ref[...] = (p_ref[...].astype(F32) + g_ref[...].astype(F32)).astype(BF16)

    grid_spec = pltpu.PrefetchScalarGridSpec(
        num_scalar_prefetch=1, grid=(4, half, R // tr),
        in_specs=[pl.BlockSpec((None, None, tr, C), lambda s, l, i, c: (s, c[0] * half + l, i, 0)),
                  pl.BlockSpec((None, None, tr, C), lambda s, l, i, c: (s, l, i, 0))],
        out_specs=pl.BlockSpec((None, None, tr, C), lambda s, l, i, c: (s, l, i, 0)))
    core = lax.axis_index("c").astype(jnp.int32).reshape(1)
    return pl.pallas_call(
        body, name=name, grid_spec=grid_spec, out_shape=jax.ShapeDtypeStruct(got.shape, BF16),
        compiler_params=_params(),
    )(core, part, got)


def _scatter_to_owner(parts):
    n = len(parts)

    def body(*refs):
        srcs, dsts = refs[:n], refs[n:2 * n]
        send_sems, recv_sems, local_sems = refs[2 * n:]
        x, y, c, chips = _place()
        me = 2 * x + y
        local = [pltpu.make_async_copy(srcs[a].at[me], dsts[a].at[me], local_sems.at[a]) for a in range(n)]
        for cp in local:
            cp.start()
        sends = [_remote(srcs[a].at[2 * px + py], dsts[a].at[me], send_sems, recv_sems, 3 * a + j, (px, py, c))
                 for a in range(n) for j, (px, py) in enumerate(chips)]
        for cp in sends:
            cp.start()
        for a in range(n):
            for j, (px, py) in enumerate(chips):
                slot = dsts[a].at[2 * px + py]
                _remote(slot, slot, send_sems, recv_sems, 3 * a + j, (px, py, c)).wait_recv()
        for cp in sends:
            cp.wait_send()
        for cp in local:
            cp.wait()

    return pl.pallas_call(
        body, name="grad_scatter", in_specs=[ANY] * n, out_specs=[ANY] * n,
        out_shape=[jax.ShapeDtypeStruct(p.shape, p.dtype) for p in parts],
        scratch_shapes=[pltpu.SemaphoreType.DMA((3 * n,)), pltpu.SemaphoreType.DMA((3 * n,)),
                        pltpu.SemaphoreType.DMA((n,))],
    )(*parts)


def _sum_chips(got, name):
    _, half, R, C = got.shape
    tr = _rows(R, C * (2 * 4 + 4))

    def body(g_ref, o_ref):
        o_ref[...] = ((g_ref[0].astype(F32) + g_ref[1].astype(F32)) + g_ref[2].astype(F32)) + g_ref[3].astype(F32)

    return pl.pallas_call(
        body, name=name, grid=(half, R // tr),
        in_specs=[pl.BlockSpec((4, None, tr, C), lambda l, i: (0, l, i, 0))],
        out_specs=pl.BlockSpec((None, tr, C), lambda l, i: (l, i, 0)),
        out_shape=jax.ShapeDtypeStruct((half, R, C), F32), compiler_params=_params(),
    )(got)


def _share_halves(halves):
    n = len(halves)

    def body(*refs):
        srcs, dsts = refs[:n], refs[n:2 * n]
        send_sems, recv_sems = refs[2 * n:]
        x, y, c, _ = _place()
        cps = [_remote(srcs[a], dsts[a], send_sems, recv_sems, a, (x, y, 1 - c)) for a in range(n)]
        for cp in cps:
            cp.start()
        for cp in cps:
            cp.wait()

    return pl.pallas_call(
        body, name="grad_share_halves", in_specs=[ANY] * n, out_specs=[ANY] * n,
        out_shape=[jax.ShapeDtypeStruct(h.shape, h.dtype) for h in halves],
        scratch_shapes=[pltpu.SemaphoreType.DMA((n,)), pltpu.SemaphoreType.DMA((n,))],
    )(*halves)


def _small_allreduce_adamw(part, w, m, v):
    R = part.shape[0]
    deltas = [(dx, dy, dc) for dx in (0, 1) for dy in (0, 1) for dc in (0, 1)][1:]

    def body(p_ref, w_ref, m_ref, v_ref, g_ref, d_ref, nm_ref, nv_ref, all_ref, send_sems, recv_sems):
        x, y, c, _ = _place()
        me = 4 * x + 2 * y + c
        all_ref[me] = p_ref[...]
        cps = [_remote(p_ref, all_ref.at[me], send_sems, recv_sems, k, (x ^ dx, y ^ dy, c ^ dc))
               for k, (dx, dy, dc) in enumerate(deltas)]
        for cp in cps:
            cp.start()
        for k, (dx, dy, dc) in enumerate(deltas):
            slot = all_ref.at[4 * (x ^ dx) + 2 * (y ^ dy) + (c ^ dc)]
            _remote(slot, slot, send_sems, recv_sems, k, (x ^ dx, y ^ dy, c ^ dc)).wait_recv()
        for cp in cps:
            cp.wait_send()
        g = all_ref[0]
        for k in range(1, 8):
            g = g + all_ref[k]
        g_ref[...] = g
        d_ref[...], nm_ref[...], nv_ref[...] = _adamw_math(w_ref[...], g, m_ref[...], v_ref[...])

    vm = pl.BlockSpec(memory_space=pltpu.VMEM)
    shape = jax.ShapeDtypeStruct((R, LANES), F32)
    return pl.pallas_call(
        body, name="small_allreduce_adamw", in_specs=[vm] * 4, out_specs=[vm] * 4, out_shape=[shape] * 4,
        scratch_shapes=[pltpu.VMEM((8, R, LANES), F32), pltpu.SemaphoreType.DMA((7,)), pltpu.SemaphoreType.DMA((7,))],
    )(part, w, m, v)


def _pack_small(bf, bg, lg, lb, extra=None):
    L, H = bf.shape
    per = jnp.concatenate([jnp.pad(bf, ((0, 0), (0, LANES - H))), bg, lg, lb], axis=1)
    flat = per.reshape(-1, LANES)
    last = jnp.zeros((8 + (-flat.shape[0]) % 8, LANES), F32)
    if extra is not None:
        last = last.at[-8, 0].set(extra)
    return jnp.concatenate([flat, last], axis=0)


def _unpack_small(p, L, H, D):
    per = p[:L * (1 + 4 * D // LANES)].reshape(L, -1)
    return per[:, :H], per[:, LANES:LANES + 2 * D], per[:, LANES + 2 * D:LANES + 3 * D], per[:, LANES + 3 * D:]


def kernel(x, w_in, b_forget, b_gate, w_up_a, w_up_b, w_out, ln_g, ln_b, loss_target, m_w_in, m_b_forget, m_b_gate, m_w_up_a, m_w_up_b, m_w_out, m_ln_g, m_ln_b, v_w_in, v_b_forget, v_b_gate, v_w_up_a, v_w_up_b, v_w_out, v_ln_g, v_ln_b):
    _, S, D = x.shape
    L, _, C4 = w_in.shape
    H = b_forget.shape[1]
    W = w_up_a.shape[1]
    D4 = D // 4
    NC = 4 * C4
    assert W == H * HEAD_DIM and NC == 8 * W + H + 2 * D and L % 2 == 0 and D % LANES == 0
    alpha = float((2 * L) ** 0.25)
    f_block = 2 * D // LANES

    own = [w_in.astype(BF16), jnp.concatenate([w_up_a, w_up_b], axis=2).astype(BF16), w_out.astype(BF16)]
    gathered = _gather_weights(own)
    me = 2 * lax.axis_index("x") + lax.axis_index("y")
    shard = lambda a, s, l: jnp.where(me == s, own[a][l], gathered[a][s, l])
    w_main, w_fg, w_ua, w_ub, w_o = [], [], [], [], []
    for l in range(L):
        full = jnp.concatenate([shard(0, s, l) for s in range(4)], axis=1)
        w_main.append(full[:, :8 * W])
        w_fg.append(jnp.concatenate([full[:, 8 * W + H:], full[:, 8 * W:8 * W + H],
                                     jnp.zeros((D, LANES - H), BF16)], axis=1))
        ups = [shard(1, s, l) for s in range(4)]
        w_ua.append(jnp.concatenate([u[:, :D4] for u in ups], axis=1))
        w_ub.append(jnp.concatenate([u[:, D4:] for u in ups], axis=1))
        w_o.append(jnp.concatenate([shard(2, s, l) for s in range(4)], axis=0))

    pos = jnp.arange(S, dtype=F32)
    inv_freq = ROPE_THETA ** (-jnp.arange(HEAD_DIM // 2, dtype=F32) / (HEAD_DIM // 2))
    ang = pos[:, None] * inv_freq[None, :]
    cos = jnp.concatenate([jnp.cos(ang), jnp.cos(ang)], axis=1)
    sin = jnp.concatenate([-jnp.sin(ang), jnp.sin(ang)], axis=1)
    bf_pad = jnp.pad(b_forget, ((0, 0), (0, LANES - H)))

    xs = x[0]
    xb = xs.astype(BF16)
    saved = []
    for l in range(L):
        h = _matmul(xb, w_main[l], mode="nn", out_dtype=BF16, name="in_proj", rope=(cos, sin), rope_cols=2 * W)
        hfg = _matmul(xb, w_fg[l], mode="nn", out_dtype=F32, name="in_proj_gates")
        qkv_a = h[:, :3 * W]
        views = [qkv_a.reshape(S // d, d * 3 * W) for _, d in DILATED_PATTERNS]
        os, lses = [], []
        for (_, d), hv in zip(DILATED_PATTERNS, views):
            o, lse = _dil_fwd(hv, d, W, f"dil_fwd_d{d}")
            os.append(o.reshape(S, W))
            lses.append(lse.transpose(1, 0, 2).reshape(S, H))
        out_a, lse_a, ga = _dil_combine_fwd(os, lses, h, W)
        c = _scan_fwd(hfg, bf_pad[l:l + 1], f_block)
        qa, ka, vt = _fox_prep(h, c, H)
        out_b, gb, lse_b = _fox_fwd(qa, ka, vt, h, H)
        up_a = _matmul(ga, w_ua[l], mode="nn", out_dtype=BF16, name="up_proj")
        up_b = _matmul(gb, w_ub[l], mode="nn", out_dtype=BF16, name="up_proj")
        u = _merge_fwd(up_a, up_b, hfg, b_gate[l:l + 1])
        r = _matmul(u, w_o[l], mode="nn", out_dtype=F32, name="out_proj", acc_in=xs, acc_scale=alpha)
        saved.append((xb, h, hfg, views, out_a, lse_a, ga, qa, ka, out_b, gb, lse_b, up_a, up_b, u, r))
        xs, xb = _ln_fwd(r, ln_g[l:l + 1], ln_b[l:l + 1])

    dx, sq = _loss(xs, loss_target[0])
    loss_part = 0.5 * jnp.sum(sq) / D

    g_in, g_up, g_out, g_bf, g_bg, g_lg, g_lb = [], [], [], [], [], [], []
    for l in reversed(range(L)):
        xb, h, hfg, views, out_a, lse_a, ga, qa, ka, out_b, gb, lse_b, up_a, up_b, u, r = saved[l]
        dr, adr, dlg, dlb = _ln_bwd(dx, r, ln_g[l:l + 1], alpha)
        du = _matmul(dr, w_o[l], mode="nt", out_dtype=F32, name="out_proj_dx")
        dwo = _matmul(u, dr, mode="tn", out_dtype=F32, name="out_proj_dw")
        dua, dub, dgl, dbg = _merge_bwd(du, up_a, up_b, hfg, b_gate[l:l + 1])
        dga = _matmul(dua, w_ua[l], mode="nt", out_dtype=F32, name="up_proj_dx")
        dgb = _matmul(dub, w_ub[l], mode="nt", out_dtype=F32, name="up_proj_dx")
        dwua = _matmul(ga, dua, mode="tn", out_dtype=F32, name="up_proj_dw")
        dwub = _matmul(gb, dub, mode="tn", out_dtype=F32, name="up_proj_dw")
        do_a, dz_a, dl_a = _gate_bwd(dga, out_a, h, 3, True, "gate_bwd_a")
        dqs, dks, dvs = [], [], []
        for (_, d), hv in zip(DILATED_PATTERNS, views):
            stat = lambda t: t.reshape(S // d, d, H).transpose(1, 0, 2)
            dq, dk, dv = _dil_bwd(hv, do_a.reshape(S // d, d * W), stat(lse_a), stat(dl_a), d, W, f"dil_bwd_d{d}")
            dqs.append(dq.reshape(S, W))
            dks.append(dk.reshape(S, W))
            dvs.append(dv.reshape(S, W))
        dqkv_a = _dil_combine_bwd(dqs, dks, dvs, cos, sin)
        do_b, dz_b, dl_b = _gate_bwd(dgb, out_b, h, 7, True, "gate_bwd_b")
        dq_f, dk_f, dv_b = _fox_bwd(qa, ka, h, do_b, lse_b, dl_b.T.reshape(H, 1, S), H)
        dq_f, dk_f = dq_f.reshape(S, H, FOX_AUG), dk_f.reshape(S, H, FOX_AUG)
        dc = jnp.pad(dq_f[:, :, K_ONES] - dk_f[:, :, Q_ONES], ((0, 0), (0, LANES - H)))
        df, dbf = _scan_bwd(dc, hfg, bf_pad[l:l + 1], f_block, H)
        att_scale = HEAD_DIM ** -0.5
        dq_b = (dq_f[:, :, :HEAD_DIM] * att_scale).astype(BF16).reshape(S, W)
        dk_b = (dk_f[:, :, :HEAD_DIM] * att_scale).astype(BF16).reshape(S, W)
        dh = jnp.concatenate([dqkv_a, dz_a, dq_b, dk_b, dv_b.astype(BF16), dz_b], axis=1)
        dhfg = jnp.concatenate([dgl, df], axis=1)
        dx1 = _matmul(dh, w_main[l], mode="nt", out_dtype=F32, name="in_proj_dx", acc_in=adr)
        dx = _matmul(dhfg, w_fg[l], mode="nt", out_dtype=F32, name="in_proj_gates_dx", acc_in=dx1)
        dwm = _matmul(xb, dh, mode="tn", out_dtype=F32, name="in_proj_dw")
        dwfg = _matmul(xb, dhfg, mode="tn", out_dtype=F32, name="in_proj_gates_dw")
        full = jnp.concatenate([dwm, dwfg[:, 2 * D:2 * D + H], dwfg[:, :2 * D]], axis=1)
        g_in.append(full.reshape(D, 4, C4).transpose(1, 0, 2).astype(BF16))
        g_up.append(jnp.concatenate([dwua.reshape(W, 4, D4), dwub.reshape(W, 4, D4)], axis=2).transpose(1, 0, 2).astype(BF16))
        g_out.append(dwo.reshape(4, D4, D).astype(BF16))
        g_bf.append(dbf[0, :H])
        g_bg.append(dbg[0])
        g_lg.append(dlg[0])
        g_lb.append(dlb[0])
    grad_x = dx[None]
    for lst in (g_in, g_up, g_out, g_bf, g_bg, g_lg, g_lb):
        lst.reverse()

    parts = [jnp.stack(g_in, axis=1), jnp.stack(g_up, axis=1), jnp.stack(g_out, axis=1)]
    names = ["w_in", "w_up", "w_out"]
    got = _swap_other_half(parts)
    chip = [_add_half(p, g, f"grad_add_half_{n}") for p, g, n in zip(parts, got, names)]
    landed = _scatter_to_owner(chip)
    halves = [_sum_chips(g, f"grad_sum_chips_{n}") for g, n in zip(landed, names)]
    theirs = _share_halves(halves)

    pair = lambda a, b: jnp.concatenate([a, b], axis=2)
    grad_w_in, d_in, nm_in, nv_in = _adamw(w_in, halves[0], theirs[0], m_w_in, v_w_in, "adamw_w_in")
    up = _adamw(pair(w_up_a, w_up_b), halves[1], theirs[1], pair(m_w_up_a, m_w_up_b), pair(v_w_up_a, v_w_up_b),
                "adamw_w_up")
    (grad_w_up_a, grad_w_up_b), (d_ua, d_ub), (nm_ua, nm_ub), (nv_ua, nv_ub) = [
        (t[:, :, :D4], t[:, :, D4:]) for t in up]
    grad_w_out, d_o, nm_o, nv_o = _adamw(w_out, halves[2], theirs[2], m_w_out, v_w_out, "adamw_w_out")

    small_g = _pack_small(jnp.stack(g_bf), jnp.stack(g_bg), jnp.stack(g_lg), jnp.stack(g_lb), loss_part)
    small = _small_allreduce_adamw(small_g, _pack_small(b_forget, b_gate, ln_g, ln_b),
                                   _pack_small(m_b_forget, m_b_gate, m_ln_g, m_ln_b),
                                   _pack_small(v_b_forget, v_b_gate, v_ln_g, v_ln_b))
    loss = small[0][-8, 0]
    (g_bf, g_bg, g_lg, g_lb), (d_bf, d_bg, d_lg, d_lb), (nm_bf, nm_bg, nm_lg, nm_lb), (nv_bf, nv_bg, nv_lg, nv_lb) = [
        _unpack_small(p, L, H, D) for p in small]

    return (loss, grad_x,
            grad_w_in, g_bf, g_bg, grad_w_up_a, grad_w_up_b, grad_w_out, g_lg, g_lb,
            d_in, d_bf, d_bg, d_ua, d_ub, d_o, d_lg, d_lb,
            nm_in, nm_bf, nm_bg, nm_ua, nm_ub, nm_o, nm_lg, nm_lb,
            nv_in, nv_bf, nv_bg, nv_ua, nv_ub, nv_o, nv_lg, nv_lb)
```

```python
import functools

import jax
import jax.numpy as jnp
from jax import lax
from jax.experimental import pallas as pl
from jax.experimental.pallas import tpu as pltpu

F32 = jnp.float32
BF16 = jnp.bfloat16
MESH = pl.DeviceIdType.MESH
ANY = pl.BlockSpec(memory_space=pl.ANY)

HEAD_DIM = 128
LANES = 128
Q_BLOCK = 128
DILATED_PATTERNS = ((128, 1), (512, 4), (2048, 16))
ROPE_THETA = 10000.0
LN_EPS = 1e-5
ADAM_LR, ADAM_B1, ADAM_B2, ADAM_EPS, ADAM_WD, ADAM_STEP = 0.001, 0.9, 0.999, 1e-08, 0.01, 10
NEG = -1e30
VMEM_LIMIT = 56 * 2**20
ELEMENTWISE_BUDGET = 20 * 2**20
FOX_TILE = 512
MM_TILES = (1024, 1024, 2048)


def _params():
    return pltpu.CompilerParams(vmem_limit_bytes=VMEM_LIMIT)


def _tile(dim, target, align):
    if dim <= target:
        return dim
    t = (target // align) * align
    while t >= align:
        if dim % t == 0:
            return t
        t -= align
    return dim


def _rows(n_rows, bytes_per_row, align=16):
    return _tile(n_rows, max(align, ELEMENTWISE_BUDGET // (2 * bytes_per_row)), align)


def _sigmoid(v):
    return 1.0 / (1.0 + jnp.exp(-v))


def _matmul(a, b, *, mode, out_dtype, name, acc_in=None, acc_scale=1.0, rope=None, rope_cols=0):
    if mode == "nn":
        (M, K), (K2, N) = a.shape, b.shape
    elif mode == "nt":
        (M, K), (N, K2) = a.shape, b.shape
    else:
        (K, M), (K2, N) = a.shape, b.shape
    assert K == K2, (a.shape, b.shape, mode)
    tm, tn, tk = _tile(M, MM_TILES[0], 128), _tile(N, MM_TILES[1], 128), _tile(K, MM_TILES[2], 128)
    if rope is not None:
        assert rope_cols % tn == 0
    nk = K // tk
    n_rope_tiles = rope_cols // tn if rope is not None else 0
    dims = {"nn": (((1,), (0,)), ((), ())), "nt": (((1,), (1,)), ((), ())), "tn": (((0,), (0,)), ((), ()))}[mode]

    def body(*refs):
        a_ref, b_ref = refs[0], refs[1]
        pos = 2
        if rope is not None:
            cos_ref, sin_ref = refs[pos], refs[pos + 1]
            pos += 2
        if acc_in is not None:
            acc_in_ref = refs[pos]
            pos += 1
        o_ref, acc_ref = refs[pos], refs[pos + 1]
        j, k = pl.program_id(1), pl.program_id(2)

        @pl.when(k == 0)
        def _():
            acc_ref[...] = jnp.zeros_like(acc_ref)

        acc_ref[...] += lax.dot_general(a_ref[...], b_ref[...], dims, preferred_element_type=F32)

        def finish(rotate):
            r = acc_ref[...]
            if acc_in is not None:
                r = r + acc_scale * acc_in_ref[...]
            if rotate:
                cos, sin = cos_ref[...], sin_ref[...]
                for g in range(tn // HEAD_DIM):
                    sl = slice(g * HEAD_DIM, (g + 1) * HEAD_DIM)
                    t = r[:, sl]
                    o_ref[:, sl] = (t * cos + pltpu.roll(t, HEAD_DIM // 2, 1) * sin).astype(o_ref.dtype)
            else:
                o_ref[...] = r.astype(o_ref.dtype)

        if n_rope_tiles:
            @pl.when((k == nk - 1) & (j < n_rope_tiles))
            def _():
                finish(True)

            @pl.when((k == nk - 1) & (j >= n_rope_tiles))
            def _():
                finish(False)
        else:
            @pl.when(k == nk - 1)
            def _():
                finish(False)

    if mode == "nn":
        in_specs = [pl.BlockSpec((tm, tk), lambda i, j, k: (i, k)), pl.BlockSpec((tk, tn), lambda i, j, k: (k, j))]
    elif mode == "nt":
        in_specs = [pl.BlockSpec((tm, tk), lambda i, j, k: (i, k)), pl.BlockSpec((tn, tk), lambda i, j, k: (j, k))]
    else:
        in_specs = [pl.BlockSpec((tk, tm), lambda i, j, k: (k, i)), pl.BlockSpec((tk, tn), lambda i, j, k: (k, j))]
    args = [a, b]
    if rope is not None:
        in_specs += [pl.BlockSpec((tm, HEAD_DIM), lambda i, j, k: (i, 0))] * 2
        args += list(rope)
    if acc_in is not None:
        in_specs.append(pl.BlockSpec((tm, tn), lambda i, j, k: (i, j)))
        args.append(acc_in)
    return pl.pallas_call(
        body, name=name, grid=(M // tm, N // tn, nk), in_specs=in_specs,
        out_specs=pl.BlockSpec((tm, tn), lambda i, j, k: (i, j)),
        out_shape=jax.ShapeDtypeStruct((M, N), out_dtype),
        scratch_shapes=[pltpu.VMEM((tm, tn), F32)], compiler_params=_params(),
    )(*args)


def _ln_fwd(r, g, b):
    S, D = r.shape
    tm = _rows(S, D * (4 + 4 + 2))

    def body(r_ref, g_ref, b_ref, x_ref, xb_ref):
        v = r_ref[...]
        mu = jnp.mean(v, axis=1, keepdims=True)
        cen = v - mu
        var = jnp.mean(cen * cen, axis=1, keepdims=True)
        out = cen * lax.rsqrt(var + LN_EPS) * g_ref[...] + b_ref[...]
        x_ref[...] = out
        xb_ref[...] = out.astype(BF16)

    row = pl.BlockSpec((tm, D), lambda i: (i, 0))
    vec = pl.BlockSpec((1, D), lambda i: (0, 0))
    return pl.pallas_call(
        body, name="ln_fwd", grid=(S // tm,), in_specs=[row, vec, vec], out_specs=[row, row],
        out_shape=[jax.ShapeDtypeStruct((S, D), F32), jax.ShapeDtypeStruct((S, D), BF16)],
        compiler_params=_params(),
    )(r, g, b)


def _ln_bwd(dx, r, g, alpha):
    S, D = r.shape
    tm = _rows(S, D * (4 + 4 + 2 + 4))

    def body(dx_ref, r_ref, g_ref, drb_ref, adr_ref, dg_ref, db_ref):
        @pl.when(pl.program_id(0) == 0)
        def _():
            dg_ref[...] = jnp.zeros_like(dg_ref)
            db_ref[...] = jnp.zeros_like(db_ref)

        v, d = r_ref[...], dx_ref[...]
        mu = jnp.mean(v, axis=1, keepdims=True)
        cen = v - mu
        var = jnp.mean(cen * cen, axis=1, keepdims=True)
        rstd = lax.rsqrt(var + LN_EPS)
        xhat = cen * rstd
        dxhat = d * g_ref[...]
        dr = rstd * (dxhat - jnp.mean(dxhat, axis=1, keepdims=True)
                     - xhat * jnp.mean(dxhat * xhat, axis=1, keepdims=True))
        drb_ref[...] = dr.astype(BF16)
        adr_ref[...] = alpha * dr
        dg_ref[...] += jnp.sum(d * xhat, axis=0, keepdims=True)
        db_ref[...] += jnp.sum(d, axis=0, keepdims=True)

    row = pl.BlockSpec((tm, D), lambda i: (i, 0))
    vec = pl.BlockSpec((1, D), lambda i: (0, 0))
    return pl.pallas_call(
        body, name="ln_bwd", grid=(S // tm,), in_specs=[row, row, vec], out_specs=[row, row, vec, vec],
        out_shape=[jax.ShapeDtypeStruct((S, D), BF16), jax.ShapeDtypeStruct((S, D), F32),
                   jax.ShapeDtypeStruct((1, D), F32), jax.ShapeDtypeStruct((1, D), F32)],
        compiler_params=_params(),
    )(dx, r, g)


def _loss(y, target):
    S, D = y.shape
    tm = _rows(S, D * 12)

    def body(y_ref, t_ref, dy_ref, sq_ref):
        @pl.when(pl.program_id(0) == 0)
        def _():
            sq_ref[...] = jnp.zeros_like(sq_ref)

        err = y_ref[...] - t_ref[...]
        dy_ref[...] = err * (1.0 / D)
        sq_ref[...] += jnp.sum(err * err, axis=0, keepdims=True)

    row = pl.BlockSpec((tm, D), lambda i: (i, 0))
    vec = pl.BlockSpec((1, D), lambda i: (0, 0))
    return pl.pallas_call(
        body, name="loss", grid=(S // tm,), in_specs=[row, row], out_specs=[row, vec],
        out_shape=[jax.ShapeDtypeStruct((S, D), F32), jax.ShapeDtypeStruct((1, D), F32)],
        compiler_params=_params(),
    )(y, target)


def _merge_fwd(up_a, up_b, hfg, b_gate):
    S, D = up_a.shape
    tm = _rows(S, D * (2 + 2 + 4 + 4 + 2))

    def body(ua_ref, ub_ref, gla_ref, glb_ref, bga_ref, bgb_ref, u_ref):
        ga = _sigmoid(gla_ref[...] + bga_ref[...])
        gb = _sigmoid(glb_ref[...] + bgb_ref[...])
        u_ref[...] = (ga * ua_ref[...].astype(F32) + gb * ub_ref[...].astype(F32)).astype(BF16)

    row = pl.BlockSpec((tm, D), lambda i: (i, 0))
    row1 = pl.BlockSpec((tm, D), lambda i: (i, 1))
    v0 = pl.BlockSpec((1, D), lambda i: (0, 0))
    v1 = pl.BlockSpec((1, D), lambda i: (0, 1))
    return pl.pallas_call(
        body, name="merge_fwd", grid=(S // tm,), in_specs=[row, row, row, row1, v0, v1], out_specs=row,
        out_shape=jax.ShapeDtypeStruct((S, D), BF16), compiler_params=_params(),
    )(up_a, up_b, hfg, hfg, b_gate, b_gate)


def _merge_bwd(du, up_a, up_b, hfg, b_gate):
    S, D = up_a.shape
    tm = _rows(S, D * (4 + 2 + 2 + 4 + 4 + 2 + 2 + 4))

    def body(du_ref, ua_ref, ub_ref, gla_ref, glb_ref, bga_ref, bgb_ref, dua_ref, dub_ref, dgl_ref, dbg_ref):
        @pl.when(pl.program_id(0) == 0)
        def _():
            dbg_ref[...] = jnp.zeros_like(dbg_ref)

        du = du_ref[...]
        ga = _sigmoid(gla_ref[...] + bga_ref[...])
        gb = _sigmoid(glb_ref[...] + bgb_ref[...])
        dua_ref[...] = (du * ga).astype(BF16)
        dub_ref[...] = (du * gb).astype(BF16)
        dgla = du * ua_ref[...].astype(F32) * ga * (1.0 - ga)
        dglb = du * ub_ref[...].astype(F32) * gb * (1.0 - gb)
        dgl_ref[:, :D] = dgla.astype(BF16)
        dgl_ref[:, D:] = dglb.astype(BF16)
        dbg_ref[:, :D] += jnp.sum(dgla, axis=0, keepdims=True)
        dbg_ref[:, D:] += jnp.sum(dglb, axis=0, keepdims=True)

    row = pl.BlockSpec((tm, D), lambda i: (i, 0))
    row1 = pl.BlockSpec((tm, D), lambda i: (i, 1))
    v0 = pl.BlockSpec((1, D), lambda i: (0, 0))
    v1 = pl.BlockSpec((1, D), lambda i: (0, 1))
    return pl.pallas_call(
        body, name="merge_bwd", grid=(S // tm,), in_specs=[row, row, row, row, row1, v0, v1],
        out_specs=[row, row, pl.BlockSpec((tm, 2 * D), lambda i: (i, 0)), pl.BlockSpec((1, 2 * D), lambda i: (0, 0))],
        out_shape=[jax.ShapeDtypeStruct((S, D), BF16), jax.ShapeDtypeStruct((S, D), BF16),
                   jax.ShapeDtypeStruct((S, 2 * D), BF16), jax.ShapeDtypeStruct((1, 2 * D), F32)],
        compiler_params=_params(),
    )(du, up_a, up_b, hfg, hfg, b_gate, b_gate)


def _gate_bwd(dg, out, h, z_block, name):
    S, W = out.shape
    H = W // HEAD_DIM
    tm = _rows(S, W * (4 + 2 + 2 + 2 + 2 + 4))

    def body(dg_ref, o_ref, z_ref, do_ref, dz_ref, dl_ref):
        z = z_ref[...].astype(F32)
        o = o_ref[...].astype(F32)
        d = dg_ref[...]
        sg = _sigmoid(z)
        dout = d * z * sg
        do_ref[...] = dout.astype(BF16)
        dz_ref[...] = (d * o * sg * (1.0 + z * (1.0 - sg))).astype(BF16)
        prod = dout * o
        for hh in range(H):
            sl = slice(hh * HEAD_DIM, (hh + 1) * HEAD_DIM)
            dl_ref[:, hh:hh + 1] = jnp.sum(prod[:, sl], axis=1, keepdims=True)

    row = pl.BlockSpec((tm, W), lambda i: (i, 0))
    return pl.pallas_call(
        body, name=name, grid=(S // tm,),
        in_specs=[row, row, pl.BlockSpec((tm, W), lambda i: (i, z_block))],
        out_specs=[row, row, pl.BlockSpec((tm, H), lambda i: (i, 0))],
        out_shape=[jax.ShapeDtypeStruct((S, W), BF16), jax.ShapeDtypeStruct((S, W), BF16),
                   jax.ShapeDtypeStruct((S, H), F32)],
        compiler_params=_params(),
    )(dg, out, h)


def _dil_masks(n):
    i = lax.broadcasted_iota(jnp.int32, (Q_BLOCK, Q_BLOCK), 0)
    j = lax.broadcasted_iota(jnp.int32, (Q_BLOCK, Q_BLOCK), 1)
    return j <= i, (j >= i) & (n > 0)


def _dil_fwd(hv, d, W, name):
    L = hv.shape[0]
    H = W // HEAD_DIM
    nblk = L // Q_BLOCK
    scale = HEAD_DIM ** -0.5
    nt = (((1,), (1,)), ((), ()))

    def body(q_ref, kp_ref, kc_ref, vp_ref, vc_ref, o_ref, lse_ref):
        mc, mp = _dil_masks(pl.program_id(1))
        for hh in range(H):
            sl = slice(hh * HEAD_DIM, (hh + 1) * HEAD_DIM)
            q = q_ref[:, sl]
            sc = jnp.where(mc, lax.dot_general(q, kc_ref[:, sl], nt, preferred_element_type=F32) * scale, NEG)
            sp = jnp.where(mp, lax.dot_general(q, kp_ref[:, sl], nt, preferred_element_type=F32) * scale, NEG)
            m = jnp.maximum(jnp.max(sc, axis=1, keepdims=True), jnp.max(sp, axis=1, keepdims=True))
            pc, pp = jnp.exp(sc - m), jnp.exp(sp - m)
            den = jnp.sum(pc, axis=1, keepdims=True) + jnp.sum(pp, axis=1, keepdims=True)
            acc = (jnp.dot(pc.astype(BF16), vc_ref[:, sl], preferred_element_type=F32)
                   + jnp.dot(pp.astype(BF16), vp_ref[:, sl], preferred_element_type=F32))
            o_ref[:, sl] = (acc / den).astype(BF16)
            lse_ref[:, hh:hh + 1] = m + jnp.log(den)

    def spec(col, prev):
        if prev:
            return pl.BlockSpec((Q_BLOCK, W), lambda r, n: (jnp.maximum(n - 1, 0), r * 3 + col))
        return pl.BlockSpec((Q_BLOCK, W), lambda r, n: (n, r * 3 + col))

    return pl.pallas_call(
        body, name=name, grid=(d, nblk),
        in_specs=[spec(0, False), spec(1, True), spec(1, False), spec(2, True), spec(2, False)],
        out_specs=[pl.BlockSpec((Q_BLOCK, W), lambda r, n: (n, r)),
                   pl.BlockSpec((None, Q_BLOCK, H), lambda r, n: (r, n, 0))],
        out_shape=[jax.ShapeDtypeStruct((L, d * W), BF16), jax.ShapeDtypeStruct((d, L, H), F32)],
        compiler_params=_params(),
    )(hv, hv, hv, hv, hv)


def _dil_combine_fwd(os, lses, h, W):
    S = h.shape[0]
    H = W // HEAD_DIM
    tm = _rows(S, W * (3 * 2 + 2 + 2 + 2) + 4 * H * 4)

    def body(o1, o2, o3, l1, l2, l3, z_ref, out_ref, lse_ref, g_ref):
        a, b, c = l1[...], l2[...], l3[...]
        m = jnp.maximum(jnp.maximum(a, b), c)
        ea, eb, ec = jnp.exp(a - m), jnp.exp(b - m), jnp.exp(c - m)
        den = ea + eb + ec
        wa, wb, wc = ea / den, eb / den, ec / den
        lse_ref[...] = m + jnp.log(den)
        for hh in range(H):
            sl = slice(hh * HEAD_DIM, (hh + 1) * HEAD_DIM)
            out = (wa[:, hh:hh + 1] * o1[:, sl].astype(F32) + wb[:, hh:hh + 1] * o2[:, sl].astype(F32)
                   + wc[:, hh:hh + 1] * o3[:, sl].astype(F32))
            z = z_ref[:, sl].astype(F32)
            out_ref[:, sl] = out.astype(BF16)
            g_ref[:, sl] = (out * z * _sigmoid(z)).astype(BF16)

    row = pl.BlockSpec((tm, W), lambda i: (i, 0))
    stat = pl.BlockSpec((tm, H), lambda i: (i, 0))
    return pl.pallas_call(
        body, name="dil_combine_fwd", grid=(S // tm,),
        in_specs=[row] * 3 + [stat] * 3 + [pl.BlockSpec((tm, W), lambda i: (i, 3))], out_specs=[row, stat, row],
        out_shape=[jax.ShapeDtypeStruct((S, W), BF16), jax.ShapeDtypeStruct((S, H), F32),
                   jax.ShapeDtypeStruct((S, W), BF16)],
        compiler_params=_params(),
    )(*os, *lses, h)


def _dil_bwd(hv, dov, lsev, dlv, d, W, name):
    L = hv.shape[0]
    H = W // HEAD_DIM
    nblk = L // Q_BLOCK
    scale = HEAD_DIM ** -0.5
    nt = (((1,), (1,)), ((), ()))
    tn = (((0,), (0,)), ((), ()))

    def body(q_ref, kp_ref, kc_ref, vp_ref, vc_ref, do_ref, lse_ref, dl_ref, dq_ref, dk_ref, dv_ref, ck_ref, cv_ref):
        n = pl.program_id(1)

        @pl.when(n == 0)
        def _():
            ck_ref[...] = jnp.zeros_like(ck_ref)
            cv_ref[...] = jnp.zeros_like(cv_ref)

        @pl.when(n < nblk)
        def _():
            mc, mp = _dil_masks(n)
            for hh in range(H):
                sl = slice(hh * HEAD_DIM, (hh + 1) * HEAD_DIM)
                q, do = q_ref[:, sl], do_ref[:, sl]
                kc, kp, vc, vp = kc_ref[:, sl], kp_ref[:, sl], vc_ref[:, sl], vp_ref[:, sl]
                lse, dl = lse_ref[:, hh:hh + 1], dl_ref[:, hh:hh + 1]
                sc = jnp.where(mc, lax.dot_general(q, kc, nt, preferred_element_type=F32) * scale, NEG)
                sp = jnp.where(mp, lax.dot_general(q, kp, nt, preferred_element_type=F32) * scale, NEG)
                pc, pp = jnp.exp(sc - lse), jnp.exp(sp - lse)
                dsc = pc * (lax.dot_general(do, vc, nt, preferred_element_type=F32) - dl) * scale
                dsp = pp * (lax.dot_general(do, vp, nt, preferred_element_type=F32) - dl) * scale
                dsc_b, dsp_b = dsc.astype(BF16), dsp.astype(BF16)
                dq_ref[:, sl] = (jnp.dot(dsc_b, kc, preferred_element_type=F32)
                                 + jnp.dot(dsp_b, kp, preferred_element_type=F32)).astype(BF16)
                dk_ref[:, sl] = (ck_ref[:, sl] + lax.dot_general(dsp_b, q, tn, preferred_element_type=F32)).astype(BF16)
                dv_ref[:, sl] = (cv_ref[:, sl]
                                 + lax.dot_general(pp.astype(BF16), do, tn, preferred_element_type=F32)).astype(BF16)
                ck_ref[:, sl] = lax.dot_general(dsc_b, q, tn, preferred_element_type=F32)
                cv_ref[:, sl] = lax.dot_general(pc.astype(BF16), do, tn, preferred_element_type=F32)

        @pl.when(n == nblk)
        def _():
            dk_ref[...] = ck_ref[...].astype(BF16)
            dv_ref[...] = cv_ref[...].astype(BF16)

    last = nblk - 1

    def hspec(col, prev):
        if prev:
            return pl.BlockSpec((Q_BLOCK, W), lambda r, n: (jnp.clip(n - 1, 0, last), r * 3 + col))
        return pl.BlockSpec((Q_BLOCK, W), lambda r, n: (jnp.minimum(n, last), r * 3 + col))

    cur = pl.BlockSpec((Q_BLOCK, W), lambda r, n: (jnp.minimum(n, last), r))
    lag = pl.BlockSpec((Q_BLOCK, W), lambda r, n: (jnp.maximum(n - 1, 0), r))
    stat = pl.BlockSpec((None, Q_BLOCK, H), lambda r, n: (r, jnp.minimum(n, last), 0))
    shape = jax.ShapeDtypeStruct((L, d * W), BF16)
    return pl.pallas_call(
        body, name=name, grid=(d, nblk + 1),
        in_specs=[hspec(0, False), hspec(1, True), hspec(1, False), hspec(2, True), hspec(2, False), cur, stat, stat],
        out_specs=[cur, lag, lag], out_shape=[shape, shape, shape],
        scratch_shapes=[pltpu.VMEM((Q_BLOCK, W), F32), pltpu.VMEM((Q_BLOCK, W), F32)],
        compiler_params=_params(),
    )(hv, hv, hv, hv, hv, dov, lsev, dlv)


def _dil_combine_bwd(dqs, dks, dvs, cos, sin):
    S, W = dqs[0].shape
    tm = _rows(S, W * (9 * 2 + 3 * 2 + 3 * 4))

    def body(q1, q2, q3, k1, k2, k3, v1, v2, v3, cos_ref, sin_ref, o_ref):
        cos_t, sin_t = cos_ref[...], -sin_ref[...]
        add3 = lambda a, b, c: a[...].astype(F32) + b[...].astype(F32) + c[...].astype(F32)
        dq = add3(q1, q2, q3)
        dk = add3(k1, k2, k3)
        for hh in range(W // HEAD_DIM):
            sl = slice(hh * HEAD_DIM, (hh + 1) * HEAD_DIM)
            tq, tk = dq[:, sl], dk[:, sl]
            o_ref[:, hh * HEAD_DIM:(hh + 1) * HEAD_DIM] = (
                tq * cos_t + pltpu.roll(tq, HEAD_DIM // 2, 1) * sin_t).astype(BF16)
            o_ref[:, W + hh * HEAD_DIM:W + (hh + 1) * HEAD_DIM] = (
                tk * cos_t + pltpu.roll(tk, HEAD_DIM // 2, 1) * sin_t).astype(BF16)
        o_ref[:, 2 * W:] = add3(v1, v2, v3).astype(BF16)

    row = pl.BlockSpec((tm, W), lambda i: (i, 0))
    tab = pl.BlockSpec((tm, HEAD_DIM), lambda i: (i, 0))
    return pl.pallas_call(
        body, name="dil_combine_bwd", grid=(S // tm,), in_specs=[row] * 9 + [tab, tab],
        out_specs=pl.BlockSpec((tm, 3 * W), lambda i: (i, 0)),
        out_shape=jax.ShapeDtypeStruct((S, 3 * W), BF16), compiler_params=_params(),
    )(*dqs, *dks, *dvs, cos, sin)


def _scan_tile(S):
    return _tile(S, 256, 8)


def _scan_fwd(hfg, bf_pad, f_block):
    S = hfg.shape[0]
    tm = _scan_tile(S)

    def body(f_ref, b_ref, c_ref, carry_ref):
        @pl.when(pl.program_id(0) == 0)
        def _():
            carry_ref[...] = jnp.zeros_like(carry_ref)

        v = f_ref[...] + b_ref[...]
        logf = jnp.minimum(v, 0.0) - jnp.log(1.0 + jnp.exp(-jnp.abs(v)))
        tri = (lax.broadcasted_iota(jnp.int32, (tm, tm), 1) <= lax.broadcasted_iota(jnp.int32, (tm, tm), 0)).astype(F32)
        c = jnp.dot(tri, logf, preferred_element_type=F32, precision=lax.Precision.HIGHEST) + carry_ref[...]
        c_ref[...] = c
        carry_ref[...] = c[tm - 1:tm, :]

    return pl.pallas_call(
        body, name="scan_fwd", grid=(S // tm,),
        in_specs=[pl.BlockSpec((tm, LANES), lambda i: (i, f_block)), pl.BlockSpec((1, LANES), lambda i: (0, 0))],
        out_specs=pl.BlockSpec((tm, LANES), lambda i: (i, 0)),
        out_shape=jax.ShapeDtypeStruct((S, LANES), F32),
        scratch_shapes=[pltpu.VMEM((1, LANES), F32)], compiler_params=_params(),
    )(hfg, bf_pad)


def _scan_bwd(dc, hfg, bf_pad, f_block, n_heads):
    S = hfg.shape[0]
    tm = _scan_tile(S)
    nt = S // tm

    def body(dc_ref, f_ref, b_ref, df_ref, db_ref, carry_ref):
        @pl.when(pl.program_id(0) == 0)
        def _():
            carry_ref[...] = jnp.zeros_like(carry_ref)
            db_ref[...] = jnp.zeros_like(db_ref)

        tri = (lax.broadcasted_iota(jnp.int32, (tm, tm), 1) >= lax.broadcasted_iota(jnp.int32, (tm, tm), 0)).astype(F32)
        dlogf = jnp.dot(tri, dc_ref[...], preferred_element_type=F32, precision=lax.Precision.HIGHEST) + carry_ref[...]
        carry_ref[...] = dlogf[0:1, :]
        v = f_ref[...] + b_ref[...]
        lane = lax.broadcasted_iota(jnp.int32, (tm, LANES), 1)
        df = jnp.where(lane < n_heads, dlogf * _sigmoid(-v), 0.0)
        df_ref[...] = df.astype(BF16)
        db_ref[...] += jnp.sum(df, axis=0, keepdims=True)

    return pl.pallas_call(
        body, name="scan_bwd", grid=(nt,),
        in_specs=[pl.BlockSpec((tm, LANES), lambda i: (nt - 1 - i, 0)),
                  pl.BlockSpec((tm, LANES), lambda i: (nt - 1 - i, f_block)),
                  pl.BlockSpec((1, LANES), lambda i: (0, 0))],
        out_specs=[pl.BlockSpec((tm, LANES), lambda i: (nt - 1 - i, 0)), pl.BlockSpec((1, LANES), lambda i: (0, 0))],
        out_shape=[jax.ShapeDtypeStruct((S, LANES), BF16), jax.ShapeDtypeStruct((1, LANES), F32)],
        scratch_shapes=[pltpu.VMEM((1, LANES), F32)], compiler_params=_params(),
    )(dc, hfg, bf_pad)


FOX_AUG = 2 * HEAD_DIM
FOX_V_ROWS = HEAD_DIM + 16
Q_ONES, K_ONES = HEAD_DIM, HEAD_DIM + 3
LOG2E = 1.4426950408889634


def _fox_prep(h, c, H):
    S = h.shape[0]
    W = H * HEAD_DIM
    tm = _rows(S, 3 * W * 2 + LANES * 4 + 2 * H * FOX_AUG * 2 + H * FOX_V_ROWS * 2, 128)
    inv_scale = HEAD_DIM ** 0.5

    def body(q_ref, k_ref, v_ref, c_ref, qa_ref, ka_ref, vt_ref):
        lane = lax.broadcasted_iota(jnp.int32, (tm, HEAD_DIM), 1)
        a = c_ref[...] * inv_scale
        for hh in range(H):
            vt_ref[hh * FOX_V_ROWS:hh * FOX_V_ROWS + HEAD_DIM, :] = (
                v_ref[:, hh * HEAD_DIM:(hh + 1) * HEAD_DIM].astype(F32).T.astype(BF16))
            vt_ref[hh * FOX_V_ROWS + HEAD_DIM:(hh + 1) * FOX_V_ROWS, :] = jnp.ones((FOX_V_ROWS - HEAD_DIM, tm), BF16)
            col = a[:, hh:hh + 1]
            hi = col.astype(BF16).astype(F32)
            mid = (col - hi).astype(BF16).astype(F32)
            lo = col - hi - mid
            piece = jnp.where(lane % 3 == 0, hi, jnp.where(lane % 3 == 1, mid, lo))
            extra_q = jnp.where(lane < 3, 1.0, jnp.where(lane < 6, piece, 0.0))
            extra_k = jnp.where(lane < 3, -piece, jnp.where(lane < 6, 1.0, 0.0))
            qa_ref[:, hh * FOX_AUG:hh * FOX_AUG + HEAD_DIM] = q_ref[:, hh * HEAD_DIM:(hh + 1) * HEAD_DIM]
            qa_ref[:, hh * FOX_AUG + HEAD_DIM:(hh + 1) * FOX_AUG] = extra_q.astype(BF16)
            ka_ref[:, hh * FOX_AUG:hh * FOX_AUG + HEAD_DIM] = k_ref[:, hh * HEAD_DIM:(hh + 1) * HEAD_DIM]
            ka_ref[:, hh * FOX_AUG + HEAD_DIM:(hh + 1) * FOX_AUG] = extra_k.astype(BF16)

    aug = pl.BlockSpec((tm, H * FOX_AUG), lambda i: (i, 0))
    return pl.pallas_call(
        body, name="fox_prep", grid=(S // tm,),
        in_specs=[pl.BlockSpec((tm, W), lambda i: (i, 4)), pl.BlockSpec((tm, W), lambda i: (i, 5)),
                  pl.BlockSpec((tm, W), lambda i: (i, 6)), pl.BlockSpec((tm, LANES), lambda i: (i, 0))],
        out_specs=[aug, aug, pl.BlockSpec((H * FOX_V_ROWS, tm), lambda i: (0, i))],
        out_shape=[jax.ShapeDtypeStruct((S, H * FOX_AUG), BF16)] * 2 + [jax.ShapeDtypeStruct((H * FOX_V_ROWS, S), BF16)],
        compiler_params=_params(),
    )(h, h, h, c)


def _causal_t(T):
    return lax.broadcasted_iota(jnp.int32, (T, T), 0) <= lax.broadcasted_iota(jnp.int32, (T, T), 1)


def _causal_pairs(n, by_query):
    if by_query:
        pairs = [(i, j) for i in range(n) for j in range(i + 1)]
    else:
        pairs = [(i, j) for j in range(n) for i in range(j, n)]
    return jnp.array([p[0] for p in pairs], jnp.int32), jnp.array([p[1] for p in pairs], jnp.int32)


def _fox_fwd(qa, ka, vt, h, H):
    S = h.shape[0]
    W = H * HEAD_DIM
    T = _tile(S, FOX_TILE, 128)
    nq = S // T
    k1 = HEAD_DIM ** -0.5 * LOG2E
    nt = (((1,), (1,)), ((), ()))

    qi, kj = _causal_pairs(nq, True)

    def body(qi_ref, kj_ref, q_ref, k_ref, vt_ref, z_ref, o_ref, g_ref, lse_ref, m_ref, acc_ref):
        i, j = qi_ref[pl.program_id(1)], kj_ref[pl.program_id(1)]

        @pl.when(j == 0)
        def _():
            m_ref[...] = jnp.full_like(m_ref, NEG)
            acc_ref[...] = jnp.zeros_like(acc_ref)

        def step(diag):
            raw = lax.dot_general(k_ref[...], q_ref[...], nt, preferred_element_type=F32)
            if diag:
                raw = jnp.where(_causal_t(T), raw, NEG)
            m_new = jnp.maximum(m_ref[...], jnp.max(raw, axis=0, keepdims=True))
            a = jnp.exp2((m_ref[...] - m_new) * k1)
            p = jnp.exp2((raw - m_new) * k1).astype(BF16)
            acc_ref[...] = a * acc_ref[...] + jnp.dot(vt_ref[...], p, preferred_element_type=F32)
            m_ref[...] = m_new

        @pl.when(j < i)
        def _():
            step(False)

        @pl.when(j == i)
        def _():
            step(True)

            acc = acc_ref[...]
            den = acc[HEAD_DIM:HEAD_DIM + 1, :]
            out = (acc[:HEAD_DIM, :] / den).T
            z = z_ref[...].astype(F32)
            o_ref[...] = out.astype(BF16)
            g_ref[...] = (out * z * _sigmoid(z)).astype(BF16)
            lse_ref[...] = m_ref[...] * k1 + jnp.log(den) * LOG2E

    out = pl.BlockSpec((T, HEAD_DIM), lambda hh, p, qi, kj: (qi[p], hh))
    grid_spec = pltpu.PrefetchScalarGridSpec(
        num_scalar_prefetch=2, grid=(H, qi.shape[0]),
        in_specs=[pl.BlockSpec((T, FOX_AUG), lambda hh, p, qi, kj: (qi[p], hh)),
                  pl.BlockSpec((T, FOX_AUG), lambda hh, p, qi, kj: (kj[p], hh)),
                  pl.BlockSpec((FOX_V_ROWS, T), lambda hh, p, qi, kj: (hh, kj[p])),
                  pl.BlockSpec((T, HEAD_DIM), lambda hh, p, qi, kj: (qi[p], 7 * H + hh))],
        out_specs=[out, out, pl.BlockSpec((None, 1, T), lambda hh, p, qi, kj: (hh, 0, qi[p]))],
        scratch_shapes=[pltpu.VMEM((1, T), F32), pltpu.VMEM((FOX_V_ROWS, T), F32)])
    return pl.pallas_call(
        body, name="fox_fwd", grid_spec=grid_spec,
        out_shape=[jax.ShapeDtypeStruct((S, W), BF16), jax.ShapeDtypeStruct((S, W), BF16),
                   jax.ShapeDtypeStruct((H, 1, S), F32)],
        compiler_params=_params(),
    )(qi, kj, qa, ka, vt, h)


def _fox_bwd(qa, ka, h, do, lse, dl, H):
    S = h.shape[0]
    W = H * HEAD_DIM
    T = _tile(S, FOX_TILE, 128)
    nq = S // T
    k1 = HEAD_DIM ** -0.5 * LOG2E
    nt = (((1,), (1,)), ((), ()))
    tn = (((0,), (0,)), ((), ()))

    qi, kj = _causal_pairs(nq, False)

    def body(qi_ref, kj_ref, q_ref, k_ref, v_ref, do_ref, lse_ref, dl_ref, dq_ref, dk_ref, dv_ref, ak_ref, av_ref):
        i, j = qi_ref[pl.program_id(1)], kj_ref[pl.program_id(1)]

        @pl.when(pl.program_id(1) == 0)
        def _():
            dq_ref[...] = jnp.zeros_like(dq_ref)

        @pl.when(i == j)
        def _():
            ak_ref[...] = jnp.zeros_like(ak_ref)
            av_ref[...] = jnp.zeros_like(av_ref)

        def step(diag):
            q, k, v, d_o = q_ref[...], k_ref[...], v_ref[...], do_ref[...]
            raw = lax.dot_general(k, q, nt, preferred_element_type=F32)
            if diag:
                raw = jnp.where(_causal_t(T), raw, NEG)
            p = jnp.exp2(raw * k1 - lse_ref[...])
            dp = lax.dot_general(v, d_o, nt, preferred_element_type=F32)
            ds = (p * (dp - dl_ref[...])).astype(BF16)
            av_ref[...] += jnp.dot(p.astype(BF16), d_o, preferred_element_type=F32)
            ak_ref[...] += jnp.dot(ds, q, preferred_element_type=F32)
            rows = pl.ds(pl.multiple_of(i * T, T), T)
            dq_ref[rows, :] += lax.dot_general(ds, k, tn, preferred_element_type=F32)

        @pl.when(i > j)
        def _():
            step(False)

        @pl.when(i == j)
        def _():
            step(True)

        @pl.when(i == nq - 1)
        def _():
            dk_ref[...] = ak_ref[...]
            dv_ref[...] = av_ref[...]

    qrow = lambda hh, p, qi, kj: (qi[p], hh)
    krow = lambda hh, p, qi, kj: (kj[p], hh)
    stat = pl.BlockSpec((None, 1, T), lambda hh, p, qi, kj: (hh, 0, qi[p]))
    grid_spec = pltpu.PrefetchScalarGridSpec(
        num_scalar_prefetch=2, grid=(H, qi.shape[0]),
        in_specs=[pl.BlockSpec((T, FOX_AUG), qrow), pl.BlockSpec((T, FOX_AUG), krow),
                  pl.BlockSpec((T, HEAD_DIM), lambda hh, p, qi, kj: (kj[p], 6 * H + hh)),
                  pl.BlockSpec((T, HEAD_DIM), qrow), stat, stat],
        out_specs=[pl.BlockSpec((S, FOX_AUG), lambda hh, p, qi, kj: (0, hh)), pl.BlockSpec((T, FOX_AUG), krow),
                   pl.BlockSpec((T, HEAD_DIM), krow)],
        scratch_shapes=[pltpu.VMEM((T, FOX_AUG), F32), pltpu.VMEM((T, HEAD_DIM), F32)])
    return pl.pallas_call(
        body, name="fox_bwd", grid_spec=grid_spec,
        out_shape=[jax.ShapeDtypeStruct((S, H * FOX_AUG), F32), jax.ShapeDtypeStruct((S, H * FOX_AUG), F32),
                   jax.ShapeDtypeStruct((S, W), F32)],
        compiler_params=_params(),
    )(qi, kj, qa, ka, h, do, lse, dl)


def _adamw_math(w, g, m, v):
    m = ADAM_B1 * m + (1.0 - ADAM_B1) * g
    v = ADAM_B2 * v + (1.0 - ADAM_B2) * (g * g)
    m_hat = m / (1.0 - ADAM_B1 ** ADAM_STEP)
    v_hat = v / (1.0 - ADAM_B2 ** ADAM_STEP)
    delta = -ADAM_LR * (m_hat / (jnp.sqrt(v_hat) + ADAM_EPS) + ADAM_WD * w)
    return delta, m, v


def _adamw(w, g_mine, g_theirs, m, v, name):
    L, R, C = w.shape
    half = L // 2
    tr = _rows(R, C * 4 * 9, 8)

    def body(c_ref, w_ref, gm_ref, gt_ref, m_ref, v_ref, g_ref, d_ref, nm_ref, nv_ref):
        g = jnp.where(pl.program_id(0) // half == c_ref[0], gm_ref[...], gt_ref[...])
        g_ref[...] = g
        d_ref[...], nm_ref[...], nv_ref[...] = _adamw_math(w_ref[...], g, m_ref[...], v_ref[...])

    blk = pl.BlockSpec((None, tr, C), lambda l, i, c: (l, i, 0))
    mine = pl.BlockSpec((None, tr, C), lambda l, i, c: (jnp.clip(l - c[0] * half, 0, half - 1), i, 0))
    theirs = pl.BlockSpec((None, tr, C), lambda l, i, c: (jnp.clip(l - (1 - c[0]) * half, 0, half - 1), i, 0))
    grid_spec = pltpu.PrefetchScalarGridSpec(
        num_scalar_prefetch=1, grid=(L, R // tr), in_specs=[blk, mine, theirs, blk, blk], out_specs=[blk] * 4)
    core = lax.axis_index("c").astype(jnp.int32).reshape(1)
    return pl.pallas_call(
        body, name=name, grid_spec=grid_spec, out_shape=[jax.ShapeDtypeStruct((L, R, C), F32)] * 4,
        compiler_params=_params(),
    )(core, w, g_mine, g_theirs, m, v)


def _place():
    x, y, c = lax.axis_index("x"), lax.axis_index("y"), lax.axis_index("c")
    return x, y, c, [(1 - x, y), (x, 1 - y), (1 - x, 1 - y)]


def _remote(src, dst, send_sems, recv_sems, k, to):
    return pltpu.make_async_remote_copy(src_ref=src, dst_ref=dst, send_sem=send_sems.at[k], recv_sem=recv_sems.at[k],
                                        device_id=to, device_id_type=MESH)


def _gather_weights(shards):
    n = len(shards)
    half = shards[0].shape[0] // 2

    def body(*refs):
        srcs, dsts = refs[:n], refs[n:2 * n]
        send_sems, recv_sems = refs[2 * n:]
        x, y, c, chips = _place()
        me = 2 * x + y
        mine, theirs = pl.ds(c * half, half), pl.ds((1 - c) * half, half)
        first = [_remote(srcs[a].at[mine], dsts[a].at[me, mine], send_sems, recv_sems, 6 * a + j, (px, py, c))
                 for a in range(n) for j, (px, py) in enumerate(chips)]
        for cp in first:
            cp.start()
        passed = []
        for a in range(n):
            for j, (px, py) in enumerate(chips):
                landed = dsts[a].at[2 * px + py, mine]
                _remote(landed, landed, send_sems, recv_sems, 6 * a + j, (px, py, c)).wait_recv()
                cp = _remote(landed, landed, send_sems, recv_sems, 6 * a + 3 + j, (x, y, 1 - c))
                cp.start()
                passed.append(cp)
        for a in range(n):
            for j, (px, py) in enumerate(chips):
                landed = dsts[a].at[2 * px + py, theirs]
                _remote(landed, landed, send_sems, recv_sems, 6 * a + 3 + j, (x, y, 1 - c)).wait_recv()
        for cp in first + passed:
            cp.wait_send()

    return pl.pallas_call(
        body, name="gather_weights", in_specs=[ANY] * n, out_specs=[ANY] * n,
        out_shape=[jax.ShapeDtypeStruct((4,) + s.shape, s.dtype) for s in shards],
        scratch_shapes=[pltpu.SemaphoreType.DMA((6 * n,)), pltpu.SemaphoreType.DMA((6 * n,))],
    )(*shards)


def _swap_other_half(parts):
    n = len(parts)
    half = parts[0].shape[1] // 2

    def body(*refs):
        srcs, dsts = refs[:n], refs[n:2 * n]
        send_sems, recv_sems = refs[2 * n:]
        x, y, c, _ = _place()
        cps = [_remote(srcs[a].at[:, pl.ds((1 - c) * half, half)], dsts[a], send_sems, recv_sems, a, (x, y, 1 - c))
               for a in range(n)]
        for cp in cps:
            cp.start()
        for cp in cps:
            cp.wait()

    return pl.pallas_call(
        body, name="grad_swap_half", in_specs=[ANY] * n, out_specs=[ANY] * n,
        out_shape=[jax.ShapeDtypeStruct((4, half) + p.shape[2:], p.dtype) for p in parts],
        scratch_shapes=[pltpu.SemaphoreType.DMA((n,)), pltpu.SemaphoreType.DMA((n,))],
    )(*parts)


def _add_half(part, got, name):
    _, half, R, C = got.shape
    tr = _rows(R, C * 2 * 3)

    def body(c_ref, p_ref, g_ref, o_ref):
        o_ref[...] = (p_ref[...].astype(F32) + g_ref[...].astype(F32)).astype(BF16)

    grid_spec = pltpu.PrefetchScalarGridSpec(
        num_scalar_prefetch=1, grid=(4, half, R // tr),
        in_specs=[pl.BlockSpec((None, None, tr, C), lambda s, l, i, c: (s, c[0] * half + l, i, 0)),
                  pl.BlockSpec((None, None, tr, C), lambda s, l, i, c: (s, l, i, 0))],
        out_specs=pl.BlockSpec((None, None, tr, C), lambda s, l, i, c: (s, l, i, 0)))
    core = lax.axis_index("c").astype(jnp.int32).reshape(1)
    return pl.pallas_call(
        body, name=name, grid_spec=grid_spec, out_shape=jax.ShapeDtypeStruct(got.shape, BF16),
        compiler_params=_params(),
    )(core, part, got)


def _scatter_to_owner(parts):
    n = len(parts)

    def body(*refs):
        srcs, dsts = refs[:n], refs[n:2 * n]
        send_sems, recv_sems, local_sems = refs[2 * n:]
        x, y, c, chips = _place()
        me = 2 * x + y
        local = [pltpu.make_async_copy(srcs[a].at[me], dsts[a].at[me], local_sems.at[a]) for a in range(n)]
        for cp in local:
            cp.start()
        sends = [_remote(srcs[a].at[2 * px + py], dsts[a].at[me], send_sems, recv_sems, 3 * a + j, (px, py, c))
                 for a in range(n) for j, (px, py) in enumerate(chips)]
        for cp in sends:
            cp.start()
        for a in range(n):
            for j, (px, py) in enumerate(chips):
                slot = dsts[a].at[2 * px + py]
                _remote(slot, slot, send_sems, recv_sems, 3 * a + j, (px, py, c)).wait_recv()
        for cp in sends:
            cp.wait_send()
        for cp in local:
            cp.wait()

    return pl.pallas_call(
        body, name="grad_scatter", in_specs=[ANY] * n, out_specs=[ANY] * n,
        out_shape=[jax.ShapeDtypeStruct(p.shape, p.dtype) for p in parts],
        scratch_shapes=[pltpu.SemaphoreType.DMA((3 * n,)), pltpu.SemaphoreType.DMA((3 * n,)),
                        pltpu.SemaphoreType.DMA((n,))],
    )(*parts)


def _sum_chips(got, name):
    _, half, R, C = got.shape
    tr = _rows(R, C * (2 * 4 + 4))

    def body(g_ref, o_ref):
        o_ref[...] = ((g_ref[0].astype(F32) + g_ref[1].astype(F32)) + g_ref[2].astype(F32)) + g_ref[3].astype(F32)

    return pl.pallas_call(
        body, name=name, grid=(half, R // tr),
        in_specs=[pl.BlockSpec((4, None, tr, C), lambda l, i: (0, l, i, 0))],
        out_specs=pl.BlockSpec((None, tr, C), lambda l, i: (l, i, 0)),
        out_shape=jax.ShapeDtypeStruct((half, R, C), F32), compiler_params=_params(),
    )(got)


def _share_halves(halves):
    n = len(halves)

    def body(*refs):
        srcs, dsts = refs[:n], refs[n:2 * n]
        send_sems, recv_sems = refs[2 * n:]
        x, y, c, _ = _place()
        cps = [_remote(srcs[a], dsts[a], send_sems, recv_sems, a, (x, y, 1 - c)) for a in range(n)]
        for cp in cps:
            cp.start()
        for cp in cps:
            cp.wait()

    return pl.pallas_call(
        body, name="grad_share_halves", in_specs=[ANY] * n, out_specs=[ANY] * n,
        out_shape=[jax.ShapeDtypeStruct(h.shape, h.dtype) for h in halves],
        scratch_shapes=[pltpu.SemaphoreType.DMA((n,)), pltpu.SemaphoreType.DMA((n,))],
    )(*halves)


def _small_allreduce_adamw(part, w, m, v):
    R = part.shape[0]
    deltas = [(dx, dy, dc) for dx in (0, 1) for dy in (0, 1) for dc in (0, 1)][1:]

    def body(p_ref, w_ref, m_ref, v_ref, g_ref, d_ref, nm_ref, nv_ref, all_ref, send_sems, recv_sems):
        x, y, c, _ = _place()
        me = 4 * x + 2 * y + c
        all_ref[me] = p_ref[...]
        cps = [_remote(p_ref, all_ref.at[me], send_sems, recv_sems, k, (x ^ dx, y ^ dy, c ^ dc))
               for k, (dx, dy, dc) in enumerate(deltas)]
        for cp in cps:
            cp.start()
        for k, (dx, dy, dc) in enumerate(deltas):
            slot = all_ref.at[4 * (x ^ dx) + 2 * (y ^ dy) + (c ^ dc)]
            _remote(slot, slot, send_sems, recv_sems, k, (x ^ dx, y ^ dy, c ^ dc)).wait_recv()
        for cp in cps:
            cp.wait_send()
        g = all_ref[0]
        for k in range(1, 8):
            g = g + all_ref[k]
        g_ref[...] = g
        d_ref[...], nm_ref[...], nv_ref[...] = _adamw_math(w_ref[...], g, m_ref[...], v_ref[...])

    vm = pl.BlockSpec(memory_space=pltpu.VMEM)
    shape = jax.ShapeDtypeStruct((R, LANES), F32)
    return pl.pallas_call(
        body, name="small_allreduce_adamw", in_specs=[vm] * 4, out_specs=[vm] * 4, out_shape=[shape] * 4,
        scratch_shapes=[pltpu.VMEM((8, R, LANES), F32), pltpu.SemaphoreType.DMA((7,)), pltpu.SemaphoreType.DMA((7,))],
    )(part, w, m, v)


def _pack_small(bf, bg, lg, lb, extra=None):
    L, H = bf.shape
    per = jnp.concatenate([jnp.pad(bf, ((0, 0), (0, LANES - H))), bg, lg, lb], axis=1)
    flat = per.reshape(-1, LANES)
    last = jnp.zeros((8 + (-flat.shape[0]) % 8, LANES), F32)
    if extra is not None:
        last = last.at[-8, 0].set(extra)
    return jnp.concatenate([flat, last], axis=0)


def _unpack_small(p, L, H, D):
    per = p[:L * (1 + 4 * D // LANES)].reshape(L, -1)
    return per[:, :H], per[:, LANES:LANES + 2 * D], per[:, LANES + 2 * D:LANES + 3 * D], per[:, LANES + 3 * D:]


def kernel(x, w_in, b_forget, b_gate, w_up_a, w_up_b, w_out, ln_g, ln_b, loss_target, m_w_in, m_b_forget, m_b_gate, m_w_up_a, m_w_up_b, m_w_out, m_ln_g, m_ln_b, v_w_in, v_b_forget, v_b_gate, v_w_up_a, v_w_up_b, v_w_out, v_ln_g, v_ln_b):
    _, S, D = x.shape
    L, _, C4 = w_in.shape
    H = b_forget.shape[1]
    W = w_up_a.shape[1]
    D4 = D // 4
    NC = 4 * C4
    assert W == H * HEAD_DIM and NC == 8 * W + H + 2 * D and L % 2 == 0 and D % LANES == 0
    alpha = float((2 * L) ** 0.25)
    f_block = 2 * D // LANES

    own = [w_in.astype(BF16), jnp.concatenate([w_up_a, w_up_b], axis=2).astype(BF16), w_out.astype(BF16)]
    gathered = _gather_weights(own)
    me = 2 * lax.axis_index("x") + lax.axis_index("y")
    shard = lambda a, s, l: jnp.where(me == s, own[a][l], gathered[a][s, l])
    w_main, w_fg, w_ua, w_ub, w_o = [], [], [], [], []
    for l in range(L):
        full = jnp.concatenate([shard(0, s, l) for s in range(4)], axis=1)
        w_main.append(full[:, :8 * W])
        w_fg.append(jnp.concatenate([full[:, 8 * W + H:], full[:, 8 * W:8 * W + H],
                                     jnp.zeros((D, LANES - H), BF16)], axis=1))
        ups = [shard(1, s, l) for s in range(4)]
        w_ua.append(jnp.concatenate([u[:, :D4] for u in ups], axis=1))
        w_ub.append(jnp.concatenate([u[:, D4:] for u in ups], axis=1))
        w_o.append(jnp.concatenate([shard(2, s, l) for s in range(4)], axis=0))

    pos = jnp.arange(S, dtype=F32)
    inv_freq = ROPE_THETA ** (-jnp.arange(HEAD_DIM // 2, dtype=F32) / (HEAD_DIM // 2))
    ang = pos[:, None] * inv_freq[None, :]
    cos = jnp.concatenate([jnp.cos(ang), jnp.cos(ang)], axis=1)
    sin = jnp.concatenate([-jnp.sin(ang), jnp.sin(ang)], axis=1)
    bf_pad = jnp.pad(b_forget, ((0, 0), (0, LANES - H)))

    xs = x[0]
    xb = xs.astype(BF16)
    saved = []
    for l in range(L):
        h = _matmul(xb, w_main[l], mode="nn", out_dtype=BF16, name="in_proj", rope=(cos, sin), rope_cols=2 * W)
        hfg = _matmul(xb, w_fg[l], mode="nn", out_dtype=F32, name="in_proj_gates")
        qkv_a = h[:, :3 * W]
        views = [qkv_a.reshape(S // d, d * 3 * W) for _, d in DILATED_PATTERNS]
        os, lses = [], []
        for (_, d), hv in zip(DILATED_PATTERNS, views):
            o, lse = _dil_fwd(hv, d, W, f"dil_fwd_d{d}")
            os.append(o.reshape(S, W))
            lses.append(lse.transpose(1, 0, 2).reshape(S, H))
        out_a, lse_a, ga = _dil_combine_fwd(os, lses, h, W)
        c = _scan_fwd(hfg, bf_pad[l:l + 1], f_block)
        qa, ka, vt = _fox_prep(h, c, H)
        out_b, gb, lse_b = _fox_fwd(qa, ka, vt, h, H)
        up_a = _matmul(ga, w_ua[l], mode="nn", out_dtype=BF16, name="up_proj")
        up_b = _matmul(gb, w_ub[l], mode="nn", out_dtype=BF16, name="up_proj")
        u = _merge_fwd(up_a, up_b, hfg, b_gate[l:l + 1])
        r = _matmul(u, w_o[l], mode="nn", out_dtype=F32, name="out_proj", acc_in=xs, acc_scale=alpha)
        saved.append((xb, h, hfg, views, out_a, lse_a, ga, qa, ka, out_b, gb, lse_b, up_a, up_b, u, r))
        xs, xb = _ln_fwd(r, ln_g[l:l + 1], ln_b[l:l + 1])

    dx, sq = _loss(xs, loss_target[0])
    loss_part = 0.5 * jnp.sum(sq) / D

    g_in, g_up, g_out, g_bf, g_bg, g_lg, g_lb = [], [], [], [], [], [], []
    for l in reversed(range(L)):
        xb, h, hfg, views, out_a, lse_a, ga, qa, ka, out_b, gb, lse_b, up_a, up_b, u, r = saved[l]
        dr, adr, dlg, dlb = _ln_bwd(dx, r, ln_g[l:l + 1], alpha)
        du = _matmul(dr, w_o[l], mode="nt", out_dtype=F32, name="out_proj_dx")
        dwo = _matmul(u, dr, mode="tn", out_dtype=F32, name="out_proj_dw")
        dua, dub, dgl, dbg = _merge_bwd(du, up_a, up_b, hfg, b_gate[l:l + 1])
        dga = _matmul(dua, w_ua[l], mode="nt", out_dtype=F32, name="up_proj_dx")
        dgb = _matmul(dub, w_ub[l], mode="nt", out_dtype=F32, name="up_proj_dx")
        dwua = _matmul(ga, dua, mode="tn", out_dtype=F32, name="up_proj_dw")
        dwub = _matmul(gb, dub, mode="tn", out_dtype=F32, name="up_proj_dw")
        do_a, dz_a, dl_a = _gate_bwd(dga, out_a, h, 3, "gate_bwd_a")
        dqs, dks, dvs = [], [], []
        for (_, d), hv in zip(DILATED_PATTERNS, views):
            stat = lambda t: t.reshape(S // d, d, H).transpose(1, 0, 2)
            dq, dk, dv = _dil_bwd(hv, do_a.reshape(S // d, d * W), stat(lse_a), stat(dl_a), d, W, f"dil_bwd_d{d}")
            dqs.append(dq.reshape(S, W))
            dks.append(dk.reshape(S, W))
            dvs.append(dv.reshape(S, W))
        dqkv_a = _dil_combine_bwd(dqs, dks, dvs, cos, sin)
        do_b, dz_b, dl_b = _gate_bwd(dgb, out_b, h, 7, "gate_bwd_b")
        dq_f, dk_f, dv_b = _fox_bwd(qa, ka, h, do_b, lse_b, dl_b.T.reshape(H, 1, S), H)
        dq_f, dk_f = dq_f.reshape(S, H, FOX_AUG), dk_f.reshape(S, H, FOX_AUG)
        dc = jnp.pad(dq_f[:, :, K_ONES] - dk_f[:, :, Q_ONES], ((0, 0), (0, LANES - H)))
        df, dbf = _scan_bwd(dc, hfg, bf_pad[l:l + 1], f_block, H)
        att_scale = HEAD_DIM ** -0.5
        dq_b = (dq_f[:, :, :HEAD_DIM] * att_scale).astype(BF16).reshape(S, W)
        dk_b = (dk_f[:, :, :HEAD_DIM] * att_scale).astype(BF16).reshape(S, W)
        dh = jnp.concatenate([dqkv_a, dz_a, dq_b, dk_b, dv_b.astype(BF16), dz_b], axis=1)
        dhfg = jnp.concatenate([dgl, df], axis=1)
        dx1 = _matmul(dh, w_main[l], mode="nt", out_dtype=F32, name="in_proj_dx", acc_in=adr)
        dx = _matmul(dhfg, w_fg[l], mode="nt", out_dtype=F32, name="in_proj_gates_dx", acc_in=dx1)
        dwm = _matmul(xb, dh, mode="tn", out_dtype=F32, name="in_proj_dw")
        dwfg = _matmul(xb, dhfg, mode="tn", out_dtype=F32, name="in_proj_gates_dw")
        full = jnp.concatenate([dwm, dwfg[:, 2 * D:2 * D + H], dwfg[:, :2 * D]], axis=1)
        g_in.append(full.reshape(D, 4, C4).transpose(1, 0, 2).astype(BF16))
        g_up.append(jnp.concatenate([dwua.reshape(W, 4, D4), dwub.reshape(W, 4, D4)], axis=2).transpose(1, 0, 2).astype(BF16))
        g_out.append(dwo.reshape(4, D4, D).astype(BF16))
        g_bf.append(dbf[0, :H])
        g_bg.append(dbg[0])
        g_lg.append(dlg[0])
        g_lb.append(dlb[0])
    grad_x = dx[None]
    for lst in (g_in, g_up, g_out, g_bf, g_bg, g_lg, g_lb):
        lst.reverse()

    parts = [jnp.stack(g_in, axis=1), jnp.stack(g_up, axis=1), jnp.stack(g_out, axis=1)]
    names = ["w_in", "w_up", "w_out"]
    got = _swap_other_half(parts)
    chip = [_add_half(p, g, f"grad_add_half_{n}") for p, g, n in zip(parts, got, names)]
    landed = _scatter_to_owner(chip)
    halves = [_sum_chips(g, f"grad_sum_chips_{n}") for g, n in zip(landed, names)]
    theirs = _share_halves(halves)

    pair = lambda a, b: jnp.concatenate([a, b], axis=2)
    grad_w_in, d_in, nm_in, nv_in = _adamw(w_in, halves[0], theirs[0], m_w_in, v_w_in, "adamw_w_in")
    up = _adamw(pair(w_up_a, w_up_b), halves[1], theirs[1], pair(m_w_up_a, m_w_up_b), pair(v_w_up_a, v_w_up_b),
                "adamw_w_up")
    (grad_w_up_a, grad_w_up_b), (d_ua, d_ub), (nm_ua, nm_ub), (nv_ua, nv_ub) = [
        (t[:, :, :D4], t[:, :, D4:]) for t in up]
    grad_w_out, d_o, nm_o, nv_o = _adamw(w_out, halves[2], theirs[2], m_w_out, v_w_out, "adamw_w_out")

    small_g = _pack_small(jnp.stack(g_bf), jnp.stack(g_bg), jnp.stack(g_lg), jnp.stack(g_lb), loss_part)
    small = _small_allreduce_adamw(small_g, _pack_small(b_forget, b_gate, ln_g, ln_b),
                                   _pack_small(m_b_forget, m_b_gate, m_ln_g, m_ln_b),
                                   _pack_small(v_b_forget, v_b_gate, v_ln_g, v_ln_b))
    loss = small[0][-8, 0]
    (g_bf, g_bg, g_lg, g_lb), (d_bf, d_bg, d_lg, d_lb), (nm_bf, nm_bg, nm_lg, nm_lb), (nv_bf, nv_bg, nv_lg, nv_lb) = [
        _unpack_small(p, L, H, D) for p in small]

    return (loss, grad_x,
            grad_w_in, g_bf, g_bg, grad_w_up_a, grad_w_up_b, grad_w_out, g_lg, g_lb,
            d_in, d_bf, d_bg, d_ua, d_ub, d_o, d_lg, d_lb,
            nm_in, nm_bf, nm_bg, nm_ua, nm_ub, nm_o, nm_lg, nm_lb,
            nv_in, nv_bf, nv_bg, nv_ua, nv_ub, nv_o, nv_lg, nv_lb)
```

```python
import functools

import jax
import jax.numpy as jnp
from jax import lax
from jax.experimental import pallas as pl
from jax.experimental.pallas import tpu as pltpu

F32 = jnp.float32
BF16 = jnp.bfloat16
MESH = pl.DeviceIdType.MESH
ANY = pl.BlockSpec(memory_space=pl.ANY)

HEAD_DIM = 128
LANES = 128
Q_BLOCK = 128
DILATED_PATTERNS = ((128, 1), (512, 4), (2048, 16))
ROPE_THETA = 10000.0
LN_EPS = 1e-5
ADAM_LR, ADAM_B1, ADAM_B2, ADAM_EPS, ADAM_WD, ADAM_STEP = 0.001, 0.9, 0.999, 1e-08, 0.01, 10
NEG = -1e30
VMEM_LIMIT = 56 * 2**20
ELEMENTWISE_BUDGET = 20 * 2**20
FOX_TILE = 1024
MM_TILES = (1024, 1024, 2048)


def _params():
    return pltpu.CompilerParams(vmem_limit_bytes=VMEM_LIMIT)


def _tile(dim, target, align):
    if dim <= target:
        return dim
    t = (target // align) * align
    while t >= align:
        if dim % t == 0:
            return t
        t -= align
    return dim


def _rows(n_rows, bytes_per_row, align=16):
    return _tile(n_rows, max(align, ELEMENTWISE_BUDGET // (2 * bytes_per_row)), align)


def _sigmoid(v):
    return 1.0 / (1.0 + jnp.exp(-v))


def _matmul(a, b, *, mode, out_dtype, name, acc_in=None, acc_scale=1.0, rope=None, rope_cols=0):
    if mode == "nn":
        (M, K), (K2, N) = a.shape, b.shape
    elif mode == "nt":
        (M, K), (N, K2) = a.shape, b.shape
    else:
        (K, M), (K2, N) = a.shape, b.shape
    assert K == K2, (a.shape, b.shape, mode)
    tm, tn, tk = _tile(M, MM_TILES[0], 128), _tile(N, MM_TILES[1], 128), _tile(K, MM_TILES[2], 128)
    if rope is not None:
        assert rope_cols % tn == 0
    nk = K // tk
    n_rope_tiles = rope_cols // tn if rope is not None else 0
    dims = {"nn": (((1,), (0,)), ((), ())), "nt": (((1,), (1,)), ((), ())), "tn": (((0,), (0,)), ((), ()))}[mode]

    def body(*refs):
        a_ref, b_ref = refs[0], refs[1]
        pos = 2
        if rope is not None:
            cos_ref, sin_ref = refs[pos], refs[pos + 1]
            pos += 2
        if acc_in is not None:
            acc_in_ref = refs[pos]
            pos += 1
        o_ref, acc_ref = refs[pos], refs[pos + 1]
        j, k = pl.program_id(1), pl.program_id(2)

        @pl.when(k == 0)
        def _():
            acc_ref[...] = jnp.zeros_like(acc_ref)

        acc_ref[...] += lax.dot_general(a_ref[...], b_ref[...], dims, preferred_element_type=F32)

        def finish(rotate):
            r = acc_ref[...]
            if acc_in is not None:
                r = r + acc_scale * acc_in_ref[...]
            if rotate:
                cos, sin = cos_ref[...], sin_ref[...]
                for g in range(tn // HEAD_DIM):
                    sl = slice(g * HEAD_DIM, (g + 1) * HEAD_DIM)
                    t = r[:, sl]
                    o_ref[:, sl] = (t * cos + pltpu.roll(t, HEAD_DIM // 2, 1) * sin).astype(o_ref.dtype)
            else:
                o_ref[...] = r.astype(o_ref.dtype)

        if n_rope_tiles:
            @pl.when((k == nk - 1) & (j < n_rope_tiles))
            def _():
                finish(True)

            @pl.when((k == nk - 1) & (j >= n_rope_tiles))
            def _():
                finish(False)
        else:
            @pl.when(k == nk - 1)
            def _():
                finish(False)

    if mode == "nn":
        in_specs = [pl.BlockSpec((tm, tk), lambda i, j, k: (i, k)), pl.BlockSpec((tk, tn), lambda i, j, k: (k, j))]
    elif mode == "nt":
        in_specs = [pl.BlockSpec((tm, tk), lambda i, j, k: (i, k)), pl.BlockSpec((tn, tk), lambda i, j, k: (j, k))]
    else:
        in_specs = [pl.BlockSpec((tk, tm), lambda i, j, k: (k, i)), pl.BlockSpec((tk, tn), lambda i, j, k: (k, j))]
    args = [a, b]
    if rope is not None:
        in_specs += [pl.BlockSpec((tm, HEAD_DIM), lambda i, j, k: (i, 0))] * 2
        args += list(rope)
    if acc_in is not None:
        in_specs.append(pl.BlockSpec((tm, tn), lambda i, j, k: (i, j)))
        args.append(acc_in)
    return pl.pallas_call(
        body, name=name, grid=(M // tm, N // tn, nk), in_specs=in_specs,
        out_specs=pl.BlockSpec((tm, tn), lambda i, j, k: (i, j)),
        out_shape=jax.ShapeDtypeStruct((M, N), out_dtype),
        scratch_shapes=[pltpu.VMEM((tm, tn), F32)], compiler_params=_params(),
    )(*args)


def _ln_fwd(r, g, b):
    S, D = r.shape
    tm = _rows(S, D * (4 + 4 + 2))

    def body(r_ref, g_ref, b_ref, x_ref, xb_ref):
        v = r_ref[...]
        mu = jnp.mean(v, axis=1, keepdims=True)
        cen = v - mu
        var = jnp.mean(cen * cen, axis=1, keepdims=True)
        out = cen * lax.rsqrt(var + LN_EPS) * g_ref[...] + b_ref[...]
        x_ref[...] = out
        xb_ref[...] = out.astype(BF16)

    row = pl.BlockSpec((tm, D), lambda i: (i, 0))
    vec = pl.BlockSpec((1, D), lambda i: (0, 0))
    return pl.pallas_call(
        body, name="ln_fwd", grid=(S // tm,), in_specs=[row, vec, vec], out_specs=[row, row],
        out_shape=[jax.ShapeDtypeStruct((S, D), F32), jax.ShapeDtypeStruct((S, D), BF16)],
        compiler_params=_params(),
    )(r, g, b)


def _ln_bwd(dx, r, g, alpha):
    S, D = r.shape
    tm = _rows(S, D * (4 + 4 + 2 + 4))

    def body(dx_ref, r_ref, g_ref, drb_ref, adr_ref, dg_ref, db_ref):
        @pl.when(pl.program_id(0) == 0)
        def _():
            dg_ref[...] = jnp.zeros_like(dg_ref)
            db_ref[...] = jnp.zeros_like(db_ref)

        v, d = r_ref[...], dx_ref[...]
        mu = jnp.mean(v, axis=1, keepdims=True)
        cen = v - mu
        var = jnp.mean(cen * cen, axis=1, keepdims=True)
        rstd = lax.rsqrt(var + LN_EPS)
        xhat = cen * rstd
        dxhat = d * g_ref[...]
        dr = rstd * (dxhat - jnp.mean(dxhat, axis=1, keepdims=True)
                     - xhat * jnp.mean(dxhat * xhat, axis=1, keepdims=True))
        drb_ref[...] = dr.astype(BF16)
        adr_ref[...] = alpha * dr
        dg_ref[...] += jnp.sum(d * xhat, axis=0, keepdims=True)
        db_ref[...] += jnp.sum(d, axis=0, keepdims=True)

    row = pl.BlockSpec((tm, D), lambda i: (i, 0))
    vec = pl.BlockSpec((1, D), lambda i: (0, 0))
    return pl.pallas_call(
        body, name="ln_bwd", grid=(S // tm,), in_specs=[row, row, vec], out_specs=[row, row, vec, vec],
        out_shape=[jax.ShapeDtypeStruct((S, D), BF16), jax.ShapeDtypeStruct((S, D), F32),
                   jax.ShapeDtypeStruct((1, D), F32), jax.ShapeDtypeStruct((1, D), F32)],
        compiler_params=_params(),
    )(dx, r, g)


def _loss(y, target):
    S, D = y.shape
    tm = _rows(S, D * 12)

    def body(y_ref, t_ref, dy_ref, sq_ref):
        @pl.when(pl.program_id(0) == 0)
        def _():
            sq_ref[...] = jnp.zeros_like(sq_ref)

        err = y_ref[...] - t_ref[...]
        dy_ref[...] = err * (1.0 / D)
        sq_ref[...] += jnp.sum(err * err, axis=0, keepdims=True)

    row = pl.BlockSpec((tm, D), lambda i: (i, 0))
    vec = pl.BlockSpec((1, D), lambda i: (0, 0))
    return pl.pallas_call(
        body, name="loss", grid=(S // tm,), in_specs=[row, row], out_specs=[row, vec],
        out_shape=[jax.ShapeDtypeStruct((S, D), F32), jax.ShapeDtypeStruct((1, D), F32)],
        compiler_params=_params(),
    )(y, target)


def _merge_fwd(up_a, up_b, hfg, b_gate):
    S, D = up_a.shape
    tm = _rows(S, D * (2 + 2 + 4 + 4 + 2))

    def body(ua_ref, ub_ref, gla_ref, glb_ref, bga_ref, bgb_ref, u_ref):
        ga = _sigmoid(gla_ref[...] + bga_ref[...])
        gb = _sigmoid(glb_ref[...] + bgb_ref[...])
        u_ref[...] = (ga * ua_ref[...].astype(F32) + gb * ub_ref[...].astype(F32)).astype(BF16)

    row = pl.BlockSpec((tm, D), lambda i: (i, 0))
    row1 = pl.BlockSpec((tm, D), lambda i: (i, 1))
    v0 = pl.BlockSpec((1, D), lambda i: (0, 0))
    v1 = pl.BlockSpec((1, D), lambda i: (0, 1))
    return pl.pallas_call(
        body, name="merge_fwd", grid=(S // tm,), in_specs=[row, row, row, row1, v0, v1], out_specs=row,
        out_shape=jax.ShapeDtypeStruct((S, D), BF16), compiler_params=_params(),
    )(up_a, up_b, hfg, hfg, b_gate, b_gate)


def _merge_bwd(du, up_a, up_b, hfg, b_gate):
    S, D = up_a.shape
    tm = _rows(S, D * (4 + 2 + 2 + 4 + 4 + 2 + 2 + 4))

    def body(du_ref, ua_ref, ub_ref, gla_ref, glb_ref, bga_ref, bgb_ref, dua_ref, dub_ref, dgl_ref, dbg_ref):
        @pl.when(pl.program_id(0) == 0)
        def _():
            dbg_ref[...] = jnp.zeros_like(dbg_ref)

        du = du_ref[...]
        ga = _sigmoid(gla_ref[...] + bga_ref[...])
        gb = _sigmoid(glb_ref[...] + bgb_ref[...])
        dua_ref[...] = (du * ga).astype(BF16)
        dub_ref[...] = (du * gb).astype(BF16)
        dgla = du * ua_ref[...].astype(F32) * ga * (1.0 - ga)
        dglb = du * ub_ref[...].astype(F32) * gb * (1.0 - gb)
        dgl_ref[:, :D] = dgla.astype(BF16)
        dgl_ref[:, D:] = dglb.astype(BF16)
        dbg_ref[:, :D] += jnp.sum(dgla, axis=0, keepdims=True)
        dbg_ref[:, D:] += jnp.sum(dglb, axis=0, keepdims=True)

    row = pl.BlockSpec((tm, D), lambda i: (i, 0))
    row1 = pl.BlockSpec((tm, D), lambda i: (i, 1))
    v0 = pl.BlockSpec((1, D), lambda i: (0, 0))
    v1 = pl.BlockSpec((1, D), lambda i: (0, 1))
    return pl.pallas_call(
        body, name="merge_bwd", grid=(S // tm,), in_specs=[row, row, row, row, row1, v0, v1],
        out_specs=[row, row, pl.BlockSpec((tm, 2 * D), lambda i: (i, 0)), pl.BlockSpec((1, 2 * D), lambda i: (0, 0))],
        out_shape=[jax.ShapeDtypeStruct((S, D), BF16), jax.ShapeDtypeStruct((S, D), BF16),
                   jax.ShapeDtypeStruct((S, 2 * D), BF16), jax.ShapeDtypeStruct((1, 2 * D), F32)],
        compiler_params=_params(),
    )(du, up_a, up_b, hfg, hfg, b_gate, b_gate)


def _gate_bwd(dg, out, h, z_block, name):
    S, W = out.shape
    H = W // HEAD_DIM
    tm = _rows(S, W * (4 + 2 + 2 + 2 + 2 + 4))

    def body(dg_ref, o_ref, z_ref, do_ref, dz_ref, dl_ref):
        z = z_ref[...].astype(F32)
        o = o_ref[...].astype(F32)
        d = dg_ref[...]
        sg = _sigmoid(z)
        dout = d * z * sg
        do_ref[...] = dout.astype(BF16)
        dz_ref[...] = (d * o * sg * (1.0 + z * (1.0 - sg))).astype(BF16)
        prod = dout * o
        for hh in range(H):
            sl = slice(hh * HEAD_DIM, (hh + 1) * HEAD_DIM)
            dl_ref[:, hh:hh + 1] = jnp.sum(prod[:, sl], axis=1, keepdims=True)

    row = pl.BlockSpec((tm, W), lambda i: (i, 0))
    return pl.pallas_call(
        body, name=name, grid=(S // tm,),
        in_specs=[row, row, pl.BlockSpec((tm, W), lambda i: (i, z_block))],
        out_specs=[row, row, pl.BlockSpec((tm, H), lambda i: (i, 0))],
        out_shape=[jax.ShapeDtypeStruct((S, W), BF16), jax.ShapeDtypeStruct((S, W), BF16),
                   jax.ShapeDtypeStruct((S, H), F32)],
        compiler_params=_params(),
    )(dg, out, h)


def _dil_mask(n):
    i = lax.broadcasted_iota(jnp.int32, (Q_BLOCK, 2 * Q_BLOCK), 0)
    j = lax.broadcasted_iota(jnp.int32, (Q_BLOCK, 2 * Q_BLOCK), 1)
    return (j >= i) & (j <= i + Q_BLOCK) & ((n > 0) | (j >= Q_BLOCK))


def _two_blocks(prev_ref, cur_ref, sl):
    return jnp.concatenate([prev_ref[:, sl], cur_ref[:, sl]], axis=0)


def _dil_fwd(hv, d, W, name):
    L = hv.shape[0]
    H = W // HEAD_DIM
    nblk = L // Q_BLOCK
    scale = HEAD_DIM ** -0.5
    nt = (((1,), (1,)), ((), ()))

    def body(q_ref, kp_ref, kc_ref, vp_ref, vc_ref, o_ref, lse_ref):
        mask = _dil_mask(pl.program_id(1))
        for hh in range(H):
            sl = slice(hh * HEAD_DIM, (hh + 1) * HEAD_DIM)
            k, v = _two_blocks(kp_ref, kc_ref, sl), _two_blocks(vp_ref, vc_ref, sl)
            s = jnp.where(mask, lax.dot_general(q_ref[:, sl], k, nt, preferred_element_type=F32) * scale, NEG)
            m = jnp.max(s, axis=1, keepdims=True)
            p = jnp.exp(s - m)
            den = jnp.sum(p, axis=1, keepdims=True)
            acc = jnp.dot(p.astype(BF16), v, preferred_element_type=F32)
            o_ref[:, sl] = (acc / den).astype(BF16)
            lse_ref[:, hh:hh + 1] = m + jnp.log(den)

    def spec(col, prev):
        if prev:
            return pl.BlockSpec((Q_BLOCK, W), lambda r, n: (jnp.maximum(n - 1, 0), r * 3 + col))
        return pl.BlockSpec((Q_BLOCK, W), lambda r, n: (n, r * 3 + col))

    return pl.pallas_call(
        body, name=name, grid=(d, nblk),
        in_specs=[spec(0, False), spec(1, True), spec(1, False), spec(2, True), spec(2, False)],
        out_specs=[pl.BlockSpec((Q_BLOCK, W), lambda r, n: (n, r)),
                   pl.BlockSpec((None, Q_BLOCK, H), lambda r, n: (r, n, 0))],
        out_shape=[jax.ShapeDtypeStruct((L, d * W), BF16), jax.ShapeDtypeStruct((d, L, H), F32)],
        compiler_params=_params(),
    )(hv, hv, hv, hv, hv)


def _dil_combine_fwd(os, lses, h, W):
    S = h.shape[0]
    H = W // HEAD_DIM
    tm = _rows(S, W * (3 * 2 + 2 + 2 + 2) + 4 * H * 4)

    def body(o1, o2, o3, l1, l2, l3, z_ref, out_ref, lse_ref, g_ref):
        a, b, c = l1[...], l2[...], l3[...]
        m = jnp.maximum(jnp.maximum(a, b), c)
        ea, eb, ec = jnp.exp(a - m), jnp.exp(b - m), jnp.exp(c - m)
        den = ea + eb + ec
        wa, wb, wc = ea / den, eb / den, ec / den
        lse_ref[...] = m + jnp.log(den)
        for hh in range(H):
            sl = slice(hh * HEAD_DIM, (hh + 1) * HEAD_DIM)
            out = (wa[:, hh:hh + 1] * o1[:, sl].astype(F32) + wb[:, hh:hh + 1] * o2[:, sl].astype(F32)
                   + wc[:, hh:hh + 1] * o3[:, sl].astype(F32))
            z = z_ref[:, sl].astype(F32)
            out_ref[:, sl] = out.astype(BF16)
            g_ref[:, sl] = (out * z * _sigmoid(z)).astype(BF16)

    row = pl.BlockSpec((tm, W), lambda i: (i, 0))
    stat = pl.BlockSpec((tm, H), lambda i: (i, 0))
    return pl.pallas_call(
        body, name="dil_combine_fwd", grid=(S // tm,),
        in_specs=[row] * 3 + [stat] * 3 + [pl.BlockSpec((tm, W), lambda i: (i, 3))], out_specs=[row, stat, row],
        out_shape=[jax.ShapeDtypeStruct((S, W), BF16), jax.ShapeDtypeStruct((S, H), F32),
                   jax.ShapeDtypeStruct((S, W), BF16)],
        compiler_params=_params(),
    )(*os, *lses, h)


def _dil_bwd(hv, dov, lsev, dlv, d, W, name):
    L = hv.shape[0]
    H = W // HEAD_DIM
    nblk = L // Q_BLOCK
    scale = HEAD_DIM ** -0.5
    nt = (((1,), (1,)), ((), ()))
    tn = (((0,), (0,)), ((), ()))

    def body(q_ref, kp_ref, kc_ref, vp_ref, vc_ref, do_ref, lse_ref, dl_ref, dq_ref, dk_ref, dv_ref, ck_ref, cv_ref):
        n = pl.program_id(1)

        @pl.when(n == 0)
        def _():
            ck_ref[...] = jnp.zeros_like(ck_ref)
            cv_ref[...] = jnp.zeros_like(cv_ref)

        @pl.when(n < nblk)
        def _():
            mask = _dil_mask(n)
            for hh in range(H):
                sl = slice(hh * HEAD_DIM, (hh + 1) * HEAD_DIM)
                q, do = q_ref[:, sl], do_ref[:, sl]
                k, v = _two_blocks(kp_ref, kc_ref, sl), _two_blocks(vp_ref, vc_ref, sl)
                s = jnp.where(mask, lax.dot_general(q, k, nt, preferred_element_type=F32) * scale, NEG)
                p = jnp.exp(s - lse_ref[:, hh:hh + 1])
                ds = p * (lax.dot_general(do, v, nt, preferred_element_type=F32) - dl_ref[:, hh:hh + 1]) * scale
                ds_b = ds.astype(BF16)
                dq_ref[:, sl] = jnp.dot(ds_b, k, preferred_element_type=F32).astype(BF16)
                dk = lax.dot_general(ds_b, q, tn, preferred_element_type=F32)
                dv = lax.dot_general(p.astype(BF16), do, tn, preferred_element_type=F32)
                dk_ref[:, sl] = (ck_ref[:, sl] + dk[:Q_BLOCK]).astype(BF16)
                dv_ref[:, sl] = (cv_ref[:, sl] + dv[:Q_BLOCK]).astype(BF16)
                ck_ref[:, sl] = dk[Q_BLOCK:]
                cv_ref[:, sl] = dv[Q_BLOCK:]

        @pl.when(n == nblk)
        def _():
            dk_ref[...] = ck_ref[...].astype(BF16)
            dv_ref[...] = cv_ref[...].astype(BF16)

    last = nblk - 1

    def hspec(col, prev):
        if prev:
            return pl.BlockSpec((Q_BLOCK, W), lambda r, n: (jnp.clip(n - 1, 0, last), r * 3 + col))
        return pl.BlockSpec((Q_BLOCK, W), lambda r, n: (jnp.minimum(n, last), r * 3 + col))

    cur = pl.BlockSpec((Q_BLOCK, W), lambda r, n: (jnp.minimum(n, last), r))
    lag = pl.BlockSpec((Q_BLOCK, W), lambda r, n: (jnp.maximum(n - 1, 0), r))
    stat = pl.BlockSpec((None, Q_BLOCK, H), lambda r, n: (r, jnp.minimum(n, last), 0))
    shape = jax.ShapeDtypeStruct((L, d * W), BF16)
    return pl.pallas_call(
        body, name=name, grid=(d, nblk + 1),
        in_specs=[hspec(0, False), hspec(1, True), hspec(1, False), hspec(2, True), hspec(2, False), cur, stat, stat],
        out_specs=[cur, lag, lag], out_shape=[shape, shape, shape],
        scratch_shapes=[pltpu.VMEM((Q_BLOCK, W), F32), pltpu.VMEM((Q_BLOCK, W), F32)],
        compiler_params=_params(),
    )(hv, hv, hv, hv, hv, dov, lsev, dlv)


def _dil_combine_bwd(dqs, dks, dvs, cos, sin):
    S, W = dqs[0].shape
    tm = _rows(S, W * (9 * 2 + 3 * 2 + 3 * 4))

    def body(q1, q2, q3, k1, k2, k3, v1, v2, v3, cos_ref, sin_ref, o_ref):
        cos_t, sin_t = cos_ref[...], -sin_ref[...]
        add3 = lambda a, b, c: a[...].astype(F32) + b[...].astype(F32) + c[...].astype(F32)
        dq = add3(q1, q2, q3)
        dk = add3(k1, k2, k3)
        for hh in range(W // HEAD_DIM):
            sl = slice(hh * HEAD_DIM, (hh + 1) * HEAD_DIM)
            tq, tk = dq[:, sl], dk[:, sl]
            o_ref[:, hh * HEAD_DIM:(hh + 1) * HEAD_DIM] = (
                tq * cos_t + pltpu.roll(tq, HEAD_DIM // 2, 1) * sin_t).astype(BF16)
            o_ref[:, W + hh * HEAD_DIM:W + (hh + 1) * HEAD_DIM] = (
                tk * cos_t + pltpu.roll(tk, HEAD_DIM // 2, 1) * sin_t).astype(BF16)
        o_ref[:, 2 * W:] = add3(v1, v2, v3).astype(BF16)

    row = pl.BlockSpec((tm, W), lambda i: (i, 0))
    tab = pl.BlockSpec((tm, HEAD_DIM), lambda i: (i, 0))
    return pl.pallas_call(
        body, name="dil_combine_bwd", grid=(S // tm,), in_specs=[row] * 9 + [tab, tab],
        out_specs=pl.BlockSpec((tm, 3 * W), lambda i: (i, 0)),
        out_shape=jax.ShapeDtypeStruct((S, 3 * W), BF16), compiler_params=_params(),
    )(*dqs, *dks, *dvs, cos, sin)


def _scan_tile(S):
    return _tile(S, 256, 8)


def _scan_fwd(hfg, bf_pad, f_block):
    S = hfg.shape[0]
    tm = _scan_tile(S)

    def body(f_ref, b_ref, c_ref, carry_ref):
        @pl.when(pl.program_id(0) == 0)
        def _():
            carry_ref[...] = jnp.zeros_like(carry_ref)

        v = f_ref[...] + b_ref[...]
        logf = jnp.minimum(v, 0.0) - jnp.log(1.0 + jnp.exp(-jnp.abs(v)))
        tri = (lax.broadcasted_iota(jnp.int32, (tm, tm), 1) <= lax.broadcasted_iota(jnp.int32, (tm, tm), 0)).astype(F32)
        c = jnp.dot(tri, logf, preferred_element_type=F32, precision=lax.Precision.HIGHEST) + carry_ref[...]
        c_ref[...] = c
        carry_ref[...] = c[tm - 1:tm, :]

    return pl.pallas_call(
        body, name="scan_fwd", grid=(S // tm,),
        in_specs=[pl.BlockSpec((tm, LANES), lambda i: (i, f_block)), pl.BlockSpec((1, LANES), lambda i: (0, 0))],
        out_specs=pl.BlockSpec((tm, LANES), lambda i: (i, 0)),
        out_shape=jax.ShapeDtypeStruct((S, LANES), F32),
        scratch_shapes=[pltpu.VMEM((1, LANES), F32)], compiler_params=_params(),
    )(hfg, bf_pad)


def _scan_bwd(dc, hfg, bf_pad, f_block, n_heads):
    S = hfg.shape[0]
    tm = _scan_tile(S)
    nt = S // tm

    def body(dc_ref, f_ref, b_ref, df_ref, db_ref, carry_ref):
        @pl.when(pl.program_id(0) == 0)
        def _():
            carry_ref[...] = jnp.zeros_like(carry_ref)
            db_ref[...] = jnp.zeros_like(db_ref)

        tri = (lax.broadcasted_iota(jnp.int32, (tm, tm), 1) >= lax.broadcasted_iota(jnp.int32, (tm, tm), 0)).astype(F32)
        dlogf = jnp.dot(tri, dc_ref[...], preferred_element_type=F32, precision=lax.Precision.HIGHEST) + carry_ref[...]
        carry_ref[...] = dlogf[0:1, :]
        v = f_ref[...] + b_ref[...]
        lane = lax.broadcasted_iota(jnp.int32, (tm, LANES), 1)
        df = jnp.where(lane < n_heads, dlogf * _sigmoid(-v), 0.0)
        df_ref[...] = df.astype(BF16)
        db_ref[...] += jnp.sum(df, axis=0, keepdims=True)

    return pl.pallas_call(
        body, name="scan_bwd", grid=(nt,),
        in_specs=[pl.BlockSpec((tm, LANES), lambda i: (nt - 1 - i, 0)),
                  pl.BlockSpec((tm, LANES), lambda i: (nt - 1 - i, f_block)),
                  pl.BlockSpec((1, LANES), lambda i: (0, 0))],
        out_specs=[pl.BlockSpec((tm, LANES), lambda i: (nt - 1 - i, 0)), pl.BlockSpec((1, LANES), lambda i: (0, 0))],
        out_shape=[jax.ShapeDtypeStruct((S, LANES), BF16), jax.ShapeDtypeStruct((1, LANES), F32)],
        scratch_shapes=[pltpu.VMEM((1, LANES), F32)], compiler_params=_params(),
    )(dc, hfg, bf_pad)


FOX_AUG = 2 * HEAD_DIM
FOX_V_ROWS = HEAD_DIM + 16
Q_ONES, K_ONES = HEAD_DIM, HEAD_DIM + 3
LOG2E = 1.4426950408889634


def _fox_prep(h, c, H):
    S = h.shape[0]
    W = H * HEAD_DIM
    tm = _rows(S, 3 * W * 2 + LANES * 4 + 2 * H * FOX_AUG * 2 + H * FOX_V_ROWS * 2, 128)
    inv_scale = HEAD_DIM ** 0.5

    def body(q_ref, k_ref, v_ref, c_ref, qa_ref, ka_ref, vt_ref):
        lane = lax.broadcasted_iota(jnp.int32, (tm, HEAD_DIM), 1)
        a = c_ref[...] * inv_scale
        for hh in range(H):
            vt_ref[hh * FOX_V_ROWS:hh * FOX_V_ROWS + HEAD_DIM, :] = (
                v_ref[:, hh * HEAD_DIM:(hh + 1) * HEAD_DIM].astype(F32).T.astype(BF16))
            vt_ref[hh * FOX_V_ROWS + HEAD_DIM:(hh + 1) * FOX_V_ROWS, :] = jnp.ones((FOX_V_ROWS - HEAD_DIM, tm), BF16)
            col = a[:, hh:hh + 1]
            hi = col.astype(BF16).astype(F32)
            mid = (col - hi).astype(BF16).astype(F32)
            lo = col - hi - mid
            piece = jnp.where(lane % 3 == 0, hi, jnp.where(lane % 3 == 1, mid, lo))
            extra_q = jnp.where(lane < 3, 1.0, jnp.where(lane < 6, piece, 0.0))
            extra_k = jnp.where(lane < 3, -piece, jnp.where(lane < 6, 1.0, 0.0))
            qa_ref[:, hh * FOX_AUG:hh * FOX_AUG + HEAD_DIM] = q_ref[:, hh * HEAD_DIM:(hh + 1) * HEAD_DIM]
            qa_ref[:, hh * FOX_AUG + HEAD_DIM:(hh + 1) * FOX_AUG] = extra_q.astype(BF16)
            ka_ref[:, hh * FOX_AUG:hh * FOX_AUG + HEAD_DIM] = k_ref[:, hh * HEAD_DIM:(hh + 1) * HEAD_DIM]
            ka_ref[:, hh * FOX_AUG + HEAD_DIM:(hh + 1) * FOX_AUG] = extra_k.astype(BF16)

    aug = pl.BlockSpec((tm, H * FOX_AUG), lambda i: (i, 0))
    return pl.pallas_call(
        body, name="fox_prep", grid=(S // tm,),
        in_specs=[pl.BlockSpec((tm, W), lambda i: (i, 4)), pl.BlockSpec((tm, W), lambda i: (i, 5)),
                  pl.BlockSpec((tm, W), lambda i: (i, 6)), pl.BlockSpec((tm, LANES), lambda i: (i, 0))],
        out_specs=[aug, aug, pl.BlockSpec((H * FOX_V_ROWS, tm), lambda i: (0, i))],
        out_shape=[jax.ShapeDtypeStruct((S, H * FOX_AUG), BF16)] * 2 + [jax.ShapeDtypeStruct((H * FOX_V_ROWS, S), BF16)],
        compiler_params=_params(),
    )(h, h, h, c)


def _causal_t(T):
    return lax.broadcasted_iota(jnp.int32, (T, T), 0) <= lax.broadcasted_iota(jnp.int32, (T, T), 1)


def _causal_pairs(n, by_query):
    if by_query:
        pairs = [(i, j) for i in range(n) for j in range(i + 1)]
    else:
        pairs = [(i, j) for j in range(n) for i in range(j, n)]
    return jnp.array([p[0] for p in pairs], jnp.int32), jnp.array([p[1] for p in pairs], jnp.int32)


def _fox_fwd(qa, ka, vt, h, H):
    S = h.shape[0]
    W = H * HEAD_DIM
    T = _tile(S, FOX_TILE, 128)
    nq = S // T
    k1 = HEAD_DIM ** -0.5 * LOG2E
    nt = (((1,), (1,)), ((), ()))

    qi, kj = _causal_pairs(nq, True)

    def body(qi_ref, kj_ref, q_ref, k_ref, vt_ref, z_ref, o_ref, g_ref, lse_ref, m_ref, acc_ref):
        i, j = qi_ref[pl.program_id(1)], kj_ref[pl.program_id(1)]

        @pl.when(j == 0)
        def _():
            m_ref[...] = jnp.full_like(m_ref, NEG)
            acc_ref[...] = jnp.zeros_like(acc_ref)

        def step(diag):
            raw = lax.dot_general(k_ref[...], q_ref[...], nt, preferred_element_type=F32)
            if diag:
                raw = jnp.where(_causal_t(T), raw, NEG)
            m_new = jnp.maximum(m_ref[...], jnp.max(raw, axis=0, keepdims=True))
            a = jnp.exp2((m_ref[...] - m_new) * k1)
            p = jnp.exp2((raw - m_new) * k1).astype(BF16)
            acc_ref[...] = a * acc_ref[...] + jnp.dot(vt_ref[...], p, preferred_element_type=F32)
            m_ref[...] = m_new

        @pl.when(j < i)
        def _():
            step(False)

        @pl.when(j == i)
        def _():
            step(True)

            acc = acc_ref[...]
            den = acc[HEAD_DIM:HEAD_DIM + 1, :]
            out = (acc[:HEAD_DIM, :] / den).T
            z = z_ref[...].astype(F32)
            o_ref[...] = out.astype(BF16)
            g_ref[...] = (out * z * _sigmoid(z)).astype(BF16)
            lse_ref[...] = m_ref[...] * k1 + jnp.log(den) * LOG2E

    out = pl.BlockSpec((T, HEAD_DIM), lambda hh, p, qi, kj: (qi[p], hh))
    grid_spec = pltpu.PrefetchScalarGridSpec(
        num_scalar_prefetch=2, grid=(H, qi.shape[0]),
        in_specs=[pl.BlockSpec((T, FOX_AUG), lambda hh, p, qi, kj: (qi[p], hh)),
                  pl.BlockSpec((T, FOX_AUG), lambda hh, p, qi, kj: (kj[p], hh)),
                  pl.BlockSpec((FOX_V_ROWS, T), lambda hh, p, qi, kj: (hh, kj[p])),
                  pl.BlockSpec((T, HEAD_DIM), lambda hh, p, qi, kj: (qi[p], 7 * H + hh))],
        out_specs=[out, out, pl.BlockSpec((None, 1, T), lambda hh, p, qi, kj: (hh, 0, qi[p]))],
        scratch_shapes=[pltpu.VMEM((1, T), F32), pltpu.VMEM((FOX_V_ROWS, T), F32)])
    return pl.pallas_call(
        body, name="fox_fwd", grid_spec=grid_spec,
        out_shape=[jax.ShapeDtypeStruct((S, W), BF16), jax.ShapeDtypeStruct((S, W), BF16),
                   jax.ShapeDtypeStruct((H, 1, S), F32)],
        compiler_params=_params(),
    )(qi, kj, qa, ka, vt, h)


def _fox_bwd(qa, ka, h, do, lse, dl, H):
    S = h.shape[0]
    W = H * HEAD_DIM
    T = _tile(S, FOX_TILE, 128)
    nq = S // T
    k1 = HEAD_DIM ** -0.5 * LOG2E
    nt = (((1,), (1,)), ((), ()))
    tn = (((0,), (0,)), ((), ()))

    qi, kj = _causal_pairs(nq, False)

    def body(qi_ref, kj_ref, q_ref, k_ref, v_ref, do_ref, lse_ref, dl_ref, dq_ref, dk_ref, dv_ref, ak_ref, av_ref):
        i, j = qi_ref[pl.program_id(1)], kj_ref[pl.program_id(1)]

        @pl.when(pl.program_id(1) == 0)
        def _():
            dq_ref[...] = jnp.zeros_like(dq_ref)

        @pl.when(i == j)
        def _():
            ak_ref[...] = jnp.zeros_like(ak_ref)
            av_ref[...] = jnp.zeros_like(av_ref)

        def step(diag):
            q, k, v, d_o = q_ref[...], k_ref[...], v_ref[...], do_ref[...]
            raw = lax.dot_general(k, q, nt, preferred_element_type=F32)
            if diag:
                raw = jnp.where(_causal_t(T), raw, NEG)
            p = jnp.exp2(raw * k1 - lse_ref[...])
            dp = lax.dot_general(v, d_o, nt, preferred_element_type=F32)
            ds = (p * (dp - dl_ref[...])).astype(BF16)
            av_ref[...] += jnp.dot(p.astype(BF16), d_o, preferred_element_type=F32)
            ak_ref[...] += jnp.dot(ds, q, preferred_element_type=F32)
            rows = pl.ds(pl.multiple_of(i * T, T), T)
            dq_ref[rows, :] += lax.dot_general(ds, k, tn, preferred_element_type=F32)

        @pl.when(i > j)
        def _():
            step(False)

        @pl.when(i == j)
        def _():
            step(True)

        @pl.when(i == nq - 1)
        def _():
            dk_ref[...] = ak_ref[...]
            dv_ref[...] = av_ref[...]

    qrow = lambda hh, p, qi, kj: (qi[p], hh)
    krow = lambda hh, p, qi, kj: (kj[p], hh)
    stat = pl.BlockSpec((None, 1, T), lambda hh, p, qi, kj: (hh, 0, qi[p]))
    grid_spec = pltpu.PrefetchScalarGridSpec(
        num_scalar_prefetch=2, grid=(H, qi.shape[0]),
        in_specs=[pl.BlockSpec((T, FOX_AUG), qrow), pl.BlockSpec((T, FOX_AUG), krow),
                  pl.BlockSpec((T, HEAD_DIM), lambda hh, p, qi, kj: (kj[p], 6 * H + hh)),
                  pl.BlockSpec((T, HEAD_DIM), qrow), stat, stat],
        out_specs=[pl.BlockSpec((S, FOX_AUG), lambda hh, p, qi, kj: (0, hh)), pl.BlockSpec((T, FOX_AUG), krow),
                   pl.BlockSpec((T, HEAD_DIM), krow)],
        scratch_shapes=[pltpu.VMEM((T, FOX_AUG), F32), pltpu.VMEM((T, HEAD_DIM), F32)])
    return pl.pallas_call(
        body, name="fox_bwd", grid_spec=grid_spec,
        out_shape=[jax.ShapeDtypeStruct((S, H * FOX_AUG), F32), jax.ShapeDtypeStruct((S, H * FOX_AUG), F32),
                   jax.ShapeDtypeStruct((S, W), F32)],
        compiler_params=_params(),
    )(qi, kj, qa, ka, h, do, lse, dl)


def _adamw_math(w, g, m, v):
    m = ADAM_B1 * m + (1.0 - ADAM_B1) * g
    v = ADAM_B2 * v + (1.0 - ADAM_B2) * (g * g)
    m_hat = m / (1.0 - ADAM_B1 ** ADAM_STEP)
    v_hat = v / (1.0 - ADAM_B2 ** ADAM_STEP)
    delta = -ADAM_LR * (m_hat / (jnp.sqrt(v_hat) + ADAM_EPS) + ADAM_WD * w)
    return delta, m, v


def _adamw(w, g_mine, g_theirs, m, v, name):
    L, R, C = w.shape
    half = L // 2
    tr = _rows(R, C * 4 * 9, 8)

    def body(c_ref, w_ref, gm_ref, gt_ref, m_ref, v_ref, g_ref, d_ref, nm_ref, nv_ref):
        g = jnp.where(pl.program_id(0) // half == c_ref[0], gm_ref[...], gt_ref[...])
        g_ref[...] = g
        d_ref[...], nm_ref[...], nv_ref[...] = _adamw_math(w_ref[...], g, m_ref[...], v_ref[...])

    blk = pl.BlockSpec((None, tr, C), lambda l, i, c: (l, i, 0))
    mine = pl.BlockSpec((None, tr, C), lambda l, i, c: (jnp.clip(l - c[0] * half, 0, half - 1), i, 0))
    theirs = pl.BlockSpec((None, tr, C), lambda l, i, c: (jnp.clip(l - (1 - c[0]) * half, 0, half - 1), i, 0))
    grid_spec = pltpu.PrefetchScalarGridSpec(
        num_scalar_prefetch=1, grid=(L, R // tr), in_specs=[blk, mine, theirs, blk, blk], out_specs=[blk] * 4)
    core = lax.axis_index("c").astype(jnp.int32).reshape(1)
    return pl.pallas_call(
        body, name=name, grid_spec=grid_spec, out_shape=[jax.ShapeDtypeStruct((L, R, C), F32)] * 4,
        compiler_params=_params(),
    )(core, w, g_mine, g_theirs, m, v)


def _place():
    x, y, c = lax.axis_index("x"), lax.axis_index("y"), lax.axis_index("c")
    return x, y, c, [(1 - x, y), (x, 1 - y), (1 - x, 1 - y)]


def _remote(src, dst, send_sems, recv_sems, k, to):
    return pltpu.make_async_remote_copy(src_ref=src, dst_ref=dst, send_sem=send_sems.at[k], recv_sem=recv_sems.at[k],
                                        device_id=to, device_id_type=MESH)


def _gather_weights(shards):
    n = len(shards)
    half = shards[0].shape[0] // 2

    def body(*refs):
        srcs, dsts = refs[:n], refs[n:2 * n]
        send_sems, recv_sems = refs[2 * n:]
        x, y, c, chips = _place()
        me = 2 * x + y
        mine, theirs = pl.ds(c * half, half), pl.ds((1 - c) * half, half)
        first = [_remote(srcs[a].at[mine], dsts[a].at[me, mine], send_sems, recv_sems, 6 * a + j, (px, py, c))
                 for a in range(n) for j, (px, py) in enumerate(chips)]
        for cp in first:
            cp.start()
        passed = []
        for a in range(n):
            for j, (px, py) in enumerate(chips):
                landed = dsts[a].at[2 * px + py, mine]
                _remote(landed, landed, send_sems, recv_sems, 6 * a + j, (px, py, c)).wait_recv()
                cp = _remote(landed, landed, send_sems, recv_sems, 6 * a + 3 + j, (x, y, 1 - c))
                cp.start()
                passed.append(cp)
        for a in range(n):
            for j, (px, py) in enumerate(chips):
                landed = dsts[a].at[2 * px + py, theirs]
                _remote(landed, landed, send_sems, recv_sems, 6 * a + 3 + j, (x, y, 1 - c)).wait_recv()
        for cp in first + passed:
            cp.wait_send()

    return pl.pallas_call(
        body, name="gather_weights", in_specs=[ANY] * n, out_specs=[ANY] * n,
        out_shape=[jax.ShapeDtypeStruct((4,) + s.shape, s.dtype) for s in shards],
        scratch_shapes=[pltpu.SemaphoreType.DMA((6 * n,)), pltpu.SemaphoreType.DMA((6 * n,))],
    )(*shards)


def _swap_other_half(parts):
    n = len(parts)
    half = parts[0].shape[1] // 2

    def body(*refs):
        srcs, dsts = refs[:n], refs[n:2 * n]
        send_sems, recv_sems = refs[2 * n:]
        x, y, c, _ = _place()
        cps = [_remote(srcs[a].at[:, pl.ds((1 - c) * half, half)], dsts[a], send_sems, recv_sems, a, (x, y, 1 - c))
               for a in range(n)]
        for cp in cps:
            cp.start()
        for cp in cps:
            cp.wait()

    return pl.pallas_call(
        body, name="grad_swap_half", in_specs=[ANY] * n, out_specs=[ANY] * n,
        out_shape=[jax.ShapeDtypeStruct((4, half) + p.shape[2:], p.dtype) for p in parts],
        scratch_shapes=[pltpu.SemaphoreType.DMA((n,)), pltpu.SemaphoreType.DMA((n,))],
    )(*parts)


def _add_half(part, got, name):
    _, half, R, C = got.shape
    tr = _rows(R, C * 2 * 3)

    def body(c_ref, p_ref, g_ref, o_ref):
        o_ref[...] = (p_ref[...].astype(F32) + g_ref[...].astype(F32)).astype(BF16)

    grid_spec = pltpu.PrefetchScalarGridSpec(
        num_scalar_prefetch=1, grid=(4, half, R // tr),
        in_specs=[pl.BlockSpec((None, None, tr, C), lambda s, l, i, c: (s, c[0] * half + l, i, 0)),
                  pl.BlockSpec((None, None, tr, C), lambda s, l, i, c: (s, l, i, 0))],
        out_specs=pl.BlockSpec((None, None, tr, C), lambda s, l, i, c: (s, l, i, 0)))
    core = lax.axis_index("c").astype(jnp.int32).reshape(1)
    return pl.pallas_call(
        body, name=name, grid_spec=grid_spec, out_shape=jax.ShapeDtypeStruct(got.shape, BF16),
        compiler_params=_params(),
    )(core, part, got)


def _scatter_to_owner(parts):
    n = len(parts)

    def body(*refs):
        srcs, dsts = refs[:n], refs[n:2 * n]
        send_sems, recv_sems, local_sems = refs[2 * n:]
        x, y, c, chips = _place()
        me = 2 * x + y
        local = [pltpu.make_async_copy(srcs[a].at[me], dsts[a].at[me], local_sems.at[a]) for a in range(n)]
        for cp in local:
            cp.start()
        sends = [_remote(srcs[a].at[2 * px + py], dsts[a].at[me], send_sems, recv_sems, 3 * a + j, (px, py, c))
                 for a in range(n) for j, (px, py) in enumerate(chips)]
        for cp in sends:
            cp.start()
        for a in range(n):
            for j, (px, py) in enumerate(chips):
                slot = dsts[a].at[2 * px + py]
                _remote(slot, slot, send_sems, recv_sems, 3 * a + j, (px, py, c)).wait_recv()
        for cp in sends:
            cp.wait_send()
        for cp in local:
            cp.wait()

    return pl.pallas_call(
        body, name="grad_scatter", in_specs=[ANY] * n, out_specs=[ANY] * n,
        out_shape=[jax.ShapeDtypeStruct(p.shape, p.dtype) for p in parts],
        scratch_shapes=[pltpu.SemaphoreType.DMA((3 * n,)), pltpu.SemaphoreType.DMA((3 * n,)),
                        pltpu.SemaphoreType.DMA((n,))],
    )(*parts)


def _sum_chips(got, name):
    _, half, R, C = got.shape
    tr = _rows(R, C * (2 * 4 + 4))

    def body(g_ref, o_ref):
        o_ref[...] = ((g_ref[0].astype(F32) + g_ref[1].astype(F32)) + g_ref[2].astype(F32)) + g_ref[3].astype(F32)

    return pl.pallas_call(
        body, name=name, grid=(half, R // tr),
        in_specs=[pl.BlockSpec((4, None, tr, C), lambda l, i: (0, l, i, 0))],
        out_specs=pl.BlockSpec((None, tr, C), lambda l, i: (l, i, 0)),
        out_shape=jax.ShapeDtypeStruct((half, R, C), F32), compiler_params=_params(),
    )(got)


def _share_halves(halves):
    n = len(halves)

    def body(*refs):
        srcs, dsts = refs[:n], refs[n:2 * n]
        send_sems, recv_sems = refs[2 * n:]
        x, y, c, _ = _place()
        cps = [_remote(srcs[a], dsts[a], send_sems, recv_sems, a, (x, y, 1 - c)) for a in range(n)]
        for cp in cps:
            cp.start()
        for cp in cps:
            cp.wait()

    return pl.pallas_call(
        body, name="grad_share_halves", in_specs=[ANY] * n, out_specs=[ANY] * n,
        out_shape=[jax.ShapeDtypeStruct(h.shape, h.dtype) for h in halves],
        scratch_shapes=[pltpu.SemaphoreType.DMA((n,)), pltpu.SemaphoreType.DMA((n,))],
    )(*halves)


def _small_allreduce_adamw(part, w, m, v):
    R = part.shape[0]
    deltas = [(dx, dy, dc) for dx in (0, 1) for dy in (0, 1) for dc in (0, 1)][1:]

    def body(p_ref, w_ref, m_ref, v_ref, g_ref, d_ref, nm_ref, nv_ref, all_ref, send_sems, recv_sems):
        x, y, c, _ = _place()
        me = 4 * x + 2 * y + c
        all_ref[me] = p_ref[...]
        cps = [_remote(p_ref, all_ref.at[me], send_sems, recv_sems, k, (x ^ dx, y ^ dy, c ^ dc))
               for k, (dx, dy, dc) in enumerate(deltas)]
        for cp in cps:
            cp.start()
        for k, (dx, dy, dc) in enumerate(deltas):
            slot = all_ref.at[4 * (x ^ dx) + 2 * (y ^ dy) + (c ^ dc)]
            _remote(slot, slot, send_sems, recv_sems, k, (x ^ dx, y ^ dy, c ^ dc)).wait_recv()
        for cp in cps:
            cp.wait_send()
        g = all_ref[0]
        for k in range(1, 8):
            g = g + all_ref[k]
        g_ref[...] = g
        d_ref[...], nm_ref[...], nv_ref[...] = _adamw_math(w_ref[...], g, m_ref[...], v_ref[...])

    vm = pl.BlockSpec(memory_space=pltpu.VMEM)
    shape = jax.ShapeDtypeStruct((R, LANES), F32)
    return pl.pallas_call(
        body, name="small_allreduce_adamw", in_specs=[vm] * 4, out_specs=[vm] * 4, out_shape=[shape] * 4,
        scratch_shapes=[pltpu.VMEM((8, R, LANES), F32), pltpu.SemaphoreType.DMA((7,)), pltpu.SemaphoreType.DMA((7,))],
    )(part, w, m, v)


def _pack_small(bf, bg, lg, lb, extra=None):
    L, H = bf.shape
    per = jnp.concatenate([jnp.pad(bf, ((0, 0), (0, LANES - H))), bg, lg, lb], axis=1)
    flat = per.reshape(-1, LANES)
    last = jnp.zeros((8 + (-flat.shape[0]) % 8, LANES), F32)
    if extra is not None:
        last = last.at[-8, 0].set(extra)
    return jnp.concatenate([flat, last], axis=0)


def _unpack_small(p, L, H, D):
    per = p[:L * (1 + 4 * D // LANES)].reshape(L, -1)
    return per[:, :H], per[:, LANES:LANES + 2 * D], per[:, LANES + 2 * D:LANES + 3 * D], per[:, LANES + 3 * D:]


def kernel(x, w_in, b_forget, b_gate, w_up_a, w_up_b, w_out, ln_g, ln_b, loss_target, m_w_in, m_b_forget, m_b_gate, m_w_up_a, m_w_up_b, m_w_out, m_ln_g, m_ln_b, v_w_in, v_b_forget, v_b_gate, v_w_up_a, v_w_up_b, v_w_out, v_ln_g, v_ln_b):
    _, S, D = x.shape
    L, _, C4 = w_in.shape
    H = b_forget.shape[1]
    W = w_up_a.shape[1]
    D4 = D // 4
    NC = 4 * C4
    assert W == H * HEAD_DIM and NC == 8 * W + H + 2 * D and L % 2 == 0 and D % LANES == 0
    alpha = float((2 * L) ** 0.25)
    f_block = 2 * D // LANES

    own = [w_in.astype(BF16), jnp.concatenate([w_up_a, w_up_b], axis=2).astype(BF16), w_out.astype(BF16)]
    gathered = _gather_weights(own)
    me = 2 * lax.axis_index("x") + lax.axis_index("y")
    shard = lambda a, s, l: jnp.where(me == s, own[a][l], gathered[a][s, l])
    w_main, w_fg, w_ua, w_ub, w_o = [], [], [], [], []
    for l in range(L):
        full = jnp.concatenate([shard(0, s, l) for s in range(4)], axis=1)
        w_main.append(full[:, :8 * W])
        w_fg.append(jnp.concatenate([full[:, 8 * W + H:], full[:, 8 * W:8 * W + H],
                                     jnp.zeros((D, LANES - H), BF16)], axis=1))
        ups = [shard(1, s, l) for s in range(4)]
        w_ua.append(jnp.concatenate([u[:, :D4] for u in ups], axis=1))
        w_ub.append(jnp.concatenate([u[:, D4:] for u in ups], axis=1))
        w_o.append(jnp.concatenate([shard(2, s, l) for s in range(4)], axis=0))

    pos = jnp.arange(S, dtype=F32)
    inv_freq = ROPE_THETA ** (-jnp.arange(HEAD_DIM // 2, dtype=F32) / (HEAD_DIM // 2))
    ang = pos[:, None] * inv_freq[None, :]
    cos = jnp.concatenate([jnp.cos(ang), jnp.cos(ang)], axis=1)
    sin = jnp.concatenate([-jnp.sin(ang), jnp.sin(ang)], axis=1)
    bf_pad = jnp.pad(b_forget, ((0, 0), (0, LANES - H)))

    xs = x[0]
    xb = xs.astype(BF16)
    saved = []
    for l in range(L):
        h = _matmul(xb, w_main[l], mode="nn", out_dtype=BF16, name="in_proj", rope=(cos, sin), rope_cols=2 * W)
        hfg = _matmul(xb, w_fg[l], mode="nn", out_dtype=F32, name="in_proj_gates")
        qkv_a = h[:, :3 * W]
        views = [qkv_a.reshape(S // d, d * 3 * W) for _, d in DILATED_PATTERNS]
        os, lses = [], []
        for (_, d), hv in zip(DILATED_PATTERNS, views):
            o, lse = _dil_fwd(hv, d, W, f"dil_fwd_d{d}")
            os.append(o.reshape(S, W))
            lses.append(lse.transpose(1, 0, 2).reshape(S, H))
        out_a, lse_a, ga = _dil_combine_fwd(os, lses, h, W)
        c = _scan_fwd(hfg, bf_pad[l:l + 1], f_block)
        qa, ka, vt = _fox_prep(h, c, H)
        out_b, gb, lse_b = _fox_fwd(qa, ka, vt, h, H)
        up_a = _matmul(ga, w_ua[l], mode="nn", out_dtype=BF16, name="up_proj")
        up_b = _matmul(gb, w_ub[l], mode="nn", out_dtype=BF16, name="up_proj")
        u = _merge_fwd(up_a, up_b, hfg, b_gate[l:l + 1])
        r = _matmul(u, w_o[l], mode="nn", out_dtype=F32, name="out_proj", acc_in=xs, acc_scale=alpha)
        saved.append((xb, h, hfg, views, out_a, lse_a, ga, qa, ka, out_b, gb, lse_b, up_a, up_b, u, r))
        xs, xb = _ln_fwd(r, ln_g[l:l + 1], ln_b[l:l + 1])

    dx, sq = _loss(xs, loss_target[0])
    loss_part = 0.5 * jnp.sum(sq) / D

    g_in, g_up, g_out, g_bf, g_bg, g_lg, g_lb = [], [], [], [], [], [], []
    for l in reversed(range(L)):
        xb, h, hfg, views, out_a, lse_a, ga, qa, ka, out_b, gb, lse_b, up_a, up_b, u, r = saved[l]
        dr, adr, dlg, dlb = _ln_bwd(dx, r, ln_g[l:l + 1], alpha)
        du = _matmul(dr, w_o[l], mode="nt", out_dtype=F32, name="out_proj_dx")
        dwo = _matmul(u, dr, mode="tn", out_dtype=F32, name="out_proj_dw")
        dua, dub, dgl, dbg = _merge_bwd(du, up_a, up_b, hfg, b_gate[l:l + 1])
        dga = _matmul(dua, w_ua[l], mode="nt", out_dtype=F32, name="up_proj_dx")
        dgb = _matmul(dub, w_ub[l], mode="nt", out_dtype=F32, name="up_proj_dx")
        dwua = _matmul(ga, dua, mode="tn", out_dtype=F32, name="up_proj_dw")
        dwub = _matmul(gb, dub, mode="tn", out_dtype=F32, name="up_proj_dw")
        do_a, dz_a, dl_a = _gate_bwd(dga, out_a, h, 3, "gate_bwd_a")
        dqs, dks, dvs = [], [], []
        for (_, d), hv in zip(DILATED_PATTERNS, views):
            stat = lambda t: t.reshape(S // d, d, H).transpose(1, 0, 2)
            dq, dk, dv = _dil_bwd(hv, do_a.reshape(S // d, d * W), stat(lse_a), stat(dl_a), d, W, f"dil_bwd_d{d}")
            dqs.append(dq.reshape(S, W))
            dks.append(dk.reshape(S, W))
            dvs.append(dv.reshape(S, W))
        dqkv_a = _dil_combine_bwd(dqs, dks, dvs, cos, sin)
        do_b, dz_b, dl_b = _gate_bwd(dgb, out_b, h, 7, "gate_bwd_b")
        dq_f, dk_f, dv_b = _fox_bwd(qa, ka, h, do_b, lse_b, dl_b.T.reshape(H, 1, S), H)
        dq_f, dk_f = dq_f.reshape(S, H, FOX_AUG), dk_f.reshape(S, H, FOX_AUG)
        dc = jnp.pad(dq_f[:, :, K_ONES] - dk_f[:, :, Q_ONES], ((0, 0), (0, LANES - H)))
        df, dbf = _scan_bwd(dc, hfg, bf_pad[l:l + 1], f_block, H)
        att_scale = HEAD_DIM ** -0.5
        dq_b = (dq_f[:, :, :HEAD_DIM] * att_scale).astype(BF16).reshape(S, W)
        dk_b = (dk_f[:, :, :HEAD_DIM] * att_scale).astype(BF16).reshape(S, W)
        dh = jnp.concatenate([dqkv_a, dz_a, dq_b, dk_b, dv_b.astype(BF16), dz_b], axis=1)
        dhfg = jnp.concatenate([dgl, df], axis=1)
        dx1 = _matmul(dh, w_main[l], mode="nt", out_dtype=F32, name="in_proj_dx", acc_in=adr)
        dx = _matmul(dhfg, w_fg[l], mode="nt", out_dtype=F32, name="in_proj_gates_dx", acc_in=dx1)
        dwm = _matmul(xb, dh, mode="tn", out_dtype=F32, name="in_proj_dw")
        dwfg = _matmul(xb, dhfg, mode="tn", out_dtype=F32, name="in_proj_gates_dw")
        full = jnp.concatenate([dwm, dwfg[:, 2 * D:2 * D + H], dwfg[:, :2 * D]], axis=1)
        g_in.append(full.reshape(D, 4, C4).transpose(1, 0, 2).astype(BF16))
        g_up.append(jnp.concatenate([dwua.reshape(W, 4, D4), dwub.reshape(W, 4, D4)], axis=2).transpose(1, 0, 2).astype(BF16))
        g_out.append(dwo.reshape(4, D4, D).astype(BF16))
        g_bf.append(dbf[0, :H])
        g_bg.append(dbg[0])
        g_lg.append(dlg[0])
        g_lb.append(dlb[0])
    grad_x = dx[None]
    for lst in (g_in, g_up, g_out, g_bf, g_bg, g_lg, g_lb):
        lst.reverse()

    parts = [jnp.stack(g_in, axis=1), jnp.stack(g_up, axis=1), jnp.stack(g_out, axis=1)]
    names = ["w_in", "w_up", "w_out"]
    got = _swap_other_half(parts)
    chip = [_add_half(p, g, f"grad_add_half_{n}") for p, g, n in zip(parts, got, names)]
    landed = _scatter_to_owner(chip)
    halves = [_sum_chips(g, f"grad_sum_chips_{n}") for g, n in zip(landed, names)]
    theirs = _share_halves(halves)

    pair = lambda a, b: jnp.concatenate([a, b], axis=2)
    grad_w_in, d_in, nm_in, nv_in = _adamw(w_in, halves[0], theirs[0], m_w_in, v_w_in, "adamw_w_in")
    up = _adamw(pair(w_up_a, w_up_b), halves[1], theirs[1], pair(m_w_up_a, m_w_up_b), pair(v_w_up_a, v_w_up_b),
                "adamw_w_up")
    (grad_w_up_a, grad_w_up_b), (d_ua, d_ub), (nm_ua, nm_ub), (nv_ua, nv_ub) = [
        (t[:, :, :D4], t[:, :, D4:]) for t in up]
    grad_w_out, d_o, nm_o, nv_o = _adamw(w_out, halves[2], theirs[2], m_w_out, v_w_out, "adamw_w_out")

    small_g = _pack_small(jnp.stack(g_bf), jnp.stack(g_bg), jnp.stack(g_lg), jnp.stack(g_lb), loss_part)
    small = _small_allreduce_adamw(small_g, _pack_small(b_forget, b_gate, ln_g, ln_b),
                                   _pack_small(m_b_forget, m_b_gate, m_ln_g, m_ln_b),
                                   _pack_small(v_b_forget, v_b_gate, v_ln_g, v_ln_b))
    loss = small[0][-8, 0]
    (g_bf, g_bg, g_lg, g_lb), (d_bf, d_bg, d_lg, d_lb), (nm_bf, nm_bg, nm_lg, nm_lb), (nv_bf, nv_bg, nv_lg, nv_lb) = [
        _unpack_small(p, L, H, D) for p in small]

    return (loss, grad_x,
            grad_w_in, g_bf, g_bg, grad_w_up_a, grad_w_up_b, grad_w_out, g_lg, g_lb,
            d_in, d_bf, d_bg, d_ua, d_ub, d_o, d_lg, d_lb,
            nm_in, nm_bf, nm_bg, nm_ua, nm_ub, nm_o, nm_lg, nm_lb,
            nv_in, nv_bf, nv_bg, nv_ua, nv_ub, nv_o, nv_lg, nv_lb)
```

```python
import jax
import jax.numpy as jnp
from jax import lax
from jax.experimental import pallas as pl
from jax.experimental.pallas import tpu as pltpu

F32 = jnp.float32
BF16 = jnp.bfloat16
MESH = pl.DeviceIdType.MESH
ANY = pl.BlockSpec(memory_space=pl.ANY)

HEAD_DIM = 128
LANES = 128
Q_BLOCK = 128
DILATED_PATTERNS = ((128, 1), (512, 4), (2048, 16))
ROPE_THETA = 10000.0
LN_EPS = 1e-5
ADAM_LR, ADAM_B1, ADAM_B2, ADAM_EPS, ADAM_WD, ADAM_STEP = 0.001, 0.9, 0.999, 1e-08, 0.01, 10
NEG = -1e30
VMEM_LIMIT = 56 * 2**20
ELEMENTWISE_BUDGET = 20 * 2**20
FOX_TILE = 1024
MM_TILES = (1024, 1024, 2048)


def _params():
    return pltpu.CompilerParams(vmem_limit_bytes=VMEM_LIMIT)


def _tile(dim, target, align):
    if dim <= target:
        return dim
    t = (target // align) * align
    while t >= align:
        if dim % t == 0:
            return t
        t -= align
    return dim


def _rows(n_rows, bytes_per_row, align=16):
    return _tile(n_rows, max(align, ELEMENTWISE_BUDGET // (2 * bytes_per_row)), align)


def _sigmoid(v):
    return 1.0 / (1.0 + jnp.exp(-v))


def _matmul(a, b, *, mode, out_dtype, name, acc_in=None, acc_scale=1.0, rope=None, rope_cols=0):
    if mode == "nn":
        (M, K), (K2, N) = a.shape, b.shape
    elif mode == "nt":
        (M, K), (N, K2) = a.shape, b.shape
    else:
        (K, M), (K2, N) = a.shape, b.shape
    assert K == K2, (a.shape, b.shape, mode)
    tm, tn, tk = _tile(M, MM_TILES[0], 128), _tile(N, MM_TILES[1], 128), _tile(K, MM_TILES[2], 128)
    if rope is not None:
        assert rope_cols % tn == 0
    nk = K // tk
    n_rope_tiles = rope_cols // tn if rope is not None else 0
    dims = {"nn": (((1,), (0,)), ((), ())), "nt": (((1,), (1,)), ((), ())), "tn": (((0,), (0,)), ((), ()))}[mode]

    def body(*refs):
        a_ref, b_ref = refs[0], refs[1]
        pos = 2
        if rope is not None:
            cos_ref, sin_ref = refs[pos], refs[pos + 1]
            pos += 2
        if acc_in is not None:
            acc_in_ref = refs[pos]
            pos += 1
        o_ref, acc_ref = refs[pos], refs[pos + 1]
        j, k = pl.program_id(1), pl.program_id(2)

        @pl.when(k == 0)
        def _():
            acc_ref[...] = jnp.zeros_like(acc_ref)

        acc_ref[...] += lax.dot_general(a_ref[...], b_ref[...], dims, preferred_element_type=F32)

        def finish(rotate):
            r = acc_ref[...]
            if acc_in is not None:
                r = r + acc_scale * acc_in_ref[...]
            if rotate:
                cos, sin = cos_ref[...], sin_ref[...]
                for g in range(tn // HEAD_DIM):
                    sl = slice(g * HEAD_DIM, (g + 1) * HEAD_DIM)
                    t = r[:, sl]
                    o_ref[:, sl] = (t * cos + pltpu.roll(t, HEAD_DIM // 2, 1) * sin).astype(o_ref.dtype)
            else:
                o_ref[...] = r.astype(o_ref.dtype)

        if n_rope_tiles:
            @pl.when((k == nk - 1) & (j < n_rope_tiles))
            def _():
                finish(True)

            @pl.when((k == nk - 1) & (j >= n_rope_tiles))
            def _():
                finish(False)
        else:
            @pl.when(k == nk - 1)
            def _():
                finish(False)

    if mode == "nn":
        in_specs = [pl.BlockSpec((tm, tk), lambda i, j, k: (i, k)), pl.BlockSpec((tk, tn), lambda i, j, k: (k, j))]
    elif mode == "nt":
        in_specs = [pl.BlockSpec((tm, tk), lambda i, j, k: (i, k)), pl.BlockSpec((tn, tk), lambda i, j, k: (j, k))]
    else:
        in_specs = [pl.BlockSpec((tk, tm), lambda i, j, k: (k, i)), pl.BlockSpec((tk, tn), lambda i, j, k: (k, j))]
    args = [a, b]
    if rope is not None:
        in_specs += [pl.BlockSpec((tm, HEAD_DIM), lambda i, j, k: (i, 0))] * 2
        args += list(rope)
    if acc_in is not None:
        in_specs.append(pl.BlockSpec((tm, tn), lambda i, j, k: (i, j)))
        args.append(acc_in)
    return pl.pallas_call(
        body, name=name, grid=(M // tm, N // tn, nk), in_specs=in_specs,
        out_specs=pl.BlockSpec((tm, tn), lambda i, j, k: (i, j)),
        out_shape=jax.ShapeDtypeStruct((M, N), out_dtype),
        scratch_shapes=[pltpu.VMEM((tm, tn), F32)], compiler_params=_params(),
    )(*args)


def _ln_fwd(r, g, b):
    S, D = r.shape
    tm = _rows(S, D * (4 + 4 + 2))

    def body(r_ref, g_ref, b_ref, x_ref, xb_ref):
        v = r_ref[...]
        mu = jnp.mean(v, axis=1, keepdims=True)
        cen = v - mu
        var = jnp.mean(cen * cen, axis=1, keepdims=True)
        out = cen * lax.rsqrt(var + LN_EPS) * g_ref[...] + b_ref[...]
        x_ref[...] = out
        xb_ref[...] = out.astype(BF16)

    row = pl.BlockSpec((tm, D), lambda i: (i, 0))
    vec = pl.BlockSpec((1, D), lambda i: (0, 0))
    return pl.pallas_call(
        body, name="ln_fwd", grid=(S // tm,), in_specs=[row, vec, vec], out_specs=[row, row],
        out_shape=[jax.ShapeDtypeStruct((S, D), F32), jax.ShapeDtypeStruct((S, D), BF16)],
        compiler_params=_params(),
    )(r, g, b)


def _ln_bwd(dx, r, g, alpha):
    S, D = r.shape
    tm = _rows(S, D * (4 + 4 + 2 + 4))

    def body(dx_ref, r_ref, g_ref, drb_ref, adr_ref, dg_ref, db_ref):
        @pl.when(pl.program_id(0) == 0)
        def _():
            dg_ref[...] = jnp.zeros_like(dg_ref)
            db_ref[...] = jnp.zeros_like(db_ref)

        v, d = r_ref[...], dx_ref[...]
        mu = jnp.mean(v, axis=1, keepdims=True)
        cen = v - mu
        var = jnp.mean(cen * cen, axis=1, keepdims=True)
        rstd = lax.rsqrt(var + LN_EPS)
        xhat = cen * rstd
        dxhat = d * g_ref[...]
        dr = rstd * (dxhat - jnp.mean(dxhat, axis=1, keepdims=True)
                     - xhat * jnp.mean(dxhat * xhat, axis=1, keepdims=True))
        drb_ref[...] = dr.astype(BF16)
        adr_ref[...] = alpha * dr
        dg_ref[...] += jnp.sum(d * xhat, axis=0, keepdims=True)
        db_ref[...] += jnp.sum(d, axis=0, keepdims=True)

    row = pl.BlockSpec((tm, D), lambda i: (i, 0))
    vec = pl.BlockSpec((1, D), lambda i: (0, 0))
    return pl.pallas_call(
        body, name="ln_bwd", grid=(S // tm,), in_specs=[row, row, vec], out_specs=[row, row, vec, vec],
        out_shape=[jax.ShapeDtypeStruct((S, D), BF16), jax.ShapeDtypeStruct((S, D), F32),
                   jax.ShapeDtypeStruct((1, D), F32), jax.ShapeDtypeStruct((1, D), F32)],
        compiler_params=_params(),
    )(dx, r, g)


def _loss(y, target):
    S, D = y.shape
    tm = _rows(S, D * 12)

    def body(y_ref, t_ref, dy_ref, sq_ref):
        @pl.when(pl.program_id(0) == 0)
        def _():
            sq_ref[...] = jnp.zeros_like(sq_ref)

        err = y_ref[...] - t_ref[...]
        dy_ref[...] = err * (1.0 / D)
        sq_ref[...] += jnp.sum(err * err, axis=0, keepdims=True)

    row = pl.BlockSpec((tm, D), lambda i: (i, 0))
    vec = pl.BlockSpec((1, D), lambda i: (0, 0))
    return pl.pallas_call(
        body, name="loss", grid=(S // tm,), in_specs=[row, row], out_specs=[row, vec],
        out_shape=[jax.ShapeDtypeStruct((S, D), F32), jax.ShapeDtypeStruct((1, D), F32)],
        compiler_params=_params(),
    )(y, target)


def _merge_fwd(up_a, up_b, hfg, b_gate):
    S, D = up_a.shape
    tm = _rows(S, D * (2 + 2 + 4 + 4 + 2))

    def body(ua_ref, ub_ref, gla_ref, glb_ref, bga_ref, bgb_ref, u_ref):
        ga = _sigmoid(gla_ref[...] + bga_ref[...])
        gb = _sigmoid(glb_ref[...] + bgb_ref[...])
        u_ref[...] = (ga * ua_ref[...].astype(F32) + gb * ub_ref[...].astype(F32)).astype(BF16)

    row = pl.BlockSpec((tm, D), lambda i: (i, 0))
    row1 = pl.BlockSpec((tm, D), lambda i: (i, 1))
    v0 = pl.BlockSpec((1, D), lambda i: (0, 0))
    v1 = pl.BlockSpec((1, D), lambda i: (0, 1))
    return pl.pallas_call(
        body, name="merge_fwd", grid=(S // tm,), in_specs=[row, row, row, row1, v0, v1], out_specs=row,
        out_shape=jax.ShapeDtypeStruct((S, D), BF16), compiler_params=_params(),
    )(up_a, up_b, hfg, hfg, b_gate, b_gate)


def _merge_bwd(du, up_a, up_b, hfg, b_gate):
    S, D = up_a.shape
    tm = _rows(S, D * (4 + 2 + 2 + 4 + 4 + 2 + 2 + 4))

    def body(du_ref, ua_ref, ub_ref, gla_ref, glb_ref, bga_ref, bgb_ref, dua_ref, dub_ref, dgl_ref, dbg_ref):
        @pl.when(pl.program_id(0) == 0)
        def _():
            dbg_ref[...] = jnp.zeros_like(dbg_ref)

        du = du_ref[...]
        ga = _sigmoid(gla_ref[...] + bga_ref[...])
        gb = _sigmoid(glb_ref[...] + bgb_ref[...])
        dua_ref[...] = (du * ga).astype(BF16)
        dub_ref[...] = (du * gb).astype(BF16)
        dgla = du * ua_ref[...].astype(F32) * ga * (1.0 - ga)
        dglb = du * ub_ref[...].astype(F32) * gb * (1.0 - gb)
        dgl_ref[:, :D] = dgla.astype(BF16)
        dgl_ref[:, D:] = dglb.astype(BF16)
        dbg_ref[:, :D] += jnp.sum(dgla, axis=0, keepdims=True)
        dbg_ref[:, D:] += jnp.sum(dglb, axis=0, keepdims=True)

    row = pl.BlockSpec((tm, D), lambda i: (i, 0))
    row1 = pl.BlockSpec((tm, D), lambda i: (i, 1))
    v0 = pl.BlockSpec((1, D), lambda i: (0, 0))
    v1 = pl.BlockSpec((1, D), lambda i: (0, 1))
    return pl.pallas_call(
        body, name="merge_bwd", grid=(S // tm,), in_specs=[row, row, row, row, row1, v0, v1],
        out_specs=[row, row, pl.BlockSpec((tm, 2 * D), lambda i: (i, 0)), pl.BlockSpec((1, 2 * D), lambda i: (0, 0))],
        out_shape=[jax.ShapeDtypeStruct((S, D), BF16), jax.ShapeDtypeStruct((S, D), BF16),
                   jax.ShapeDtypeStruct((S, 2 * D), BF16), jax.ShapeDtypeStruct((1, 2 * D), F32)],
        compiler_params=_params(),
    )(du, up_a, up_b, hfg, hfg, b_gate, b_gate)


def _gate_bwd(dg, out, h, z_block, name):
    S, W = out.shape
    H = W // HEAD_DIM
    tm = _rows(S, W * (4 + 2 + 2 + 2 + 2 + 4))

    def body(dg_ref, o_ref, z_ref, do_ref, dz_ref, dl_ref):
        z = z_ref[...].astype(F32)
        o = o_ref[...].astype(F32)
        d = dg_ref[...]
        sg = _sigmoid(z)
        dout = d * z * sg
        do_ref[...] = dout.astype(BF16)
        dz_ref[...] = (d * o * sg * (1.0 + z * (1.0 - sg))).astype(BF16)
        prod = dout * o
        for hh in range(H):
            sl = slice(hh * HEAD_DIM, (hh + 1) * HEAD_DIM)
            dl_ref[:, hh:hh + 1] = jnp.sum(prod[:, sl], axis=1, keepdims=True)

    row = pl.BlockSpec((tm, W), lambda i: (i, 0))
    return pl.pallas_call(
        body, name=name, grid=(S // tm,),
        in_specs=[row, row, pl.BlockSpec((tm, W), lambda i: (i, z_block))],
        out_specs=[row, row, pl.BlockSpec((tm, H), lambda i: (i, 0))],
        out_shape=[jax.ShapeDtypeStruct((S, W), BF16), jax.ShapeDtypeStruct((S, W), BF16),
                   jax.ShapeDtypeStruct((S, H), F32)],
        compiler_params=_params(),
    )(dg, out, h)


def _dil_mask(n):
    i = lax.broadcasted_iota(jnp.int32, (Q_BLOCK, 2 * Q_BLOCK), 0)
    j = lax.broadcasted_iota(jnp.int32, (Q_BLOCK, 2 * Q_BLOCK), 1)
    return (j >= i) & (j <= i + Q_BLOCK) & ((n > 0) | (j >= Q_BLOCK))


def _two_blocks(prev_ref, cur_ref, start):
    sl = slice(start, start + HEAD_DIM)
    return jnp.concatenate([prev_ref[:, sl], cur_ref[:, sl]], axis=0)


def _dil_fwd(hv, d, W, name):
    L = hv.shape[0]
    H = W // HEAD_DIM
    nblk = L // Q_BLOCK
    scale = HEAD_DIM ** -0.5
    nt = (((1,), (1,)), ((), ()))

    def body(prev_ref, cur_ref, o_ref, lse_ref):
        mask = _dil_mask(pl.program_id(1))
        for hh in range(H):
            sl = slice(hh * HEAD_DIM, (hh + 1) * HEAD_DIM)
            k = _two_blocks(prev_ref, cur_ref, W + hh * HEAD_DIM)
            v = _two_blocks(prev_ref, cur_ref, 2 * W + hh * HEAD_DIM)
            s = jnp.where(mask, lax.dot_general(cur_ref[:, sl], k, nt, preferred_element_type=F32) * scale, NEG)
            m = jnp.max(s, axis=1, keepdims=True)
            p = jnp.exp(s - m)
            den = jnp.sum(p, axis=1, keepdims=True)
            acc = jnp.dot(p.astype(BF16), v, preferred_element_type=F32)
            o_ref[:, sl] = (acc / den).astype(BF16)
            lse_ref[:, hh:hh + 1] = m + jnp.log(den)

    return pl.pallas_call(
        body, name=name, grid=(d, nblk),
        in_specs=[pl.BlockSpec((Q_BLOCK, 3 * W), lambda r, n: (jnp.maximum(n - 1, 0), r)),
                  pl.BlockSpec((Q_BLOCK, 3 * W), lambda r, n: (n, r))],
        out_specs=[pl.BlockSpec((Q_BLOCK, W), lambda r, n: (n, r)),
                   pl.BlockSpec((None, Q_BLOCK, H), lambda r, n: (r, n, 0))],
        out_shape=[jax.ShapeDtypeStruct((L, d * W), BF16), jax.ShapeDtypeStruct((d, L, H), F32)],
        compiler_params=_params(),
    )(hv, hv)


def _dil_combine_fwd(os, lses, h, W):
    S = h.shape[0]
    H = W // HEAD_DIM
    tm = _rows(S, W * (3 * 2 + 2 + 2 + 2) + 4 * H * 4)

    def body(o1, o2, o3, l1, l2, l3, z_ref, out_ref, lse_ref, g_ref):
        a, b, c = l1[...], l2[...], l3[...]
        m = jnp.maximum(jnp.maximum(a, b), c)
        ea, eb, ec = jnp.exp(a - m), jnp.exp(b - m), jnp.exp(c - m)
        den = ea + eb + ec
        wa, wb, wc = ea / den, eb / den, ec / den
        lse_ref[...] = m + jnp.log(den)
        for hh in range(H):
            sl = slice(hh * HEAD_DIM, (hh + 1) * HEAD_DIM)
            out = (wa[:, hh:hh + 1] * o1[:, sl].astype(F32) + wb[:, hh:hh + 1] * o2[:, sl].astype(F32)
                   + wc[:, hh:hh + 1] * o3[:, sl].astype(F32))
            z = z_ref[:, sl].astype(F32)
            out_ref[:, sl] = out.astype(BF16)
            g_ref[:, sl] = (out * z * _sigmoid(z)).astype(BF16)

    row = pl.BlockSpec((tm, W), lambda i: (i, 0))
    stat = pl.BlockSpec((tm, H), lambda i: (i, 0))
    return pl.pallas_call(
        body, name="dil_combine_fwd", grid=(S // tm,),
        in_specs=[row] * 3 + [stat] * 3 + [pl.BlockSpec((tm, W), lambda i: (i, 3))], out_specs=[row, stat, row],
        out_shape=[jax.ShapeDtypeStruct((S, W), BF16), jax.ShapeDtypeStruct((S, H), F32),
                   jax.ShapeDtypeStruct((S, W), BF16)],
        compiler_params=_params(),
    )(*os, *lses, h)


def _dil_bwd(hv, dov, lsev, dlv, d, W, name):
    L = hv.shape[0]
    H = W // HEAD_DIM
    nblk = L // Q_BLOCK
    scale = HEAD_DIM ** -0.5
    nt = (((1,), (1,)), ((), ()))
    tn = (((0,), (0,)), ((), ()))

    def body(prev_ref, cur_ref, do_ref, lse_ref, dl_ref, dq_ref, dk_ref, dv_ref, ck_ref, cv_ref):
        n = pl.program_id(1)

        @pl.when(n == 0)
        def _():
            ck_ref[...] = jnp.zeros_like(ck_ref)
            cv_ref[...] = jnp.zeros_like(cv_ref)

        @pl.when(n < nblk)
        def _():
            mask = _dil_mask(n)
            for hh in range(H):
                sl = slice(hh * HEAD_DIM, (hh + 1) * HEAD_DIM)
                q, do = cur_ref[:, sl], do_ref[:, sl]
                k = _two_blocks(prev_ref, cur_ref, W + hh * HEAD_DIM)
                v = _two_blocks(prev_ref, cur_ref, 2 * W + hh * HEAD_DIM)
                s = jnp.where(mask, lax.dot_general(q, k, nt, preferred_element_type=F32) * scale, NEG)
                p = jnp.exp(s - lse_ref[:, hh:hh + 1])
                ds = p * (lax.dot_general(do, v, nt, preferred_element_type=F32) - dl_ref[:, hh:hh + 1]) * scale
                ds_b = ds.astype(BF16)
                dq_ref[:, sl] = jnp.dot(ds_b, k, preferred_element_type=F32).astype(BF16)
                dk = lax.dot_general(ds_b, q, tn, preferred_element_type=F32)
                dv = lax.dot_general(p.astype(BF16), do, tn, preferred_element_type=F32)
                dk_ref[:, sl] = (ck_ref[:, sl] + dk[:Q_BLOCK]).astype(BF16)
                dv_ref[:, sl] = (cv_ref[:, sl] + dv[:Q_BLOCK]).astype(BF16)
                ck_ref[:, sl] = dk[Q_BLOCK:]
                cv_ref[:, sl] = dv[Q_BLOCK:]

        @pl.when(n == nblk)
        def _():
            dk_ref[...] = ck_ref[...].astype(BF16)
            dv_ref[...] = cv_ref[...].astype(BF16)

    last = nblk - 1

    qkv_prev = pl.BlockSpec((Q_BLOCK, 3 * W), lambda r, n: (jnp.clip(n - 1, 0, last), r))
    qkv_cur = pl.BlockSpec((Q_BLOCK, 3 * W), lambda r, n: (jnp.minimum(n, last), r))
    cur = pl.BlockSpec((Q_BLOCK, W), lambda r, n: (jnp.minimum(n, last), r))
    lag = pl.BlockSpec((Q_BLOCK, W), lambda r, n: (jnp.maximum(n - 1, 0), r))
    stat = pl.BlockSpec((None, Q_BLOCK, H), lambda r, n: (r, jnp.minimum(n, last), 0))
    shape = jax.ShapeDtypeStruct((L, d * W), BF16)
    return pl.pallas_call(
        body, name=name, grid=(d, nblk + 1),
        in_specs=[qkv_prev, qkv_cur, cur, stat, stat],
        out_specs=[cur, lag, lag], out_shape=[shape, shape, shape],
        scratch_shapes=[pltpu.VMEM((Q_BLOCK, W), F32), pltpu.VMEM((Q_BLOCK, W), F32)],
        compiler_params=_params(),
    )(hv, hv, dov, lsev, dlv)


def _dil_combine_bwd(dqs, dks, dvs, cos, sin):
    S, W = dqs[0].shape
    tm = _rows(S, W * (9 * 2 + 3 * 2 + 3 * 4))

    def body(q1, q2, q3, k1, k2, k3, v1, v2, v3, cos_ref, sin_ref, o_ref):
        cos_t, sin_t = cos_ref[...], -sin_ref[...]
        add3 = lambda a, b, c: a[...].astype(F32) + b[...].astype(F32) + c[...].astype(F32)
        dq = add3(q1, q2, q3)
        dk = add3(k1, k2, k3)
        for hh in range(W // HEAD_DIM):
            sl = slice(hh * HEAD_DIM, (hh + 1) * HEAD_DIM)
            tq, tk = dq[:, sl], dk[:, sl]
            o_ref[:, hh * HEAD_DIM:(hh + 1) * HEAD_DIM] = (
                tq * cos_t + pltpu.roll(tq, HEAD_DIM // 2, 1) * sin_t).astype(BF16)
            o_ref[:, W + hh * HEAD_DIM:W + (hh + 1) * HEAD_DIM] = (
                tk * cos_t + pltpu.roll(tk, HEAD_DIM // 2, 1) * sin_t).astype(BF16)
        o_ref[:, 2 * W:] = add3(v1, v2, v3).astype(BF16)

    row = pl.BlockSpec((tm, W), lambda i: (i, 0))
    tab = pl.BlockSpec((tm, HEAD_DIM), lambda i: (i, 0))
    return pl.pallas_call(
        body, name="dil_combine_bwd", grid=(S // tm,), in_specs=[row] * 9 + [tab, tab],
        out_specs=pl.BlockSpec((tm, 3 * W), lambda i: (i, 0)),
        out_shape=jax.ShapeDtypeStruct((S, 3 * W), BF16), compiler_params=_params(),
    )(*dqs, *dks, *dvs, cos, sin)


def _scan_tile(S):
    return _tile(S, 256, 8)


def _scan_fwd(hfg, bf_pad, f_block):
    S = hfg.shape[0]
    tm = _scan_tile(S)

    def body(f_ref, b_ref, c_ref, carry_ref):
        @pl.when(pl.program_id(0) == 0)
        def _():
            carry_ref[...] = jnp.zeros_like(carry_ref)

        v = f_ref[...] + b_ref[...]
        logf = jnp.minimum(v, 0.0) - jnp.log(1.0 + jnp.exp(-jnp.abs(v)))
        tri = (lax.broadcasted_iota(jnp.int32, (tm, tm), 1) <= lax.broadcasted_iota(jnp.int32, (tm, tm), 0)).astype(F32)
        c = jnp.dot(tri, logf, preferred_element_type=F32, precision=lax.Precision.HIGHEST) + carry_ref[...]
        c_ref[...] = c
        carry_ref[...] = c[tm - 1:tm, :]

    return pl.pallas_call(
        body, name="scan_fwd", grid=(S // tm,),
        in_specs=[pl.BlockSpec((tm, LANES), lambda i: (i, f_block)), pl.BlockSpec((1, LANES), lambda i: (0, 0))],
        out_specs=pl.BlockSpec((tm, LANES), lambda i: (i, 0)),
        out_shape=jax.ShapeDtypeStruct((S, LANES), F32),
        scratch_shapes=[pltpu.VMEM((1, LANES), F32)], compiler_params=_params(),
    )(hfg, bf_pad)


def _scan_bwd(dc, hfg, bf_pad, f_block, n_heads):
    S = hfg.shape[0]
    tm = _scan_tile(S)
    nt = S // tm

    def body(dc_ref, f_ref, b_ref, df_ref, db_ref, carry_ref):
        @pl.when(pl.program_id(0) == 0)
        def _():
            carry_ref[...] = jnp.zeros_like(carry_ref)
            db_ref[...] = jnp.zeros_like(db_ref)

        tri = (lax.broadcasted_iota(jnp.int32, (tm, tm), 1) >= lax.broadcasted_iota(jnp.int32, (tm, tm), 0)).astype(F32)
        dlogf = jnp.dot(tri, dc_ref[...], preferred_element_type=F32, precision=lax.Precision.HIGHEST) + carry_ref[...]
        carry_ref[...] = dlogf[0:1, :]
        v = f_ref[...] + b_ref[...]
        lane = lax.broadcasted_iota(jnp.int32, (tm, LANES), 1)
        df = jnp.where(lane < n_heads, dlogf * _sigmoid(-v), 0.0)
        df_ref[...] = df.astype(BF16)
        db_ref[...] += jnp.sum(df, axis=0, keepdims=True)

    return pl.pallas_call(
        body, name="scan_bwd", grid=(nt,),
        in_specs=[pl.BlockSpec((tm, LANES), lambda i: (nt - 1 - i, 0)),
                  pl.BlockSpec((tm, LANES), lambda i: (nt - 1 - i, f_block)),
                  pl.BlockSpec((1, LANES), lambda i: (0, 0))],
        out_specs=[pl.BlockSpec((tm, LANES), lambda i: (nt - 1 - i, 0)), pl.BlockSpec((1, LANES), lambda i: (0, 0))],
        out_shape=[jax.ShapeDtypeStruct((S, LANES), BF16), jax.ShapeDtypeStruct((1, LANES), F32)],
        scratch_shapes=[pltpu.VMEM((1, LANES), F32)], compiler_params=_params(),
    )(dc, hfg, bf_pad)


FOX_AUG = 2 * HEAD_DIM
FOX_V_ROWS = HEAD_DIM + 16
Q_ONES, K_ONES = HEAD_DIM, HEAD_DIM + 3
LOG2E = 1.4426950408889634


def _fox_prep(h, c, H):
    S = h.shape[0]
    W = H * HEAD_DIM
    tm = _rows(S, 3 * W * 2 + LANES * 4 + 2 * H * FOX_AUG * 2 + H * FOX_V_ROWS * 2, 128)
    inv_scale = HEAD_DIM ** 0.5

    def body(q_ref, k_ref, v_ref, c_ref, qa_ref, ka_ref, vt_ref):
        lane = lax.broadcasted_iota(jnp.int32, (tm, HEAD_DIM), 1)
        a = c_ref[...] * inv_scale
        for hh in range(H):
            vt_ref[hh * FOX_V_ROWS:hh * FOX_V_ROWS + HEAD_DIM, :] = (
                v_ref[:, hh * HEAD_DIM:(hh + 1) * HEAD_DIM].astype(F32).T.astype(BF16))
            vt_ref[hh * FOX_V_ROWS + HEAD_DIM:(hh + 1) * FOX_V_ROWS, :] = jnp.ones((FOX_V_ROWS - HEAD_DIM, tm), BF16)
            col = a[:, hh:hh + 1]
            hi = col.astype(BF16).astype(F32)
            mid = (col - hi).astype(BF16).astype(F32)
            lo = col - hi - mid
            piece = jnp.where(lane % 3 == 0, hi, jnp.where(lane % 3 == 1, mid, lo))
            extra_q = jnp.where(lane < 3, 1.0, jnp.where(lane < 6, piece, 0.0))
            extra_k = jnp.where(lane < 3, -piece, jnp.where(lane < 6, 1.0, 0.0))
            qa_ref[:, hh * FOX_AUG:hh * FOX_AUG + HEAD_DIM] = q_ref[:, hh * HEAD_DIM:(hh + 1) * HEAD_DIM]
            qa_ref[:, hh * FOX_AUG + HEAD_DIM:(hh + 1) * FOX_AUG] = extra_q.astype(BF16)
            ka_ref[:, hh * FOX_AUG:hh * FOX_AUG + HEAD_DIM] = k_ref[:, hh * HEAD_DIM:(hh + 1) * HEAD_DIM]
            ka_ref[:, hh * FOX_AUG + HEAD_DIM:(hh + 1) * FOX_AUG] = extra_k.astype(BF16)

    aug = pl.BlockSpec((tm, H * FOX_AUG), lambda i: (i, 0))
    return pl.pallas_call(
        body, name="fox_prep", grid=(S // tm,),
        in_specs=[pl.BlockSpec((tm, W), lambda i: (i, 4)), pl.BlockSpec((tm, W), lambda i: (i, 5)),
                  pl.BlockSpec((tm, W), lambda i: (i, 6)), pl.BlockSpec((tm, LANES), lambda i: (i, 0))],
        out_specs=[aug, aug, pl.BlockSpec((H * FOX_V_ROWS, tm), lambda i: (0, i))],
        out_shape=[jax.ShapeDtypeStruct((S, H * FOX_AUG), BF16)] * 2 + [jax.ShapeDtypeStruct((H * FOX_V_ROWS, S), BF16)],
        compiler_params=_params(),
    )(h, h, h, c)


def _causal_t(T):
    return lax.broadcasted_iota(jnp.int32, (T, T), 0) <= lax.broadcasted_iota(jnp.int32, (T, T), 1)


def _causal_pairs(n, by_query):
    if by_query:
        pairs = [(i, j) for i in range(n) for j in range(i + 1)]
    else:
        pairs = [(i, j) for j in range(n) for i in range(j, n)]
    return jnp.array([p[0] for p in pairs], jnp.int32), jnp.array([p[1] for p in pairs], jnp.int32)


def _fox_fwd(qa, ka, vt, h, H):
    S = h.shape[0]
    W = H * HEAD_DIM
    T = _tile(S, FOX_TILE, 128)
    nq = S // T
    k1 = HEAD_DIM ** -0.5 * LOG2E
    nt = (((1,), (1,)), ((), ()))

    qi, kj = _causal_pairs(nq, True)

    def body(qi_ref, kj_ref, q_ref, k_ref, vt_ref, z_ref, o_ref, g_ref, lse_ref, m_ref, acc_ref):
        i, j = qi_ref[pl.program_id(1)], kj_ref[pl.program_id(1)]

        @pl.when(j == 0)
        def _():
            m_ref[...] = jnp.full_like(m_ref, NEG)
            acc_ref[...] = jnp.zeros_like(acc_ref)

        def step(diag):
            raw = lax.dot_general(k_ref[...], q_ref[...], nt, preferred_element_type=F32)
            if diag:
                raw = jnp.where(_causal_t(T), raw, NEG)
            m_new = jnp.maximum(m_ref[...], jnp.max(raw, axis=0, keepdims=True))
            a = jnp.exp2((m_ref[...] - m_new) * k1)
            p = jnp.exp2((raw - m_new) * k1).astype(BF16)
            acc_ref[...] = a * acc_ref[...] + jnp.dot(vt_ref[...], p, preferred_element_type=F32)
            m_ref[...] = m_new

        @pl.when(j < i)
        def _():
            step(False)

        @pl.when(j == i)
        def _():
            step(True)

            acc = acc_ref[...]
            den = acc[HEAD_DIM:HEAD_DIM + 1, :]
            out = (acc[:HEAD_DIM, :] / den).T
            z = z_ref[...].astype(F32)
            o_ref[...] = out.astype(BF16)
            g_ref[...] = (out * z * _sigmoid(z)).astype(BF16)
            lse_ref[...] = m_ref[...] * k1 + jnp.log(den) * LOG2E

    out = pl.BlockSpec((T, HEAD_DIM), lambda hh, p, qi, kj: (qi[p], hh))
    grid_spec = pltpu.PrefetchScalarGridSpec(
        num_scalar_prefetch=2, grid=(H, qi.shape[0]),
        in_specs=[pl.BlockSpec((T, FOX_AUG), lambda hh, p, qi, kj: (qi[p], hh)),
                  pl.BlockSpec((T, FOX_AUG), lambda hh, p, qi, kj: (kj[p], hh)),
                  pl.BlockSpec((FOX_V_ROWS, T), lambda hh, p, qi, kj: (hh, kj[p])),
                  pl.BlockSpec((T, HEAD_DIM), lambda hh, p, qi, kj: (qi[p], 7 * H + hh))],
        out_specs=[out, out, pl.BlockSpec((None, 1, T), lambda hh, p, qi, kj: (hh, 0, qi[p]))],
        scratch_shapes=[pltpu.VMEM((1, T), F32), pltpu.VMEM((FOX_V_ROWS, T), F32)])
    return pl.pallas_call(
        body, name="fox_fwd", grid_spec=grid_spec,
        out_shape=[jax.ShapeDtypeStruct((S, W), BF16), jax.ShapeDtypeStruct((S, W), BF16),
                   jax.ShapeDtypeStruct((H, 1, S), F32)],
        compiler_params=_params(),
    )(qi, kj, qa, ka, vt, h)


def _fox_bwd(qa, ka, h, do, lse, dl, H):
    S = h.shape[0]
    W = H * HEAD_DIM
    T = _tile(S, FOX_TILE, 128)
    nq = S // T
    k1 = HEAD_DIM ** -0.5 * LOG2E
    nt = (((1,), (1,)), ((), ()))
    tn = (((0,), (0,)), ((), ()))

    qi, kj = _causal_pairs(nq, False)

    def body(qi_ref, kj_ref, q_ref, k_ref, v_ref, do_ref, lse_ref, dl_ref, dq_ref, dk_ref, dv_ref, ak_ref, av_ref):
        i, j = qi_ref[pl.program_id(1)], kj_ref[pl.program_id(1)]

        @pl.when(pl.program_id(1) == 0)
        def _():
            dq_ref[...] = jnp.zeros_like(dq_ref)

        @pl.when(i == j)
        def _():
            ak_ref[...] = jnp.zeros_like(ak_ref)
            av_ref[...] = jnp.zeros_like(av_ref)

        def step(diag):
            q, k, v, d_o = q_ref[...], k_ref[...], v_ref[...], do_ref[...]
            raw = lax.dot_general(k, q, nt, preferred_element_type=F32)
            if diag:
                raw = jnp.where(_causal_t(T), raw, NEG)
            p = jnp.exp2(raw * k1 - lse_ref[...])
            dp = lax.dot_general(v, d_o, nt, preferred_element_type=F32)
            ds = (p * (dp - dl_ref[...])).astype(BF16)
            av_ref[...] += jnp.dot(p.astype(BF16), d_o, preferred_element_type=F32)
            ak_ref[...] += jnp.dot(ds, q, preferred_element_type=F32)
            rows = pl.ds(pl.multiple_of(i * T, T), T)
            dq_ref[rows, :] += lax.dot_general(ds, k, tn, preferred_element_type=F32)

        @pl.when(i > j)
        def _():
            step(False)

        @pl.when(i == j)
        def _():
            step(True)

        @pl.when(i == nq - 1)
        def _():
            dk_ref[...] = ak_ref[...]
            dv_ref[...] = av_ref[...]

    qrow = lambda hh, p, qi, kj: (qi[p], hh)
    krow = lambda hh, p, qi, kj: (kj[p], hh)
    stat = pl.BlockSpec((None, 1, T), lambda hh, p, qi, kj: (hh, 0, qi[p]))
    grid_spec = pltpu.PrefetchScalarGridSpec(
        num_scalar_prefetch=2, grid=(H, qi.shape[0]),
        in_specs=[pl.BlockSpec((T, FOX_AUG), qrow), pl.BlockSpec((T, FOX_AUG), krow),
                  pl.BlockSpec((T, HEAD_DIM), lambda hh, p, qi, kj: (kj[p], 6 * H + hh)),
                  pl.BlockSpec((T, HEAD_DIM), qrow), stat, stat],
        out_specs=[pl.BlockSpec((S, FOX_AUG), lambda hh, p, qi, kj: (0, hh)), pl.BlockSpec((T, FOX_AUG), krow),
                   pl.BlockSpec((T, HEAD_DIM), krow)],
        scratch_shapes=[pltpu.VMEM((T, FOX_AUG), F32), pltpu.VMEM((T, HEAD_DIM), F32)])
    return pl.pallas_call(
        body, name="fox_bwd", grid_spec=grid_spec,
        out_shape=[jax.ShapeDtypeStruct((S, H * FOX_AUG), F32), jax.ShapeDtypeStruct((S, H * FOX_AUG), F32),
                   jax.ShapeDtypeStruct((S, W), F32)],
        compiler_params=_params(),
    )(qi, kj, qa, ka, h, do, lse, dl)


def _adamw_math(w, g, m, v):
    m = ADAM_B1 * m + (1.0 - ADAM_B1) * g
    v = ADAM_B2 * v + (1.0 - ADAM_B2) * (g * g)
    m_hat = m / (1.0 - ADAM_B1 ** ADAM_STEP)
    v_hat = v / (1.0 - ADAM_B2 ** ADAM_STEP)
    delta = -ADAM_LR * (m_hat / (jnp.sqrt(v_hat) + ADAM_EPS) + ADAM_WD * w)
    return delta, m, v


def _adamw(w, g_mine, g_theirs, m, v, name):
    L, R, C = w.shape
    half = L // 2
    tr = _rows(R, C * 4 * 9, 8)

    def body(c_ref, w_ref, gm_ref, gt_ref, m_ref, v_ref, g_ref, d_ref, nm_ref, nv_ref):
        g = jnp.where(pl.program_id(0) // half == c_ref[0], gm_ref[...], gt_ref[...])
        g_ref[...] = g
        d_ref[...], nm_ref[...], nv_ref[...] = _adamw_math(w_ref[...], g, m_ref[...], v_ref[...])

    blk = pl.BlockSpec((None, tr, C), lambda l, i, c: (l, i, 0))
    mine = pl.BlockSpec((None, tr, C), lambda l, i, c: (jnp.clip(l - c[0] * half, 0, half - 1), i, 0))
    theirs = pl.BlockSpec((None, tr, C), lambda l, i, c: (jnp.clip(l - (1 - c[0]) * half, 0, half - 1), i, 0))
    grid_spec = pltpu.PrefetchScalarGridSpec(
        num_scalar_prefetch=1, grid=(L, R // tr), in_specs=[blk, mine, theirs, blk, blk], out_specs=[blk] * 4)
    core = lax.axis_index("c").astype(jnp.int32).reshape(1)
    return pl.pallas_call(
        body, name=name, grid_spec=grid_spec, out_shape=[jax.ShapeDtypeStruct((L, R, C), F32)] * 4,
        compiler_params=_params(),
    )(core, w, g_mine, g_theirs, m, v)


def _place():
    x, y, c = lax.axis_index("x"), lax.axis_index("y"), lax.axis_index("c")
    return x, y, c, [(1 - x, y), (x, 1 - y), (1 - x, 1 - y)]


def _remote(src, dst, send_sems, recv_sems, k, to):
    return pltpu.make_async_remote_copy(src_ref=src, dst_ref=dst, send_sem=send_sems.at[k], recv_sem=recv_sems.at[k],
                                        device_id=to, device_id_type=MESH)


def _gather_weights(shards):
    n = len(shards)
    half = shards[0].shape[0] // 2

    def body(*refs):
        srcs, dsts = refs[:n], refs[n:2 * n]
        send_sems, recv_sems = refs[2 * n:]
        x, y, c, chips = _place()
        me = 2 * x + y
        mine, theirs = pl.ds(c * half, half), pl.ds((1 - c) * half, half)
        first = [_remote(srcs[a].at[mine], dsts[a].at[me, mine], send_sems, recv_sems, 6 * a + j, (px, py, c))
                 for a in range(n) for j, (px, py) in enumerate(chips)]
        for cp in first:
            cp.start()
        passed = []
        for a in range(n):
            for j, (px, py) in enumerate(chips):
                landed = dsts[a].at[2 * px + py, mine]
                _remote(landed, landed, send_sems, recv_sems, 6 * a + j, (px, py, c)).wait_recv()
                cp = _remote(landed, landed, send_sems, recv_sems, 6 * a + 3 + j, (x, y, 1 - c))
                cp.start()
                passed.append(cp)
        for a in range(n):
            for j, (px, py) in enumerate(chips):
                landed = dsts[a].at[2 * px + py, theirs]
                _remote(landed, landed, send_sems, recv_sems, 6 * a + 3 + j, (x, y, 1 - c)).wait_recv()
        for cp in first + passed:
            cp.wait_send()

    return pl.pallas_call(
        body, name="gather_weights", in_specs=[ANY] * n, out_specs=[ANY] * n,
        out_shape=[jax.ShapeDtypeStruct((4,) + s.shape, s.dtype) for s in shards],
        scratch_shapes=[pltpu.SemaphoreType.DMA((6 * n,)), pltpu.SemaphoreType.DMA((6 * n,))],
    )(*shards)


def _swap_other_half(parts):
    n = len(parts)
    half = parts[0].shape[1] // 2

    def body(*refs):
        srcs, dsts = refs[:n], refs[n:2 * n]
        send_sems, recv_sems = refs[2 * n:]
        x, y, c, _ = _place()
        cps = [_remote(srcs[a].at[:, pl.ds((1 - c) * half, half)], dsts[a], send_sems, recv_sems, a, (x, y, 1 - c))
               for a in range(n)]
        for cp in cps:
            cp.start()
        for cp in cps:
            cp.wait()

    return pl.pallas_call(
        body, name="grad_swap_half", in_specs=[ANY] * n, out_specs=[ANY] * n,
        out_shape=[jax.ShapeDtypeStruct((4, half) + p.shape[2:], p.dtype) for p in parts],
        scratch_shapes=[pltpu.SemaphoreType.DMA((n,)), pltpu.SemaphoreType.DMA((n,))],
    )(*parts)


def _add_half(part, got, name):
    _, half, R, C = got.shape
    tr = _rows(R, C * 2 * 3)

    def body(c_ref, p_ref, g_ref, o_ref):
        o_ref[...] = (p_ref[...].astype(F32) + g_ref[...].astype(F32)).astype(BF16)

    grid_spec = pltpu.PrefetchScalarGridSpec(
        num_scalar_prefetch=1, grid=(4, half, R // tr),
        in_specs=[pl.BlockSpec((None, None, tr, C), lambda s, l, i, c: (s, c[0] * half + l, i, 0)),
                  pl.BlockSpec((None, None, tr, C), lambda s, l, i, c: (s, l, i, 0))],
        out_specs=pl.BlockSpec((None, None, tr, C), lambda s, l, i, c: (s, l, i, 0)))
    core = lax.axis_index("c").astype(jnp.int32).reshape(1)
    return pl.pallas_call(
        body, name=name, grid_spec=grid_spec, out_shape=jax.ShapeDtypeStruct(got.shape, BF16),
        compiler_params=_params(),
    )(core, part, got)


def _scatter_to_owner(parts):
    n = len(parts)

    def body(*refs):
        srcs, dsts = refs[:n], refs[n:2 * n]
        send_sems, recv_sems, local_sems = refs[2 * n:]
        x, y, c, chips = _place()
        me = 2 * x + y
        local = [pltpu.make_async_copy(srcs[a].at[me], dsts[a].at[me], local_sems.at[a]) for a in range(n)]
        for cp in local:
            cp.start()
        sends = [_remote(srcs[a].at[2 * px + py], dsts[a].at[me], send_sems, recv_sems, 3 * a + j, (px, py, c))
                 for a in range(n) for j, (px, py) in enumerate(chips)]
        for cp in sends:
            cp.start()
        for a in range(n):
            for j, (px, py) in enumerate(chips):
                slot = dsts[a].at[2 * px + py]
                _remote(slot, slot, send_sems, recv_sems, 3 * a + j, (px, py, c)).wait_recv()
        for cp in sends:
            cp.wait_send()
        for cp in local:
            cp.wait()

    return pl.pallas_call(
        body, name="grad_scatter", in_specs=[ANY] * n, out_specs=[ANY] * n,
        out_shape=[jax.ShapeDtypeStruct(p.shape, p.dtype) for p in parts],
        scratch_shapes=[pltpu.SemaphoreType.DMA((3 * n,)), pltpu.SemaphoreType.DMA((3 * n,)),
                        pltpu.SemaphoreType.DMA((n,))],
    )(*parts)


def _sum_chips(got, name):
    _, half, R, C = got.shape
    tr = _rows(R, C * (2 * 4 + 4))

    def body(g_ref, o_ref):
        o_ref[...] = ((g_ref[0].astype(F32) + g_ref[1].astype(F32)) + g_ref[2].astype(F32)) + g_ref[3].astype(F32)

    return pl.pallas_call(
        body, name=name, grid=(half, R // tr),
        in_specs=[pl.BlockSpec((4, None, tr, C), lambda l, i: (0, l, i, 0))],
        out_specs=pl.BlockSpec((None, tr, C), lambda l, i: (l, i, 0)),
        out_shape=jax.ShapeDtypeStruct((half, R, C), F32), compiler_params=_params(),
    )(got)


def _share_halves(halves):
    n = len(halves)

    def body(*refs):
        srcs, dsts = refs[:n], refs[n:2 * n]
        send_sems, recv_sems = refs[2 * n:]
        x, y, c, _ = _place()
        cps = [_remote(srcs[a], dsts[a], send_sems, recv_sems, a, (x, y, 1 - c)) for a in range(n)]
        for cp in cps:
            cp.start()
        for cp in cps:
            cp.wait()

    return pl.pallas_call(
        body, name="grad_share_halves", in_specs=[ANY] * n, out_specs=[ANY] * n,
        out_shape=[jax.ShapeDtypeStruct(h.shape, h.dtype) for h in halves],
        scratch_shapes=[pltpu.SemaphoreType.DMA((n,)), pltpu.SemaphoreType.DMA((n,))],
    )(*halves)


def _small_allreduce_adamw(part, w, m, v):
    R = part.shape[0]
    deltas = [(dx, dy, dc) for dx in (0, 1) for dy in (0, 1) for dc in (0, 1)][1:]

    def body(p_ref, w_ref, m_ref, v_ref, g_ref, d_ref, nm_ref, nv_ref, all_ref, send_sems, recv_sems):
        x, y, c, _ = _place()
        me = 4 * x + 2 * y + c
        all_ref[me] = p_ref[...]
        cps = [_remote(p_ref, all_ref.at[me], send_sems, recv_sems, k, (x ^ dx, y ^ dy, c ^ dc))
               for k, (dx, dy, dc) in enumerate(deltas)]
        for cp in cps:
            cp.start()
        for k, (dx, dy, dc) in enumerate(deltas):
            slot = all_ref.at[4 * (x ^ dx) + 2 * (y ^ dy) + (c ^ dc)]
            _remote(slot, slot, send_sems, recv_sems, k, (x ^ dx, y ^ dy, c ^ dc)).wait_recv()
        for cp in cps:
            cp.wait_send()
        g = all_ref[0]
        for k in range(1, 8):
            g = g + all_ref[k]
        g_ref[...] = g
        d_ref[...], nm_ref[...], nv_ref[...] = _adamw_math(w_ref[...], g, m_ref[...], v_ref[...])

    vm = pl.BlockSpec(memory_space=pltpu.VMEM)
    shape = jax.ShapeDtypeStruct((R, LANES), F32)
    return pl.pallas_call(
        body, name="small_allreduce_adamw", in_specs=[vm] * 4, out_specs=[vm] * 4, out_shape=[shape] * 4,
        scratch_shapes=[pltpu.VMEM((8, R, LANES), F32), pltpu.SemaphoreType.DMA((7,)), pltpu.SemaphoreType.DMA((7,))],
    )(part, w, m, v)


def _pack_small(bf, bg, lg, lb, extra=None):
    L, H = bf.shape
    per = jnp.concatenate([jnp.pad(bf, ((0, 0), (0, LANES - H))), bg, lg, lb], axis=1)
    flat = per.reshape(-1, LANES)
    last = jnp.zeros((8 + (-flat.shape[0]) % 8, LANES), F32)
    if extra is not None:
        last = last.at[-8, 0].set(extra)
    return jnp.concatenate([flat, last], axis=0)


def _unpack_small(p, L, H, D):
    per = p[:L * (1 + 4 * D // LANES)].reshape(L, -1)
    return per[:, :H], per[:, LANES:LANES + 2 * D], per[:, LANES + 2 * D:LANES + 3 * D], per[:, LANES + 3 * D:]


def kernel(x, w_in, b_forget, b_gate, w_up_a, w_up_b, w_out, ln_g, ln_b, loss_target, m_w_in, m_b_forget, m_b_gate, m_w_up_a, m_w_up_b, m_w_out, m_ln_g, m_ln_b, v_w_in, v_b_forget, v_b_gate, v_w_up_a, v_w_up_b, v_w_out, v_ln_g, v_ln_b):
    _, S, D = x.shape
    L, _, C4 = w_in.shape
    H = b_forget.shape[1]
    W = w_up_a.shape[1]
    D4 = D // 4
    NC = 4 * C4
    assert W == H * HEAD_DIM and NC == 8 * W + H + 2 * D and L % 2 == 0 and D % LANES == 0
    alpha = float((2 * L) ** 0.25)
    f_block = 2 * D // LANES

    own = [w_in.astype(BF16), jnp.concatenate([w_up_a, w_up_b], axis=2).astype(BF16), w_out.astype(BF16)]
    gathered = _gather_weights(own)
    me = 2 * lax.axis_index("x") + lax.axis_index("y")
    shard = lambda a, s, l: jnp.where(me == s, own[a][l], gathered[a][s, l])
    w_main, w_fg, w_ua, w_ub, w_o = [], [], [], [], []
    for l in range(L):
        full = jnp.concatenate([shard(0, s, l) for s in range(4)], axis=1)
        w_main.append(full[:, :8 * W])
        w_fg.append(jnp.concatenate([full[:, 8 * W + H:], full[:, 8 * W:8 * W + H],
                                     jnp.zeros((D, LANES - H), BF16)], axis=1))
        ups = [shard(1, s, l) for s in range(4)]
        w_ua.append(jnp.concatenate([u[:, :D4] for u in ups], axis=1))
        w_ub.append(jnp.concatenate([u[:, D4:] for u in ups], axis=1))
        w_o.append(jnp.concatenate([shard(2, s, l) for s in range(4)], axis=0))

    pos = jnp.arange(S, dtype=F32)
    inv_freq = ROPE_THETA ** (-jnp.arange(HEAD_DIM // 2, dtype=F32) / (HEAD_DIM // 2))
    ang = pos[:, None] * inv_freq[None, :]
    cos = jnp.concatenate([jnp.cos(ang), jnp.cos(ang)], axis=1)
    sin = jnp.concatenate([-jnp.sin(ang), jnp.sin(ang)], axis=1)
    bf_pad = jnp.pad(b_forget, ((0, 0), (0, LANES - H)))

    xs = x[0]
    xb = xs.astype(BF16)
    saved = []
    for l in range(L):
        h = _matmul(xb, w_main[l], mode="nn", out_dtype=BF16, name="in_proj", rope=(cos, sin), rope_cols=2 * W)
        hfg = _matmul(xb, w_fg[l], mode="nn", out_dtype=F32, name="in_proj_gates")
        qkv_a = h[:, :3 * W]
        views = [qkv_a.reshape(S // d, d * 3 * W) for _, d in DILATED_PATTERNS]
        os, lses = [], []
        for (_, d), hv in zip(DILATED_PATTERNS, views):
            o, lse = _dil_fwd(hv, d, W, f"dil_fwd_d{d}")
            os.append(o.reshape(S, W))
            lses.append(lse.transpose(1, 0, 2).reshape(S, H))
        out_a, lse_a, ga = _dil_combine_fwd(os, lses, h, W)
        c = _scan_fwd(hfg, bf_pad[l:l + 1], f_block)
        qa, ka, vt = _fox_prep(h, c, H)
        out_b, gb, lse_b = _fox_fwd(qa, ka, vt, h, H)
        up_a = _matmul(ga, w_ua[l], mode="nn", out_dtype=BF16, name="up_proj")
        up_b = _matmul(gb, w_ub[l], mode="nn", out_dtype=BF16, name="up_proj")
        u = _merge_fwd(up_a, up_b, hfg, b_gate[l:l + 1])
        r = _matmul(u, w_o[l], mode="nn", out_dtype=F32, name="out_proj", acc_in=xs, acc_scale=alpha)
        saved.append((xb, h, hfg, views, out_a, lse_a, ga, qa, ka, out_b, gb, lse_b, up_a, up_b, u, r))
        xs, xb = _ln_fwd(r, ln_g[l:l + 1], ln_b[l:l + 1])

    dx, sq = _loss(xs, loss_target[0])
    loss_part = 0.5 * jnp.sum(sq) / D

    g_in, g_up, g_out, g_bf, g_bg, g_lg, g_lb = [], [], [], [], [], [], []
    for l in reversed(range(L)):
        xb, h, hfg, views, out_a, lse_a, ga, qa, ka, out_b, gb, lse_b, up_a, up_b, u, r = saved[l]
        dr, adr, dlg, dlb = _ln_bwd(dx, r, ln_g[l:l + 1], alpha)
        du = _matmul(dr, w_o[l], mode="nt", out_dtype=F32, name="out_proj_dx")
        dwo = _matmul(u, dr, mode="tn", out_dtype=F32, name="out_proj_dw")
        dua, dub, dgl, dbg = _merge_bwd(du, up_a, up_b, hfg, b_gate[l:l + 1])
        dga = _matmul(dua, w_ua[l], mode="nt", out_dtype=F32, name="up_proj_dx")
        dgb = _matmul(dub, w_ub[l], mode="nt", out_dtype=F32, name="up_proj_dx")
        dwua = _matmul(ga, dua, mode="tn", out_dtype=F32, name="up_proj_dw")
        dwub = _matmul(gb, dub, mode="tn", out_dtype=F32, name="up_proj_dw")
        do_a, dz_a, dl_a = _gate_bwd(dga, out_a, h, 3, "gate_bwd_a")
        dqs, dks, dvs = [], [], []
        for (_, d), hv in zip(DILATED_PATTERNS, views):
            stat = lambda t: t.reshape(S // d, d, H).transpose(1, 0, 2)
            dq, dk, dv = _dil_bwd(hv, do_a.reshape(S // d, d * W), stat(lse_a), stat(dl_a), d, W, f"dil_bwd_d{d}")
            dqs.append(dq.reshape(S, W))
            dks.append(dk.reshape(S, W))
            dvs.append(dv.reshape(S, W))
        dqkv_a = _dil_combine_bwd(dqs, dks, dvs, cos, sin)
        do_b, dz_b, dl_b = _gate_bwd(dgb, out_b, h, 7, "gate_bwd_b")
        dq_f, dk_f, dv_b = _fox_bwd(qa, ka, h, do_b, lse_b, dl_b.T.reshape(H, 1, S), H)
        dq_f, dk_f = dq_f.reshape(S, H, FOX_AUG), dk_f.reshape(S, H, FOX_AUG)
        dc = jnp.pad(dq_f[:, :, K_ONES] - dk_f[:, :, Q_ONES], ((0, 0), (0, LANES - H)))
        df, dbf = _scan_bwd(dc, hfg, bf_pad[l:l + 1], f_block, H)
        att_scale = HEAD_DIM ** -0.5
        dq_b = (dq_f[:, :, :HEAD_DIM] * att_scale).astype(BF16).reshape(S, W)
        dk_b = (dk_f[:, :, :HEAD_DIM] * att_scale).astype(BF16).reshape(S, W)
        dh = jnp.concatenate([dqkv_a, dz_a, dq_b, dk_b, dv_b.astype(BF16), dz_b], axis=1)
        dhfg = jnp.concatenate([dgl, df], axis=1)
        dx1 = _matmul(dh, w_main[l], mode="nt", out_dtype=F32, name="in_proj_dx", acc_in=adr)
        dx = _matmul(dhfg, w_fg[l], mode="nt", out_dtype=F32, name="in_proj_gates_dx", acc_in=dx1)
        dwm = _matmul(xb, dh, mode="tn", out_dtype=F32, name="in_proj_dw")
        dwfg = _matmul(xb, dhfg, mode="tn", out_dtype=F32, name="in_proj_gates_dw")
        full = jnp.concatenate([dwm, dwfg[:, 2 * D:2 * D + H], dwfg[:, :2 * D]], axis=1)
        g_in.append(full.reshape(D, 4, C4).transpose(1, 0, 2).astype(BF16))
        g_up.append(jnp.concatenate([dwua.reshape(W, 4, D4), dwub.reshape(W, 4, D4)], axis=2).transpose(1, 0, 2).astype(BF16))
        g_out.append(dwo.reshape(4, D4, D).astype(BF16))
        g_bf.append(dbf[0, :H])
        g_bg.append(dbg[0])
        g_lg.append(dlg[0])
        g_lb.append(dlb[0])
    grad_x = dx[None]
    for lst in (g_in, g_up, g_out, g_bf, g_bg, g_lg, g_lb):
        lst.reverse()

    parts = [jnp.stack(g_in, axis=1), jnp.stack(g_up, axis=1), jnp.stack(g_out, axis=1)]
    names = ["w_in", "w_up", "w_out"]
    got = _swap_other_half(parts)
    chip = [_add_half(p, g, f"grad_add_half_{n}") for p, g, n in zip(parts, got, names)]
    landed = _scatter_to_owner(chip)
    halves = [_sum_chips(g, f"grad_sum_chips_{n}") for g, n in zip(landed, names)]
    theirs = _share_halves(halves)

    pair = lambda a, b: jnp.concatenate([a, b], axis=2)
    grad_w_in, d_in, nm_in, nv_in = _adamw(w_in, halves[0], theirs[0], m_w_in, v_w_in, "adamw_w_in")
    up = _adamw(pair(w_up_a, w_up_b), halves[1], theirs[1], pair(m_w_up_a, m_w_up_b), pair(v_w_up_a, v_w_up_b),
                "adamw_w_up")
    (grad_w_up_a, grad_w_up_b), (d_ua, d_ub), (nm_ua, nm_ub), (nv_ua, nv_ub) = [
        (t[:, :, :D4], t[:, :, D4:]) for t in up]
    grad_w_out, d_o, nm_o, nv_o = _adamw(w_out, halves[2], theirs[2], m_w_out, v_w_out, "adamw_w_out")

    small_g = _pack_small(jnp.stack(g_bf), jnp.stack(g_bg), jnp.stack(g_lg), jnp.stack(g_lb), loss_part)
    small = _small_allreduce_adamw(small_g, _pack_small(b_forget, b_gate, ln_g, ln_b),
                                   _pack_small(m_b_forget, m_b_gate, m_ln_g, m_ln_b),
                                   _pack_small(v_b_forget, v_b_gate, v_ln_g, v_ln_b))
    loss = small[0][-8, 0]
    (g_bf, g_bg, g_lg, g_lb), (d_bf, d_bg, d_lg, d_lb), (nm_bf, nm_bg, nm_lg, nm_lb), (nv_bf, nv_bg, nv_lg, nv_lb) = [
        _unpack_small(p, L, H, D) for p in small]

    return (loss, grad_x,
            grad_w_in, g_bf, g_bg, grad_w_up_a, grad_w_up_b, grad_w_out, g_lg, g_lb,
            d_in, d_bf, d_bg, d_ua, d_ub, d_o, d_lg, d_lb,
            nm_in, nm_bf, nm_bg, nm_ua, nm_ub, nm_o, nm_lg, nm_lb,
            nv_in, nv_bf, nv_bg, nv_ua, nv_ub, nv_o, nv_lg, nv_lb)
```

```python
import jax
import jax.numpy as jnp
from jax import lax
from jax.experimental import pallas as pl
from jax.experimental.pallas import tpu as pltpu

F32 = jnp.float32
BF16 = jnp.bfloat16
MESH = pl.DeviceIdType.MESH
ANY = pl.BlockSpec(memory_space=pl.ANY)

HEAD_DIM = 128
LANES = 128
Q_BLOCK = 128
DILATED_PATTERNS = ((128, 1), (512, 4), (2048, 16))
ROPE_THETA = 10000.0
LN_EPS = 1e-5
ADAM_LR, ADAM_B1, ADAM_B2, ADAM_EPS, ADAM_WD, ADAM_STEP = 0.001, 0.9, 0.999, 1e-08, 0.01, 10
NEG = -1e30
VMEM_LIMIT = 56 * 2**20
ELEMENTWISE_BUDGET = 20 * 2**20
FOX_TILE = 1024
MM_TILES = (1024, 11 * LANES, 2048)


def _params():
    return pltpu.CompilerParams(vmem_limit_bytes=VMEM_LIMIT)


def _tile(dim, target, align):
    if dim <= target:
        return dim
    t = (target // align) * align
    while t >= align:
        if dim % t == 0:
            return t
        t -= align
    return dim


def _rows(n_rows, bytes_per_row, align=16):
    return _tile(n_rows, max(align, ELEMENTWISE_BUDGET // (2 * bytes_per_row)), align)


def _sigmoid(v):
    return 1.0 / (1.0 + jnp.exp(-v))


def _matmul(a, b, *, mode, out_dtype, name, acc_in=None, acc_scale=1.0, rope=None, rope_cols=0):
    if mode == "nn":
        (M, K), (K2, N) = a.shape, b.shape
    elif mode == "nt":
        (M, K), (N, K2) = a.shape, b.shape
    else:
        (K, M), (K2, N) = a.shape, b.shape
    assert K == K2, (a.shape, b.shape, mode)
    tm, tn, tk = _tile(M, MM_TILES[0], 128), _tile(N, MM_TILES[1], 128), _tile(K, MM_TILES[2], 128)
    if rope is not None:
        assert rope_cols % tn == 0
    nk = K // tk
    n_rope_tiles = rope_cols // tn if rope is not None else 0
    dims = {"nn": (((1,), (0,)), ((), ())), "nt": (((1,), (1,)), ((), ())), "tn": (((0,), (0,)), ((), ()))}[mode]

    def body(*refs):
        a_ref, b_ref = refs[0], refs[1]
        pos = 2
        if rope is not None:
            cos_ref, sin_ref = refs[pos], refs[pos + 1]
            pos += 2
        if acc_in is not None:
            acc_in_ref = refs[pos]
            pos += 1
        o_ref, acc_ref = refs[pos], refs[pos + 1]
        j, k = pl.program_id(1), pl.program_id(2)

        @pl.when(k == 0)
        def _():
            acc_ref[...] = jnp.zeros_like(acc_ref)

        acc_ref[...] += lax.dot_general(a_ref[...], b_ref[...], dims, preferred_element_type=F32)

        def finish(rotate):
            r = acc_ref[...]
            if acc_in is not None:
                r = r + acc_scale * acc_in_ref[...]
            if rotate:
                cos, sin = cos_ref[...], sin_ref[...]
                for g in range(tn // HEAD_DIM):
                    sl = slice(g * HEAD_DIM, (g + 1) * HEAD_DIM)
                    t = r[:, sl]
                    o_ref[:, sl] = (t * cos + pltpu.roll(t, HEAD_DIM // 2, 1) * sin).astype(o_ref.dtype)
            else:
                o_ref[...] = r.astype(o_ref.dtype)

        if n_rope_tiles:
            @pl.when((k == nk - 1) & (j < n_rope_tiles))
            def _():
                finish(True)

            @pl.when((k == nk - 1) & (j >= n_rope_tiles))
            def _():
                finish(False)
        else:
            @pl.when(k == nk - 1)
            def _():
                finish(False)

    if mode == "nn":
        in_specs = [pl.BlockSpec((tm, tk), lambda i, j, k: (i, k)), pl.BlockSpec((tk, tn), lambda i, j, k: (k, j))]
    elif mode == "nt":
        in_specs = [pl.BlockSpec((tm, tk), lambda i, j, k: (i, k)), pl.BlockSpec((tn, tk), lambda i, j, k: (j, k))]
    else:
        in_specs = [pl.BlockSpec((tk, tm), lambda i, j, k: (k, i)), pl.BlockSpec((tk, tn), lambda i, j, k: (k, j))]
    args = [a, b]
    if rope is not None:
        in_specs += [pl.BlockSpec((tm, HEAD_DIM), lambda i, j, k: (i, 0))] * 2
        args += list(rope)
    if acc_in is not None:
        in_specs.append(pl.BlockSpec((tm, tn), lambda i, j, k: (i, j)))
        args.append(acc_in)
    return pl.pallas_call(
        body, name=name, grid=(M // tm, N // tn, nk), in_specs=in_specs,
        out_specs=pl.BlockSpec((tm, tn), lambda i, j, k: (i, j)),
        out_shape=jax.ShapeDtypeStruct((M, N), out_dtype),
        scratch_shapes=[pltpu.VMEM((tm, tn), F32)], compiler_params=_params(),
    )(*args)


def _ln_fwd(r, g, b):
    S, D = r.shape
    tm = _rows(S, D * (4 + 4 + 2))

    def body(r_ref, g_ref, b_ref, x_ref, xb_ref):
        v = r_ref[...]
        mu = jnp.mean(v, axis=1, keepdims=True)
        cen = v - mu
        var = jnp.mean(cen * cen, axis=1, keepdims=True)
        out = cen * lax.rsqrt(var + LN_EPS) * g_ref[...] + b_ref[...]
        x_ref[...] = out
        xb_ref[...] = out.astype(BF16)

    row = pl.BlockSpec((tm, D), lambda i: (i, 0))
    vec = pl.BlockSpec((1, D), lambda i: (0, 0))
    return pl.pallas_call(
        body, name="ln_fwd", grid=(S // tm,), in_specs=[row, vec, vec], out_specs=[row, row],
        out_shape=[jax.ShapeDtypeStruct((S, D), F32), jax.ShapeDtypeStruct((S, D), BF16)],
        compiler_params=_params(),
    )(r, g, b)


def _ln_bwd(dx, r, g, alpha):
    S, D = r.shape
    tm = _rows(S, D * (4 + 4 + 2 + 4))

    def body(dx_ref, r_ref, g_ref, drb_ref, adr_ref, dg_ref, db_ref):
        @pl.when(pl.program_id(0) == 0)
        def _():
            dg_ref[...] = jnp.zeros_like(dg_ref)
            db_ref[...] = jnp.zeros_like(db_ref)

        v, d = r_ref[...], dx_ref[...]
        mu = jnp.mean(v, axis=1, keepdims=True)
        cen = v - mu
        var = jnp.mean(cen * cen, axis=1, keepdims=True)
        rstd = lax.rsqrt(var + LN_EPS)
        xhat = cen * rstd
        dxhat = d * g_ref[...]
        dr = rstd * (dxhat - jnp.mean(dxhat, axis=1, keepdims=True)
                     - xhat * jnp.mean(dxhat * xhat, axis=1, keepdims=True))
        drb_ref[...] = dr.astype(BF16)
        adr_ref[...] = alpha * dr
        dg_ref[...] += jnp.sum(d * xhat, axis=0, keepdims=True)
        db_ref[...] += jnp.sum(d, axis=0, keepdims=True)

    row = pl.BlockSpec((tm, D), lambda i: (i, 0))
    vec = pl.BlockSpec((1, D), lambda i: (0, 0))
    return pl.pallas_call(
        body, name="ln_bwd", grid=(S // tm,), in_specs=[row, row, vec], out_specs=[row, row, vec, vec],
        out_shape=[jax.ShapeDtypeStruct((S, D), BF16), jax.ShapeDtypeStruct((S, D), F32),
                   jax.ShapeDtypeStruct((1, D), F32), jax.ShapeDtypeStruct((1, D), F32)],
        compiler_params=_params(),
    )(dx, r, g)


def _loss(y, target):
    S, D = y.shape
    tm = _rows(S, D * 12)

    def body(y_ref, t_ref, dy_ref, sq_ref):
        @pl.when(pl.program_id(0) == 0)
        def _():
            sq_ref[...] = jnp.zeros_like(sq_ref)

        err = y_ref[...] - t_ref[...]
        dy_ref[...] = err * (1.0 / D)
        sq_ref[...] += jnp.sum(err * err, axis=0, keepdims=True)

    row = pl.BlockSpec((tm, D), lambda i: (i, 0))
    vec = pl.BlockSpec((1, D), lambda i: (0, 0))
    return pl.pallas_call(
        body, name="loss", grid=(S // tm,), in_specs=[row, row], out_specs=[row, vec],
        out_shape=[jax.ShapeDtypeStruct((S, D), F32), jax.ShapeDtypeStruct((1, D), F32)],
        compiler_params=_params(),
    )(y, target)


def _merge_fwd(up_a, up_b, hfg, b_gate):
    S, D = up_a.shape
    tm = _rows(S, D * (2 + 2 + 4 + 4 + 2))

    def body(ua_ref, ub_ref, gla_ref, glb_ref, bga_ref, bgb_ref, u_ref):
        ga = _sigmoid(gla_ref[...] + bga_ref[...])
        gb = _sigmoid(glb_ref[...] + bgb_ref[...])
        u_ref[...] = (ga * ua_ref[...].astype(F32) + gb * ub_ref[...].astype(F32)).astype(BF16)

    row = pl.BlockSpec((tm, D), lambda i: (i, 0))
    row1 = pl.BlockSpec((tm, D), lambda i: (i, 1))
    v0 = pl.BlockSpec((1, D), lambda i: (0, 0))
    v1 = pl.BlockSpec((1, D), lambda i: (0, 1))
    return pl.pallas_call(
        body, name="merge_fwd", grid=(S // tm,), in_specs=[row, row, row, row1, v0, v1], out_specs=row,
        out_shape=jax.ShapeDtypeStruct((S, D), BF16), compiler_params=_params(),
    )(up_a, up_b, hfg, hfg, b_gate, b_gate)


def _merge_bwd(du, up_a, up_b, hfg, b_gate):
    S, D = up_a.shape
    tm = _rows(S, D * (4 + 2 + 2 + 4 + 4 + 2 + 2 + 4))

    def body(du_ref, ua_ref, ub_ref, gla_ref, glb_ref, bga_ref, bgb_ref, dua_ref, dub_ref, dgl_ref, dbg_ref):
        @pl.when(pl.program_id(0) == 0)
        def _():
            dbg_ref[...] = jnp.zeros_like(dbg_ref)

        du = du_ref[...]
        ga = _sigmoid(gla_ref[...] + bga_ref[...])
        gb = _sigmoid(glb_ref[...] + bgb_ref[...])
        dua_ref[...] = (du * ga).astype(BF16)
        dub_ref[...] = (du * gb).astype(BF16)
        dgla = du * ua_ref[...].astype(F32) * ga * (1.0 - ga)
        dglb = du * ub_ref[...].astype(F32) * gb * (1.0 - gb)
        dgl_ref[:, :D] = dgla.astype(BF16)
        dgl_ref[:, D:] = dglb.astype(BF16)
        dbg_ref[:, :D] += jnp.sum(dgla, axis=0, keepdims=True)
        dbg_ref[:, D:] += jnp.sum(dglb, axis=0, keepdims=True)

    row = pl.BlockSpec((tm, D), lambda i: (i, 0))
    row1 = pl.BlockSpec((tm, D), lambda i: (i, 1))
    v0 = pl.BlockSpec((1, D), lambda i: (0, 0))
    v1 = pl.BlockSpec((1, D), lambda i: (0, 1))
    return pl.pallas_call(
        body, name="merge_bwd", grid=(S // tm,), in_specs=[row, row, row, row, row1, v0, v1],
        out_specs=[row, row, pl.BlockSpec((tm, 2 * D), lambda i: (i, 0)), pl.BlockSpec((1, 2 * D), lambda i: (0, 0))],
        out_shape=[jax.ShapeDtypeStruct((S, D), BF16), jax.ShapeDtypeStruct((S, D), BF16),
                   jax.ShapeDtypeStruct((S, 2 * D), BF16), jax.ShapeDtypeStruct((1, 2 * D), F32)],
        compiler_params=_params(),
    )(du, up_a, up_b, hfg, hfg, b_gate, b_gate)


def _gate_bwd(dg, out, h, z_block, name):
    S, W = out.shape
    H = W // HEAD_DIM
    tm = _rows(S, W * (4 + 2 + 2 + 2 + 2 + 4))

    def body(dg_ref, o_ref, z_ref, do_ref, dz_ref, dl_ref):
        z = z_ref[...].astype(F32)
        o = o_ref[...].astype(F32)
        d = dg_ref[...]
        sg = _sigmoid(z)
        dout = d * z * sg
        do_ref[...] = dout.astype(BF16)
        dz_ref[...] = (d * o * sg * (1.0 + z * (1.0 - sg))).astype(BF16)
        prod = dout * o
        for hh in range(H):
            sl = slice(hh * HEAD_DIM, (hh + 1) * HEAD_DIM)
            dl_ref[:, hh:hh + 1] = jnp.sum(prod[:, sl], axis=1, keepdims=True)

    row = pl.BlockSpec((tm, W), lambda i: (i, 0))
    return pl.pallas_call(
        body, name=name, grid=(S // tm,),
        in_specs=[row, row, pl.BlockSpec((tm, W), lambda i: (i, z_block))],
        out_specs=[row, row, pl.BlockSpec((tm, H), lambda i: (i, 0))],
        out_shape=[jax.ShapeDtypeStruct((S, W), BF16), jax.ShapeDtypeStruct((S, W), BF16),
                   jax.ShapeDtypeStruct((S, H), F32)],
        compiler_params=_params(),
    )(dg, out, h)


def _dil_mask(n):
    i = lax.broadcasted_iota(jnp.int32, (Q_BLOCK, 2 * Q_BLOCK), 0)
    j = lax.broadcasted_iota(jnp.int32, (Q_BLOCK, 2 * Q_BLOCK), 1)
    return (j >= i) & (j <= i + Q_BLOCK) & ((n > 0) | (j >= Q_BLOCK))


def _two_blocks(prev_ref, cur_ref, start):
    sl = slice(start, start + HEAD_DIM)
    return jnp.concatenate([prev_ref[:, sl], cur_ref[:, sl]], axis=0)


def _dil_fwd(hv, d, W, name):
    L = hv.shape[0]
    H = W // HEAD_DIM
    nblk = L // Q_BLOCK
    scale = HEAD_DIM ** -0.5
    nt = (((1,), (1,)), ((), ()))

    def body(prev_ref, cur_ref, o_ref, lse_ref):
        mask = _dil_mask(pl.program_id(1))
        for hh in range(H):
            sl = slice(hh * HEAD_DIM, (hh + 1) * HEAD_DIM)
            k = _two_blocks(prev_ref, cur_ref, W + hh * HEAD_DIM)
            v = _two_blocks(prev_ref, cur_ref, 2 * W + hh * HEAD_DIM)
            s = jnp.where(mask, lax.dot_general(cur_ref[:, sl], k, nt, preferred_element_type=F32) * scale, NEG)
            m = jnp.max(s, axis=1, keepdims=True)
            p = jnp.exp(s - m)
            den = jnp.sum(p, axis=1, keepdims=True)
            acc = jnp.dot(p.astype(BF16), v, preferred_element_type=F32)
            o_ref[:, sl] = (acc / den).astype(BF16)
            lse_ref[:, hh:hh + 1] = m + jnp.log(den)

    return pl.pallas_call(
        body, name=name, grid=(d, nblk),
        in_specs=[pl.BlockSpec((Q_BLOCK, 3 * W), lambda r, n: (jnp.maximum(n - 1, 0), r)),
                  pl.BlockSpec((Q_BLOCK, 3 * W), lambda r, n: (n, r))],
        out_specs=[pl.BlockSpec((Q_BLOCK, W), lambda r, n: (n, r)),
                   pl.BlockSpec((None, Q_BLOCK, H), lambda r, n: (r, n, 0))],
        out_shape=[jax.ShapeDtypeStruct((L, d * W), BF16), jax.ShapeDtypeStruct((d, L, H), F32)],
        compiler_params=_params(),
    )(hv, hv)


def _dil_combine_fwd(os, lses, h, W):
    S = h.shape[0]
    H = W // HEAD_DIM
    tm = _rows(S, W * (3 * 2 + 2 + 2 + 2) + 4 * H * 4)

    def body(o1, o2, o3, l1, l2, l3, z_ref, out_ref, lse_ref, g_ref):
        a, b, c = l1[...], l2[...], l3[...]
        m = jnp.maximum(jnp.maximum(a, b), c)
        ea, eb, ec = jnp.exp(a - m), jnp.exp(b - m), jnp.exp(c - m)
        den = ea + eb + ec
        wa, wb, wc = ea / den, eb / den, ec / den
        lse_ref[...] = m + jnp.log(den)
        for hh in range(H):
            sl = slice(hh * HEAD_DIM, (hh + 1) * HEAD_DIM)
            out = (wa[:, hh:hh + 1] * o1[:, sl].astype(F32) + wb[:, hh:hh + 1] * o2[:, sl].astype(F32)
                   + wc[:, hh:hh + 1] * o3[:, sl].astype(F32))
            z = z_ref[:, sl].astype(F32)
            out_ref[:, sl] = out.astype(BF16)
            g_ref[:, sl] = (out * z * _sigmoid(z)).astype(BF16)

    row = pl.BlockSpec((tm, W), lambda i: (i, 0))
    stat = pl.BlockSpec((tm, H), lambda i: (i, 0))
    return pl.pallas_call(
        body, name="dil_combine_fwd", grid=(S // tm,),
        in_specs=[row] * 3 + [stat] * 3 + [pl.BlockSpec((tm, W), lambda i: (i, 3))], out_specs=[row, stat, row],
        out_shape=[jax.ShapeDtypeStruct((S, W), BF16), jax.ShapeDtypeStruct((S, H), F32),
                   jax.ShapeDtypeStruct((S, W), BF16)],
        compiler_params=_params(),
    )(*os, *lses, h)


def _dil_bwd(hv, dov, lsev, dlv, d, W, name):
    L = hv.shape[0]
    H = W // HEAD_DIM
    nblk = L // Q_BLOCK
    scale = HEAD_DIM ** -0.5
    nt = (((1,), (1,)), ((), ()))
    tn = (((0,), (0,)), ((), ()))

    def body(prev_ref, cur_ref, do_ref, lse_ref, dl_ref, dq_ref, dk_ref, dv_ref, ck_ref, cv_ref):
        n = pl.program_id(1)

        @pl.when(n == 0)
        def _():
            ck_ref[...] = jnp.zeros_like(ck_ref)
            cv_ref[...] = jnp.zeros_like(cv_ref)

        @pl.when(n < nblk)
        def _():
            mask = _dil_mask(n)
            for hh in range(H):
                sl = slice(hh * HEAD_DIM, (hh + 1) * HEAD_DIM)
                q, do = cur_ref[:, sl], do_ref[:, sl]
                k = _two_blocks(prev_ref, cur_ref, W + hh * HEAD_DIM)
                v = _two_blocks(prev_ref, cur_ref, 2 * W + hh * HEAD_DIM)
                s = jnp.where(mask, lax.dot_general(q, k, nt, preferred_element_type=F32) * scale, NEG)
                p = jnp.exp(s - lse_ref[:, hh:hh + 1])
                ds = p * (lax.dot_general(do, v, nt, preferred_element_type=F32) - dl_ref[:, hh:hh + 1]) * scale
                ds_b = ds.astype(BF16)
                dq_ref[:, sl] = jnp.dot(ds_b, k, preferred_element_type=F32).astype(BF16)
                dk = lax.dot_general(ds_b, q, tn, preferred_element_type=F32)
                dv = lax.dot_general(p.astype(BF16), do, tn, preferred_element_type=F32)
                dk_ref[:, sl] = (ck_ref[:, sl] + dk[:Q_BLOCK]).astype(BF16)
                dv_ref[:, sl] = (cv_ref[:, sl] + dv[:Q_BLOCK]).astype(BF16)
                ck_ref[:, sl] = dk[Q_BLOCK:]
                cv_ref[:, sl] = dv[Q_BLOCK:]

        @pl.when(n == nblk)
        def _():
            dk_ref[...] = ck_ref[...].astype(BF16)
            dv_ref[...] = cv_ref[...].astype(BF16)

    last = nblk - 1

    qkv_prev = pl.BlockSpec((Q_BLOCK, 3 * W), lambda r, n: (jnp.clip(n - 1, 0, last), r))
    qkv_cur = pl.BlockSpec((Q_BLOCK, 3 * W), lambda r, n: (jnp.minimum(n, last), r))
    cur = pl.BlockSpec((Q_BLOCK, W), lambda r, n: (jnp.minimum(n, last), r))
    lag = pl.BlockSpec((Q_BLOCK, W), lambda r, n: (jnp.maximum(n - 1, 0), r))
    stat = pl.BlockSpec((None, Q_BLOCK, H), lambda r, n: (r, jnp.minimum(n, last), 0))
    shape = jax.ShapeDtypeStruct((L, d * W), BF16)
    return pl.pallas_call(
        body, name=name, grid=(d, nblk + 1),
        in_specs=[qkv_prev, qkv_cur, cur, stat, stat],
        out_specs=[cur, lag, lag], out_shape=[shape, shape, shape],
        scratch_shapes=[pltpu.VMEM((Q_BLOCK, W), F32), pltpu.VMEM((Q_BLOCK, W), F32)],
        compiler_params=_params(),
    )(hv, hv, dov, lsev, dlv)


def _dil_combine_bwd(dqs, dks, dvs, cos, sin):
    S, W = dqs[0].shape
    tm = _rows(S, W * (9 * 2 + 3 * 2 + 3 * 4))

    def body(q1, q2, q3, k1, k2, k3, v1, v2, v3, cos_ref, sin_ref, o_ref):
        cos_t, sin_t = cos_ref[...], -sin_ref[...]
        add3 = lambda a, b, c: a[...].astype(F32) + b[...].astype(F32) + c[...].astype(F32)
        dq = add3(q1, q2, q3)
        dk = add3(k1, k2, k3)
        for hh in range(W // HEAD_DIM):
            sl = slice(hh * HEAD_DIM, (hh + 1) * HEAD_DIM)
            tq, tk = dq[:, sl], dk[:, sl]
            o_ref[:, hh * HEAD_DIM:(hh + 1) * HEAD_DIM] = (
                tq * cos_t + pltpu.roll(tq, HEAD_DIM // 2, 1) * sin_t).astype(BF16)
            o_ref[:, W + hh * HEAD_DIM:W + (hh + 1) * HEAD_DIM] = (
                tk * cos_t + pltpu.roll(tk, HEAD_DIM // 2, 1) * sin_t).astype(BF16)
        o_ref[:, 2 * W:] = add3(v1, v2, v3).astype(BF16)

    row = pl.BlockSpec((tm, W), lambda i: (i, 0))
    tab = pl.BlockSpec((tm, HEAD_DIM), lambda i: (i, 0))
    return pl.pallas_call(
        body, name="dil_combine_bwd", grid=(S // tm,), in_specs=[row] * 9 + [tab, tab],
        out_specs=pl.BlockSpec((tm, 3 * W), lambda i: (i, 0)),
        out_shape=jax.ShapeDtypeStruct((S, 3 * W), BF16), compiler_params=_params(),
    )(*dqs, *dks, *dvs, cos, sin)


def _scan_tile(S):
    return _tile(S, 256, 8)


def _scan_fwd(hfg, bf_pad, f_block):
    S = hfg.shape[0]
    tm = _scan_tile(S)

    def body(f_ref, b_ref, c_ref, carry_ref):
        @pl.when(pl.program_id(0) == 0)
        def _():
            carry_ref[...] = jnp.zeros_like(carry_ref)

        v = f_ref[...] + b_ref[...]
        logf = jnp.minimum(v, 0.0) - jnp.log(1.0 + jnp.exp(-jnp.abs(v)))
        tri = (lax.broadcasted_iota(jnp.int32, (tm, tm), 1) <= lax.broadcasted_iota(jnp.int32, (tm, tm), 0)).astype(F32)
        c = jnp.dot(tri, logf, preferred_element_type=F32, precision=lax.Precision.HIGHEST) + carry_ref[...]
        c_ref[...] = c
        carry_ref[...] = c[tm - 1:tm, :]

    return pl.pallas_call(
        body, name="scan_fwd", grid=(S // tm,),
        in_specs=[pl.BlockSpec((tm, LANES), lambda i: (i, f_block)), pl.BlockSpec((1, LANES), lambda i: (0, 0))],
        out_specs=pl.BlockSpec((tm, LANES), lambda i: (i, 0)),
        out_shape=jax.ShapeDtypeStruct((S, LANES), F32),
        scratch_shapes=[pltpu.VMEM((1, LANES), F32)], compiler_params=_params(),
    )(hfg, bf_pad)


def _scan_bwd(dc, hfg, bf_pad, f_block, n_heads):
    S = hfg.shape[0]
    tm = _scan_tile(S)
    nt = S // tm

    def body(dc_ref, f_ref, b_ref, df_ref, db_ref, carry_ref):
        @pl.when(pl.program_id(0) == 0)
        def _():
            carry_ref[...] = jnp.zeros_like(carry_ref)
            db_ref[...] = jnp.zeros_like(db_ref)

        tri = (lax.broadcasted_iota(jnp.int32, (tm, tm), 1) >= lax.broadcasted_iota(jnp.int32, (tm, tm), 0)).astype(F32)
        dlogf = jnp.dot(tri, dc_ref[...], preferred_element_type=F32, precision=lax.Precision.HIGHEST) + carry_ref[...]
        carry_ref[...] = dlogf[0:1, :]
        v = f_ref[...] + b_ref[...]
        lane = lax.broadcasted_iota(jnp.int32, (tm, LANES), 1)
        df = jnp.where(lane < n_heads, dlogf * _sigmoid(-v), 0.0)
        df_ref[...] = df.astype(BF16)
        db_ref[...] += jnp.sum(df, axis=0, keepdims=True)

    return pl.pallas_call(
        body, name="scan_bwd", grid=(nt,),
        in_specs=[pl.BlockSpec((tm, LANES), lambda i: (nt - 1 - i, 0)),
                  pl.BlockSpec((tm, LANES), lambda i: (nt - 1 - i, f_block)),
                  pl.BlockSpec((1, LANES), lambda i: (0, 0))],
        out_specs=[pl.BlockSpec((tm, LANES), lambda i: (nt - 1 - i, 0)), pl.BlockSpec((1, LANES), lambda i: (0, 0))],
        out_shape=[jax.ShapeDtypeStruct((S, LANES), BF16), jax.ShapeDtypeStruct((1, LANES), F32)],
        scratch_shapes=[pltpu.VMEM((1, LANES), F32)], compiler_params=_params(),
    )(dc, hfg, bf_pad)


FOX_AUG = 2 * HEAD_DIM
FOX_V_ROWS = HEAD_DIM + 16
Q_ONES, K_ONES = HEAD_DIM, HEAD_DIM + 3
LOG2E = 1.4426950408889634


def _fox_prep(h, c, H):
    S = h.shape[0]
    W = H * HEAD_DIM
    tm = _rows(S, 3 * W * 2 + LANES * 4 + 2 * H * FOX_AUG * 2 + H * FOX_V_ROWS * 2, 128)
    inv_scale = HEAD_DIM ** 0.5

    def body(q_ref, k_ref, v_ref, c_ref, qa_ref, ka_ref, vt_ref):
        lane = lax.broadcasted_iota(jnp.int32, (tm, HEAD_DIM), 1)
        a = c_ref[...] * inv_scale
        for hh in range(H):
            vt_ref[hh * FOX_V_ROWS:hh * FOX_V_ROWS + HEAD_DIM, :] = (
                v_ref[:, hh * HEAD_DIM:(hh + 1) * HEAD_DIM].astype(F32).T.astype(BF16))
            vt_ref[hh * FOX_V_ROWS + HEAD_DIM:(hh + 1) * FOX_V_ROWS, :] = jnp.ones((FOX_V_ROWS - HEAD_DIM, tm), BF16)
            col = a[:, hh:hh + 1]
            hi = col.astype(BF16).astype(F32)
            mid = (col - hi).astype(BF16).astype(F32)
            lo = col - hi - mid
            piece = jnp.where(lane % 3 == 0, hi, jnp.where(lane % 3 == 1, mid, lo))
            extra_q = jnp.where(lane < 3, 1.0, jnp.where(lane < 6, piece, 0.0))
            extra_k = jnp.where(lane < 3, -piece, jnp.where(lane < 6, 1.0, 0.0))
            qa_ref[:, hh * FOX_AUG:hh * FOX_AUG + HEAD_DIM] = q_ref[:, hh * HEAD_DIM:(hh + 1) * HEAD_DIM]
            qa_ref[:, hh * FOX_AUG + HEAD_DIM:(hh + 1) * FOX_AUG] = extra_q.astype(BF16)
            ka_ref[:, hh * FOX_AUG:hh * FOX_AUG + HEAD_DIM] = k_ref[:, hh * HEAD_DIM:(hh + 1) * HEAD_DIM]
            ka_ref[:, hh * FOX_AUG + HEAD_DIM:(hh + 1) * FOX_AUG] = extra_k.astype(BF16)

    aug = pl.BlockSpec((tm, H * FOX_AUG), lambda i: (i, 0))
    return pl.pallas_call(
        body, name="fox_prep", grid=(S // tm,),
        in_specs=[pl.BlockSpec((tm, W), lambda i: (i, 4)), pl.BlockSpec((tm, W), lambda i: (i, 5)),
                  pl.BlockSpec((tm, W), lambda i: (i, 6)), pl.BlockSpec((tm, LANES), lambda i: (i, 0))],
        out_specs=[aug, aug, pl.BlockSpec((H * FOX_V_ROWS, tm), lambda i: (0, i))],
        out_shape=[jax.ShapeDtypeStruct((S, H * FOX_AUG), BF16)] * 2 + [jax.ShapeDtypeStruct((H * FOX_V_ROWS, S), BF16)],
        compiler_params=_params(),
    )(h, h, h, c)


def _causal_t(T):
    return lax.broadcasted_iota(jnp.int32, (T, T), 0) <= lax.broadcasted_iota(jnp.int32, (T, T), 1)


def _causal_pairs(n, by_query):
    if by_query:
        pairs = [(i, j) for i in range(n) for j in range(i + 1)]
    else:
        pairs = [(i, j) for j in range(n) for i in range(j, n)]
    return jnp.array([p[0] for p in pairs], jnp.int32), jnp.array([p[1] for p in pairs], jnp.int32)


def _fox_fwd(qa, ka, vt, h, H):
    S = h.shape[0]
    W = H * HEAD_DIM
    T = _tile(S, FOX_TILE, 128)
    nq = S // T
    k1 = HEAD_DIM ** -0.5 * LOG2E
    nt = (((1,), (1,)), ((), ()))

    qi, kj = _causal_pairs(nq, True)

    def body(qi_ref, kj_ref, q_ref, k_ref, vt_ref, z_ref, o_ref, g_ref, lse_ref, m_ref, acc_ref):
        i, j = qi_ref[pl.program_id(1)], kj_ref[pl.program_id(1)]

        @pl.when(j == 0)
        def _():
            m_ref[...] = jnp.full_like(m_ref, NEG)
            acc_ref[...] = jnp.zeros_like(acc_ref)

        def step(diag):
            raw = lax.dot_general(k_ref[...], q_ref[...], nt, preferred_element_type=F32)
            if diag:
                raw = jnp.where(_causal_t(T), raw, NEG)
            m_new = jnp.maximum(m_ref[...], jnp.max(raw, axis=0, keepdims=True))
            a = jnp.exp2((m_ref[...] - m_new) * k1)
            p = jnp.exp2((raw - m_new) * k1).astype(BF16)
            acc_ref[...] = a * acc_ref[...] + jnp.dot(vt_ref[...], p, preferred_element_type=F32)
            m_ref[...] = m_new

        @pl.when(j < i)
        def _():
            step(False)

        @pl.when(j == i)
        def _():
            step(True)

            acc = acc_ref[...]
            den = acc[HEAD_DIM:HEAD_DIM + 1, :]
            out = (acc[:HEAD_DIM, :] / den).T
            z = z_ref[...].astype(F32)
            o_ref[...] = out.astype(BF16)
            g_ref[...] = (out * z * _sigmoid(z)).astype(BF16)
            lse_ref[...] = m_ref[...] * k1 + jnp.log(den) * LOG2E

    out = pl.BlockSpec((T, HEAD_DIM), lambda hh, p, qi, kj: (qi[p], hh))
    grid_spec = pltpu.PrefetchScalarGridSpec(
        num_scalar_prefetch=2, grid=(H, qi.shape[0]),
        in_specs=[pl.BlockSpec((T, FOX_AUG), lambda hh, p, qi, kj: (qi[p], hh)),
                  pl.BlockSpec((T, FOX_AUG), lambda hh, p, qi, kj: (kj[p], hh)),
                  pl.BlockSpec((FOX_V_ROWS, T), lambda hh, p, qi, kj: (hh, kj[p])),
                  pl.BlockSpec((T, HEAD_DIM), lambda hh, p, qi, kj: (qi[p], 7 * H + hh))],
        out_specs=[out, out, pl.BlockSpec((None, 1, T), lambda hh, p, qi, kj: (hh, 0, qi[p]))],
        scratch_shapes=[pltpu.VMEM((1, T), F32), pltpu.VMEM((FOX_V_ROWS, T), F32)])
    return pl.pallas_call(
        body, name="fox_fwd", grid_spec=grid_spec,
        out_shape=[jax.ShapeDtypeStruct((S, W), BF16), jax.ShapeDtypeStruct((S, W), BF16),
                   jax.ShapeDtypeStruct((H, 1, S), F32)],
        compiler_params=_params(),
    )(qi, kj, qa, ka, vt, h)


def _fox_bwd(qa, ka, h, do, lse, dl, H):
    S = h.shape[0]
    W = H * HEAD_DIM
    T = _tile(S, FOX_TILE, 128)
    nq = S // T
    k1 = HEAD_DIM ** -0.5 * LOG2E
    nt = (((1,), (1,)), ((), ()))
    tn = (((0,), (0,)), ((), ()))

    qi, kj = _causal_pairs(nq, False)

    def body(qi_ref, kj_ref, q_ref, k_ref, v_ref, do_ref, lse_ref, dl_ref, dq_ref, dk_ref, dv_ref, ak_ref, av_ref):
        i, j = qi_ref[pl.program_id(1)], kj_ref[pl.program_id(1)]

        @pl.when(pl.program_id(1) == 0)
        def _():
            dq_ref[...] = jnp.zeros_like(dq_ref)

        @pl.when(i == j)
        def _():
            ak_ref[...] = jnp.zeros_like(ak_ref)
            av_ref[...] = jnp.zeros_like(av_ref)

        def step(diag):
            q, k, v, d_o = q_ref[...], k_ref[...], v_ref[...], do_ref[...]
            raw = lax.dot_general(k, q, nt, preferred_element_type=F32)
            if diag:
                raw = jnp.where(_causal_t(T), raw, NEG)
            p = jnp.exp2(raw * k1 - lse_ref[...])
            dp = lax.dot_general(v, d_o, nt, preferred_element_type=F32)
            ds = (p * (dp - dl_ref[...])).astype(BF16)
            av_ref[...] += jnp.dot(p.astype(BF16), d_o, preferred_element_type=F32)
            ak_ref[...] += jnp.dot(ds, q, preferred_element_type=F32)
            rows = pl.ds(pl.multiple_of(i * T, T), T)
            dq_ref[rows, :] += lax.dot_general(ds, k, tn, preferred_element_type=F32)

        @pl.when(i > j)
        def _():
            step(False)

        @pl.when(i == j)
        def _():
            step(True)

        @pl.when(i == nq - 1)
        def _():
            dk_ref[...] = ak_ref[...]
            dv_ref[...] = av_ref[...]

    qrow = lambda hh, p, qi, kj: (qi[p], hh)
    krow = lambda hh, p, qi, kj: (kj[p], hh)
    stat = pl.BlockSpec((None, 1, T), lambda hh, p, qi, kj: (hh, 0, qi[p]))
    grid_spec = pltpu.PrefetchScalarGridSpec(
        num_scalar_prefetch=2, grid=(H, qi.shape[0]),
        in_specs=[pl.BlockSpec((T, FOX_AUG), qrow), pl.BlockSpec((T, FOX_AUG), krow),
                  pl.BlockSpec((T, HEAD_DIM), lambda hh, p, qi, kj: (kj[p], 6 * H + hh)),
                  pl.BlockSpec((T, HEAD_DIM), qrow), stat, stat],
        out_specs=[pl.BlockSpec((S, FOX_AUG), lambda hh, p, qi, kj: (0, hh)), pl.BlockSpec((T, FOX_AUG), krow),
                   pl.BlockSpec((T, HEAD_DIM), krow)],
        scratch_shapes=[pltpu.VMEM((T, FOX_AUG), F32), pltpu.VMEM((T, HEAD_DIM), F32)])
    return pl.pallas_call(
        body, name="fox_bwd", grid_spec=grid_spec,
        out_shape=[jax.ShapeDtypeStruct((S, H * FOX_AUG), F32), jax.ShapeDtypeStruct((S, H * FOX_AUG), F32),
                   jax.ShapeDtypeStruct((S, W), F32)],
        compiler_params=_params(),
    )(qi, kj, qa, ka, h, do, lse, dl)


def _adamw_math(w, g, m, v):
    m = ADAM_B1 * m + (1.0 - ADAM_B1) * g
    v = ADAM_B2 * v + (1.0 - ADAM_B2) * (g * g)
    m_hat = m / (1.0 - ADAM_B1 ** ADAM_STEP)
    v_hat = v / (1.0 - ADAM_B2 ** ADAM_STEP)
    delta = -ADAM_LR * (m_hat / (jnp.sqrt(v_hat) + ADAM_EPS) + ADAM_WD * w)
    return delta, m, v


def _adamw(w, g_mine, g_theirs, m, v, name):
    L, R, C = w.shape
    half = L // 2
    tr = _rows(R, C * 4 * 9, 8)

    def body(c_ref, w_ref, gm_ref, gt_ref, m_ref, v_ref, g_ref, d_ref, nm_ref, nv_ref):
        g = jnp.where(pl.program_id(0) // half == c_ref[0], gm_ref[...], gt_ref[...])
        g_ref[...] = g
        d_ref[...], nm_ref[...], nv_ref[...] = _adamw_math(w_ref[...], g, m_ref[...], v_ref[...])

    blk = pl.BlockSpec((None, tr, C), lambda l, i, c: (l, i, 0))
    mine = pl.BlockSpec((None, tr, C), lambda l, i, c: (jnp.clip(l - c[0] * half, 0, half - 1), i, 0))
    theirs = pl.BlockSpec((None, tr, C), lambda l, i, c: (jnp.clip(l - (1 - c[0]) * half, 0, half - 1), i, 0))
    grid_spec = pltpu.PrefetchScalarGridSpec(
        num_scalar_prefetch=1, grid=(L, R // tr), in_specs=[blk, mine, theirs, blk, blk], out_specs=[blk] * 4)
    core = lax.axis_index("c").astype(jnp.int32).reshape(1)
    return pl.pallas_call(
        body, name=name, grid_spec=grid_spec, out_shape=[jax.ShapeDtypeStruct((L, R, C), F32)] * 4,
        compiler_params=_params(),
    )(core, w, g_mine, g_theirs, m, v)


def _place():
    x, y, c = lax.axis_index("x"), lax.axis_index("y"), lax.axis_index("c")
    return x, y, c, [(1 - x, y), (x, 1 - y), (1 - x, 1 - y)]


def _remote(src, dst, send_sems, recv_sems, k, to):
    return pltpu.make_async_remote_copy(src_ref=src, dst_ref=dst, send_sem=send_sems.at[k], recv_sem=recv_sems.at[k],
                                        device_id=to, device_id_type=MESH)


def _gather_weights(shards):
    n = len(shards)
    half = shards[0].shape[0] // 2

    def body(*refs):
        srcs, dsts = refs[:n], refs[n:2 * n]
        send_sems, recv_sems = refs[2 * n:]
        x, y, c, chips = _place()
        me = 2 * x + y
        mine, theirs = pl.ds(c * half, half), pl.ds((1 - c) * half, half)
        first = [_remote(srcs[a].at[mine], dsts[a].at[me, mine], send_sems, recv_sems, 6 * a + j, (px, py, c))
                 for a in range(n) for j, (px, py) in enumerate(chips)]
        for cp in first:
            cp.start()
        passed = []
        for a in range(n):
            for j, (px, py) in enumerate(chips):
                landed = dsts[a].at[2 * px + py, mine]
                _remote(landed, landed, send_sems, recv_sems, 6 * a + j, (px, py, c)).wait_recv()
                cp = _remote(landed, landed, send_sems, recv_sems, 6 * a + 3 + j, (x, y, 1 - c))
                cp.start()
                passed.append(cp)
        for a in range(n):
            for j, (px, py) in enumerate(chips):
                landed = dsts[a].at[2 * px + py, theirs]
                _remote(landed, landed, send_sems, recv_sems, 6 * a + 3 + j, (x, y, 1 - c)).wait_recv()
        for cp in first + passed:
            cp.wait_send()

    return pl.pallas_call(
        body, name="gather_weights", in_specs=[ANY] * n, out_specs=[ANY] * n,
        out_shape=[jax.ShapeDtypeStruct((4,) + s.shape, s.dtype) for s in shards],
        scratch_shapes=[pltpu.SemaphoreType.DMA((6 * n,)), pltpu.SemaphoreType.DMA((6 * n,))],
    )(*shards)


def _swap_other_half(parts):
    n = len(parts)
    half = parts[0].shape[1] // 2

    def body(*refs):
        srcs, dsts = refs[:n], refs[n:2 * n]
        send_sems, recv_sems = refs[2 * n:]
        x, y, c, _ = _place()
        cps = [_remote(srcs[a].at[:, pl.ds((1 - c) * half, half)], dsts[a], send_sems, recv_sems, a, (x, y, 1 - c))
               for a in range(n)]
        for cp in cps:
            cp.start()
        for cp in cps:
            cp.wait()

    return pl.pallas_call(
        body, name="grad_swap_half", in_specs=[ANY] * n, out_specs=[ANY] * n,
        out_shape=[jax.ShapeDtypeStruct((4, half) + p.shape[2:], p.dtype) for p in parts],
        scratch_shapes=[pltpu.SemaphoreType.DMA((n,)), pltpu.SemaphoreType.DMA((n,))],
    )(*parts)


def _add_half(part, got, name):
    _, half, R, C = got.shape
    tr = _rows(R, C * 2 * 3)

    def body(c_ref, p_ref, g_ref, o_ref):
        o_ref[...] = (p_ref[...].astype(F32) + g_ref[...].astype(F32)).astype(BF16)

    grid_spec = pltpu.PrefetchScalarGridSpec(
        num_scalar_prefetch=1, grid=(4, half, R // tr),
        in_specs=[pl.BlockSpec((None, None, tr, C), lambda s, l, i, c: (s, c[0] * half + l, i, 0)),
                  pl.BlockSpec((None, None, tr, C), lambda s, l, i, c: (s, l, i, 0))],
        out_specs=pl.BlockSpec((None, None, tr, C), lambda s, l, i, c: (s, l, i, 0)))
    core = lax.axis_index("c").astype(jnp.int32).reshape(1)
    return pl.pallas_call(
        body, name=name, grid_spec=grid_spec, out_shape=jax.ShapeDtypeStruct(got.shape, BF16),
        compiler_params=_params(),
    )(core, part, got)


def _scatter_to_owner(parts):
    n = len(parts)

    def body(*refs):
        srcs, dsts = refs[:n], refs[n:2 * n]
        send_sems, recv_sems, local_sems = refs[2 * n:]
        x, y, c, chips = _place()
        me = 2 * x + y
        local = [pltpu.make_async_copy(srcs[a].at[me], dsts[a].at[me], local_sems.at[a]) for a in range(n)]
        for cp in local:
            cp.start()
        sends = [_remote(srcs[a].at[2 * px + py], dsts[a].at[me], send_sems, recv_sems, 3 * a + j, (px, py, c))
                 for a in range(n) for j, (px, py) in enumerate(chips)]
        for cp in sends:
            cp.start()
        for a in range(n):
            for j, (px, py) in enumerate(chips):
                slot = dsts[a].at[2 * px + py]
                _remote(slot, slot, send_sems, recv_sems, 3 * a + j, (px, py, c)).wait_recv()
        for cp in sends:
            cp.wait_send()
        for cp in local:
            cp.wait()

    return pl.pallas_call(
        body, name="grad_scatter", in_specs=[ANY] * n, out_specs=[ANY] * n,
        out_shape=[jax.ShapeDtypeStruct(p.shape, p.dtype) for p in parts],
        scratch_shapes=[pltpu.SemaphoreType.DMA((3 * n,)), pltpu.SemaphoreType.DMA((3 * n,)),
                        pltpu.SemaphoreType.DMA((n,))],
    )(*parts)


def _sum_chips(got, name):
    _, half, R, C = got.shape
    tr = _rows(R, C * (2 * 4 + 4))

    def body(g_ref, o_ref):
        o_ref[...] = ((g_ref[0].astype(F32) + g_ref[1].astype(F32)) + g_ref[2].astype(F32)) + g_ref[3].astype(F32)

    return pl.pallas_call(
        body, name=name, grid=(half, R // tr),
        in_specs=[pl.BlockSpec((4, None, tr, C), lambda l, i: (0, l, i, 0))],
        out_specs=pl.BlockSpec((None, tr, C), lambda l, i: (l, i, 0)),
        out_shape=jax.ShapeDtypeStruct((half, R, C), F32), compiler_params=_params(),
    )(got)


def _share_halves(halves):
    n = len(halves)

    def body(*refs):
        srcs, dsts = refs[:n], refs[n:2 * n]
        send_sems, recv_sems = refs[2 * n:]
        x, y, c, _ = _place()
        cps = [_remote(srcs[a], dsts[a], send_sems, recv_sems, a, (x, y, 1 - c)) for a in range(n)]
        for cp in cps:
            cp.start()
        for cp in cps:
            cp.wait()

    return pl.pallas_call(
        body, name="grad_share_halves", in_specs=[ANY] * n, out_specs=[ANY] * n,
        out_shape=[jax.ShapeDtypeStruct(h.shape, h.dtype) for h in halves],
        scratch_shapes=[pltpu.SemaphoreType.DMA((n,)), pltpu.SemaphoreType.DMA((n,))],
    )(*halves)


def _small_allreduce_adamw(part, w, m, v):
    R = part.shape[0]
    deltas = [(dx, dy, dc) for dx in (0, 1) for dy in (0, 1) for dc in (0, 1)][1:]

    def body(p_ref, w_ref, m_ref, v_ref, g_ref, d_ref, nm_ref, nv_ref, all_ref, send_sems, recv_sems):
        x, y, c, _ = _place()
        me = 4 * x + 2 * y + c
        all_ref[me] = p_ref[...]
        cps = [_remote(p_ref, all_ref.at[me], send_sems, recv_sems, k, (x ^ dx, y ^ dy, c ^ dc))
               for k, (dx, dy, dc) in enumerate(deltas)]
        for cp in cps:
            cp.start()
        for k, (dx, dy, dc) in enumerate(deltas):
            slot = all_ref.at[4 * (x ^ dx) + 2 * (y ^ dy) + (c ^ dc)]
            _remote(slot, slot, send_sems, recv_sems, k, (x ^ dx, y ^ dy, c ^ dc)).wait_recv()
        for cp in cps:
            cp.wait_send()
        g = all_ref[0]
        for k in range(1, 8):
            g = g + all_ref[k]
        g_ref[...] = g
        d_ref[...], nm_ref[...], nv_ref[...] = _adamw_math(w_ref[...], g, m_ref[...], v_ref[...])

    vm = pl.BlockSpec(memory_space=pltpu.VMEM)
    shape = jax.ShapeDtypeStruct((R, LANES), F32)
    return pl.pallas_call(
        body, name="small_allreduce_adamw", in_specs=[vm] * 4, out_specs=[vm] * 4, out_shape=[shape] * 4,
        scratch_shapes=[pltpu.VMEM((8, R, LANES), F32), pltpu.SemaphoreType.DMA((7,)), pltpu.SemaphoreType.DMA((7,))],
    )(part, w, m, v)


def _pack_small(bf, bg, lg, lb, extra=None):
    L, H = bf.shape
    per = jnp.concatenate([jnp.pad(bf, ((0, 0), (0, LANES - H))), bg, lg, lb], axis=1)
    flat = per.reshape(-1, LANES)
    last = jnp.zeros((8 + (-flat.shape[0]) % 8, LANES), F32)
    if extra is not None:
        last = last.at[-8, 0].set(extra)
    return jnp.concatenate([flat, last], axis=0)


def _unpack_small(p, L, H, D):
    per = p[:L * (1 + 4 * D // LANES)].reshape(L, -1)
    return per[:, :H], per[:, LANES:LANES + 2 * D], per[:, LANES + 2 * D:LANES + 3 * D], per[:, LANES + 3 * D:]


def kernel(x, w_in, b_forget, b_gate, w_up_a, w_up_b, w_out, ln_g, ln_b, loss_target, m_w_in, m_b_forget, m_b_gate, m_w_up_a, m_w_up_b, m_w_out, m_ln_g, m_ln_b, v_w_in, v_b_forget, v_b_gate, v_w_up_a, v_w_up_b, v_w_out, v_ln_g, v_ln_b):
    _, S, D = x.shape
    L, _, C4 = w_in.shape
    H = b_forget.shape[1]
    W = w_up_a.shape[1]
    D4 = D // 4
    NC = 4 * C4
    assert W == H * HEAD_DIM and NC == 8 * W + H + 2 * D and L % 2 == 0 and D % LANES == 0
    alpha = float((2 * L) ** 0.25)
    f_block = 2 * D // LANES

    own = [w_in.astype(BF16), jnp.concatenate([w_up_a, w_up_b], axis=2).astype(BF16), w_out.astype(BF16)]
    gathered = _gather_weights(own)
    me = 2 * lax.axis_index("x") + lax.axis_index("y")
    shard = lambda a, s, l: jnp.where(me == s, own[a][l], gathered[a][s, l])
    w_main, w_fg, w_ua, w_ub, w_o = [], [], [], [], []
    for l in range(L):
        full = jnp.concatenate([shard(0, s, l) for s in range(4)], axis=1)
        w_main.append(full[:, :8 * W])
        w_fg.append(jnp.concatenate([full[:, 8 * W + H:], full[:, 8 * W:8 * W + H],
                                     jnp.zeros((D, LANES - H), BF16)], axis=1))
        ups = [shard(1, s, l) for s in range(4)]
        w_ua.append(jnp.concatenate([u[:, :D4] for u in ups], axis=1))
        w_ub.append(jnp.concatenate([u[:, D4:] for u in ups], axis=1))
        w_o.append(jnp.concatenate([shard(2, s, l) for s in range(4)], axis=0))

    pos = jnp.arange(S, dtype=F32)
    inv_freq = ROPE_THETA ** (-jnp.arange(HEAD_DIM // 2, dtype=F32) / (HEAD_DIM // 2))
    ang = pos[:, None] * inv_freq[None, :]
    cos = jnp.concatenate([jnp.cos(ang), jnp.cos(ang)], axis=1)
    sin = jnp.concatenate([-jnp.sin(ang), jnp.sin(ang)], axis=1)
    bf_pad = jnp.pad(b_forget, ((0, 0), (0, LANES - H)))

    xs = x[0]
    xb = xs.astype(BF16)
    saved = []
    for l in range(L):
        h = _matmul(xb, w_main[l], mode="nn", out_dtype=BF16, name="in_proj", rope=(cos, sin), rope_cols=2 * W)
        hfg = _matmul(xb, w_fg[l], mode="nn", out_dtype=F32, name="in_proj_gates")
        qkv_a = h[:, :3 * W]
        views = [qkv_a.reshape(S // d, d * 3 * W) for _, d in DILATED_PATTERNS]
        os, lses = [], []
        for (_, d), hv in zip(DILATED_PATTERNS, views):
            o, lse = _dil_fwd(hv, d, W, f"dil_fwd_d{d}")
            os.append(o.reshape(S, W))
            lses.append(lse.transpose(1, 0, 2).reshape(S, H))
        out_a, lse_a, ga = _dil_combine_fwd(os, lses, h, W)
        c = _scan_fwd(hfg, bf_pad[l:l + 1], f_block)
        qa, ka, vt = _fox_prep(h, c, H)
        out_b, gb, lse_b = _fox_fwd(qa, ka, vt, h, H)
        up_a = _matmul(ga, w_ua[l], mode="nn", out_dtype=BF16, name="up_proj")
        up_b = _matmul(gb, w_ub[l], mode="nn", out_dtype=BF16, name="up_proj")
        u = _merge_fwd(up_a, up_b, hfg, b_gate[l:l + 1])
        r = _matmul(u, w_o[l], mode="nn", out_dtype=F32, name="out_proj", acc_in=xs, acc_scale=alpha)
        saved.append((xb, h, hfg, views, out_a, lse_a, ga, qa, ka, out_b, gb, lse_b, up_a, up_b, u, r))
        xs, xb = _ln_fwd(r, ln_g[l:l + 1], ln_b[l:l + 1])

    dx, sq = _loss(xs, loss_target[0])
    loss_part = 0.5 * jnp.sum(sq) / D

    g_in, g_up, g_out, g_bf, g_bg, g_lg, g_lb = [], [], [], [], [], [], []
    for l in reversed(range(L)):
        xb, h, hfg, views, out_a, lse_a, ga, qa, ka, out_b, gb, lse_b, up_a, up_b, u, r = saved[l]
        dr, adr, dlg, dlb = _ln_bwd(dx, r, ln_g[l:l + 1], alpha)
        du = _matmul(dr, w_o[l], mode="nt", out_dtype=F32, name="out_proj_dx")
        dwo = _matmul(u, dr, mode="tn", out_dtype=F32, name="out_proj_dw")
        dua, dub, dgl, dbg = _merge_bwd(du, up_a, up_b, hfg, b_gate[l:l + 1])
        dga = _matmul(dua, w_ua[l], mode="nt", out_dtype=F32, name="up_proj_dx")
        dgb = _matmul(dub, w_ub[l], mode="nt", out_dtype=F32, name="up_proj_dx")
        dwua = _matmul(ga, dua, mode="tn", out_dtype=F32, name="up_proj_dw")
        dwub = _matmul(gb, dub, mode="tn", out_dtype=F32, name="up_proj_dw")
        do_a, dz_a, dl_a = _gate_bwd(dga, out_a, h, 3, "gate_bwd_a")
        dqs, dks, dvs = [], [], []
        for (_, d), hv in zip(DILATED_PATTERNS, views):
            stat = lambda t: t.reshape(S // d, d, H).transpose(1, 0, 2)
            dq, dk, dv = _dil_bwd(hv, do_a.reshape(S // d, d * W), stat(lse_a), stat(dl_a), d, W, f"dil_bwd_d{d}")
            dqs.append(dq.reshape(S, W))
            dks.append(dk.reshape(S, W))
            dvs.append(dv.reshape(S, W))
        dqkv_a = _dil_combine_bwd(dqs, dks, dvs, cos, sin)
        do_b, dz_b, dl_b = _gate_bwd(dgb, out_b, h, 7, "gate_bwd_b")
        dq_f, dk_f, dv_b = _fox_bwd(qa, ka, h, do_b, lse_b, dl_b.T.reshape(H, 1, S), H)
        dq_f, dk_f = dq_f.reshape(S, H, FOX_AUG), dk_f.reshape(S, H, FOX_AUG)
        dc = jnp.pad(dq_f[:, :, K_ONES] - dk_f[:, :, Q_ONES], ((0, 0), (0, LANES - H)))
        df, dbf = _scan_bwd(dc, hfg, bf_pad[l:l + 1], f_block, H)
        att_scale = HEAD_DIM ** -0.5
        dq_b = (dq_f[:, :, :HEAD_DIM] * att_scale).astype(BF16).reshape(S, W)
        dk_b = (dk_f[:, :, :HEAD_DIM] * att_scale).astype(BF16).reshape(S, W)
        dh = jnp.concatenate([dqkv_a, dz_a, dq_b, dk_b, dv_b.astype(BF16), dz_b], axis=1)
        dhfg = jnp.concatenate([dgl, df], axis=1)
        dx1 = _matmul(dh, w_main[l], mode="nt", out_dtype=F32, name="in_proj_dx", acc_in=adr)
        dx = _matmul(dhfg, w_fg[l], mode="nt", out_dtype=F32, name="in_proj_gates_dx", acc_in=dx1)
        dwm = _matmul(xb, dh, mode="tn", out_dtype=F32, name="in_proj_dw")
        dwfg = _matmul(xb, dhfg, mode="tn", out_dtype=F32, name="in_proj_gates_dw")
        full = jnp.concatenate([dwm, dwfg[:, 2 * D:2 * D + H], dwfg[:, :2 * D]], axis=1)
        g_in.append(full.reshape(D, 4, C4).transpose(1, 0, 2).astype(BF16))
        g_up.append(jnp.concatenate([dwua.reshape(W, 4, D4), dwub.reshape(W, 4, D4)], axis=2).transpose(1, 0, 2).astype(BF16))
        g_out.append(dwo.reshape(4, D4, D).astype(BF16))
        g_bf.append(dbf[0, :H])
        g_bg.append(dbg[0])
        g_lg.append(dlg[0])
        g_lb.append(dlb[0])
    grad_x = dx[None]
    for lst in (g_in, g_up, g_out, g_bf, g_bg, g_lg, g_lb):
        lst.reverse()

    parts = [jnp.stack(g_in, axis=1), jnp.stack(g_up, axis=1), jnp.stack(g_out, axis=1)]
    names = ["w_in", "w_up", "w_out"]
    got = _swap_other_half(parts)
    chip = [_add_half(p, g, f"grad_add_half_{n}") for p, g, n in zip(parts, got, names)]
    landed = _scatter_to_owner(chip)
    halves = [_sum_chips(g, f"grad_sum_chips_{n}") for g, n in zip(landed, names)]
    theirs = _share_halves(halves)

    pair = lambda a, b: jnp.concatenate([a, b], axis=2)
    grad_w_in, d_in, nm_in, nv_in = _adamw(w_in, halves[0], theirs[0], m_w_in, v_w_in, "adamw_w_in")
    up = _adamw(pair(w_up_a, w_up_b), halves[1], theirs[1], pair(m_w_up_a, m_w_up_b), pair(v_w_up_a, v_w_up_b),
                "adamw_w_up")
    (grad_w_up_a, grad_w_up_b), (d_ua, d_ub), (nm_ua, nm_ub), (nv_ua, nv_ub) = [
        (t[:, :, :D4], t[:, :, D4:]) for t in up]
    grad_w_out, d_o, nm_o, nv_o = _adamw(w_out, halves[2], theirs[2], m_w_out, v_w_out, "adamw_w_out")

    small_g = _pack_small(jnp.stack(g_bf), jnp.stack(g_bg), jnp.stack(g_lg), jnp.stack(g_lb), loss_part)
    small = _small_allreduce_adamw(small_g, _pack_small(b_forget, b_gate, ln_g, ln_b),
                                   _pack_small(m_b_forget, m_b_gate, m_ln_g, m_ln_b),
                                   _pack_small(v_b_forget, v_b_gate, v_ln_g, v_ln_b))
    loss = small[0][-8, 0]
    (g_bf, g_bg, g_lg, g_lb), (d_bf, d_bg, d_lg, d_lb), (nm_bf, nm_bg, nm_lg, nm_lb), (nv_bf, nv_bg, nv_lg, nv_lb) = [
        _unpack_small(p, L, H, D) for p in small]

    return (loss, grad_x,
            grad_w_in, g_bf, g_bg, grad_w_up_a, grad_w_up_b, grad_w_out, g_lg, g_lb,
            d_in, d_bf, d_bg, d_ua, d_ub, d_o, d_lg, d_lb,
            nm_in, nm_bf, nm_bg, nm_ua, nm_ub, nm_o, nm_lg, nm_lb,
            nv_in, nv_bf, nv_bg, nv_ua, nv_ub, nv_o, nv_lg, nv_lb)
```
